```python
import math
import jax
import jax.numpy as jnp
from jax import lax
import numpy as np

D_MODEL = 1024
BATCH = 16
SEQ = 4096
DEPTH = 1

CTX_LEN = 256
GRID_W = 64
MIX_W = D_MODEL
HEAD_DIM = 64
ATT_W = MIX_W // 2
HYENA_W = MIX_W - ATT_W
N_HEADS = ATT_W // HEAD_DIM
N_KV_HEADS = 2
GQA_GROUP = N_HEADS // N_KV_HEADS
KV_W = N_KV_HEADS * HEAD_DIM
ATT_SCALE = HEAD_DIM ** -0.5
WINDOW = 128
Q_BLOCK = 128
NEG_INF = -1e30
ROPE_THETA = 10000.0
ROPE_PAIRS = HEAD_DIM // 4
HYENA_ORDER = 2
HYENA_PROJ_W = (HYENA_ORDER + 1) * HYENA_W
HYENA_BANDS = 16
HYENA_EMB = 2 * HYENA_BANDS + 1
HYENA_FILTER_HIDDEN = 64
SHORT_CONV = 3
DECAY_TARGET = 1e-2
FAST_DECAY_PCT = 0.3
SLOW_DECAY_PCT = 1.5
K_OFF = ATT_W
V_OFF = ATT_W + KV_W
HY_OFF = ATT_W + 2 * KV_W
PROJ_W = HY_OFF + HYENA_PROJ_W
N_GROUPS = 4
EXPERTS_PER_GROUP = 8
N_EXPERTS = N_GROUPS * EXPERTS_PER_GROUP
EXPERT_HIDDEN = D_MODEL // 2
TOP_K = 2
MOE_BLOCK = 128
EPS = 1e-6

kernel_name = 'hymba_swa_hyena_hmoe_dit_block'


def rms_norm(x, w):
    xf = x.astype(jnp.float32)
    y = xf * lax.rsqrt(jnp.mean(xf * xf, axis=-1, keepdims=True) + EPS)
    return y.astype(x.dtype) * w


def modulate(h, shift, scale):
    return h * (1.0 + scale) + shift


def split_heads(t, n_heads):
    return t.reshape(t.shape[:-1] + (n_heads, HEAD_DIM))


def rotate_pairs(x, ang):
    m = x.shape[-1] // 2
    cos = jnp.cos(ang).astype(x.dtype)
    sin = jnp.sin(ang).astype(x.dtype)
    x1, x2 = x[..., :m], x[..., m:]
    return jnp.concatenate([x1 * cos - x2 * sin, x2 * cos + x1 * sin], axis=-1)


def rope_2d(x, ang_r, ang_c):
    half = HEAD_DIM // 2
    return jnp.concatenate([rotate_pairs(x[..., :half], ang_r), rotate_pairs(x[..., half:], ang_c)], axis=-1)


def short_conv(u, w, b):
    seq_len = u.shape[1]
    half = SHORT_CONV // 2
    up = jnp.pad(u, ((0, 0), (half, half), (0, 0)))
    out = b
    for j in range(SHORT_CONV):
        out = out + up[:, j:j + seq_len] * w[j]
    return out


def implicit_filter_spectrum(seq_len, w1, b1, w2, b2, w3, sin_freq):
    pos = jnp.arange(seq_len, dtype=jnp.float32)
    t = pos / seq_len
    bands = jnp.linspace(1e-4, HYENA_BANDS - 1, HYENA_BANDS, dtype=jnp.float32)
    ang = (2.0 * math.pi / seq_len) * pos[:, None] * bands[None, :]
    feat = jnp.concatenate([t[:, None], jnp.cos(ang), -jnp.sin(ang)], axis=-1).astype(w1.dtype)
    h = jnp.sin(sin_freq[0] * (feat @ w1 + b1))
    h = jnp.sin(sin_freq[1] * (h @ w2 + b2))
    h = (h @ w3).astype(jnp.float32).reshape(seq_len, HYENA_ORDER, 2, HYENA_W)
    max_decay = math.log(DECAY_TARGET) / FAST_DECAY_PCT
    min_decay = math.log(DECAY_TARGET) / SLOW_DECAY_PCT
    deltas = jnp.abs(jnp.linspace(min_decay, max_decay, HYENA_W, dtype=jnp.float32))
    h = h * jnp.exp(-t[:, None] * deltas[None, :])[:, None, None, :]
    fwd, bwd = h[:, :, 0], h[:, :, 1]
    kern = jnp.concatenate([fwd, jnp.zeros((1, HYENA_ORDER, HYENA_W), jnp.float32), bwd[1:][::-1]], axis=0)
    return jnp.fft.rfft(kern, axis=0)


def hyena_mixer(u, conv_w, conv_b, w1, b1, w2, b2, w3, sin_freq, skip):
    seq_len = u.shape[1]
    parts = jnp.split(short_conv(u, conv_w, conv_b), HYENA_ORDER + 1, axis=-1)
    spec = implicit_filter_spectrum(seq_len, w1, b1, w2, b2, w3, sin_freq)
    z = parts[0].astype(jnp.float32)
    for o in range(HYENA_ORDER):
        zf = jnp.fft.rfft(z, n=2 * seq_len, axis=1)
        conv = jnp.fft.irfft(zf * spec[None, :, o], n=2 * seq_len, axis=1)[:, :seq_len]
        z = parts[o + 1].astype(jnp.float32) * (conv + skip[o].astype(jnp.float32) * z)
    return z.astype(u.dtype)


def windowed_attention(q, k, v, kc, vc, sink):
    bsz, seq_len = q.shape[:2]
    n_blocks = seq_len // Q_BLOCK
    band = Q_BLOCK + 2 * WINDOW
    pad = ((0, 0), (WINDOW, WINDOW), (0, 0), (0, 0))
    k_pad = jnp.pad(k, pad)
    v_pad = jnp.pad(v, pad)
    qg = q.reshape(bsz, seq_len, N_KV_HEADS, GQA_GROUP, HEAD_DIM)
    n_ctx = kc.shape[1]
    s_sink = jnp.broadcast_to(sink.reshape(N_KV_HEADS, GQA_GROUP, 1, 1).astype(jnp.float32),
                              (bsz, N_KV_HEADS, GQA_GROUP, Q_BLOCK, 1))

    def block(i):
        q0 = i * Q_BLOCK
        qb = lax.dynamic_slice_in_dim(qg, q0, Q_BLOCK, axis=1)
        kb = lax.dynamic_slice_in_dim(k_pad, q0, band, axis=1)
        vb = lax.dynamic_slice_in_dim(v_pad, q0, band, axis=1)
        q_pos = q0 + jnp.arange(Q_BLOCK)
        k_pos = q0 - WINDOW + jnp.arange(band)
        valid = ((jnp.abs(k_pos[None, :] - q_pos[:, None]) <= WINDOW)
                 & (k_pos[None, :] >= 0) & (k_pos[None, :] < seq_len))
        s_band = jnp.einsum('bqkgd,bskd->bkgqs', qb, kb).astype(jnp.float32) * ATT_SCALE
        s_band = jnp.where(valid, s_band, NEG_INF)
        s_ctx = jnp.einsum('bqkgd,bckd->bkgqc', qb, kc).astype(jnp.float32) * ATT_SCALE
        p = jax.nn.softmax(jnp.concatenate([s_sink, s_ctx, s_band], axis=-1), axis=-1).astype(v.dtype)
        o = (jnp.einsum('bkgqc,bckd->bqkgd', p[..., 1:1 + n_ctx], vc)
             + jnp.einsum('bkgqs,bskd->bqkgd', p[..., 1 + n_ctx:], vb))
        return o.reshape(bsz, Q_BLOCK, ATT_W)

    out = lax.map(block, jnp.arange(n_blocks))
    return out.transpose(1, 0, 2, 3).reshape(bsz, seq_len, ATT_W)


def context_attention(qc, kc, vc, sink):
    bsz, n_ctx = qc.shape[:2]
    qg = qc.reshape(bsz, n_ctx, N_KV_HEADS, GQA_GROUP, HEAD_DIM)
    s = jnp.einsum('bqkgd,bckd->bkgqc', qg, kc).astype(jnp.float32) * ATT_SCALE
    s_sink = jnp.broadcast_to(sink.reshape(N_KV_HEADS, GQA_GROUP, 1, 1).astype(jnp.float32),
                              (bsz, N_KV_HEADS, GQA_GROUP, n_ctx, 1))
    p = jax.nn.softmax(jnp.concatenate([s_sink, s], axis=-1), axis=-1).astype(vc.dtype)
    o = jnp.einsum('bkgqc,bckd->bqkgd', p[..., 1:], vc)
    return o.reshape(bsz, n_ctx, ATT_W)


def hierarchical_moe(h, router_g_w, router_g_b, router_e_w, router_e_b, exp_w1, exp_w3, exp_w2):
    bsz, seq_len, d = h.shape
    n_tok = bsz * seq_len
    xt = h.reshape(n_tok, d)
    g_prob = jax.nn.softmax((xt @ router_g_w + router_g_b).astype(jnp.float32), axis=-1)
    g_val, g_idx = lax.top_k(g_prob, 1)
    e_logits = (xt @ router_e_w + router_e_b).astype(jnp.float32).reshape(n_tok, N_GROUPS, EXPERTS_PER_GROUP)
    e_logits = jnp.take_along_axis(e_logits, g_idx[:, :, None], axis=1)[:, 0]
    e_val, e_idx = lax.top_k(jax.nn.softmax(e_logits, axis=-1), TOP_K)
    gate = g_val * e_val / jnp.sum(e_val, axis=-1, keepdims=True)
    expert = g_idx * EXPERTS_PER_GROUP + e_idx
    n_assign = n_tok * TOP_K
    e_flat = expert.reshape(n_assign)
    order = jnp.argsort(e_flat)
    e_sorted = e_flat[order]
    tok_sorted = order // TOP_K
    gate_sorted = gate.reshape(n_assign)[order]
    counts = jnp.bincount(e_flat, length=N_EXPERTS)
    starts = jnp.cumsum(counts) - counts
    padded = (counts + MOE_BLOCK - 1) // MOE_BLOCK * MOE_BLOCK
    padded_end = jnp.cumsum(padded)
    padded_start = padded_end - padded
    dest = padded_start[e_sorted] + jnp.arange(n_assign) - starts[e_sorted]
    n_blocks = -(-n_assign // MOE_BLOCK) + N_EXPERTS
    buf = jnp.zeros((n_blocks * MOE_BLOCK, d), h.dtype).at[dest].set(xt[tok_sorted])
    block_expert = jnp.minimum(
        jnp.searchsorted(padded_end, jnp.arange(n_blocks) * MOE_BLOCK, side='right'), N_EXPERTS - 1)

    def expert_block(args):
        xb, e = args
        return (jax.nn.silu(xb @ exp_w1[e]) * (xb @ exp_w3[e])) @ exp_w2[e]

    y_buf = lax.map(expert_block, (buf.reshape(n_blocks, MOE_BLOCK, d), block_expert))
    y = y_buf.reshape(n_blocks * MOE_BLOCK, d)[dest] * gate_sorted[:, None].astype(h.dtype)
    out = jnp.zeros((n_tok, d), h.dtype).at[tok_sorted].add(y)
    return out.reshape(bsz, seq_len, d)


def trunk_layer(x, xc, mod_lat, mod_ctx, ang_r, ang_c, norm1_w, w_in, q_norm_w, k_norm_w, attn_sink,
                hy_conv_w, hy_conv_b, hy_w1, hy_b1, hy_w2, hy_b2, hy_w3, hy_sin_freq, hy_skip,
                group_norm_w, w_out, norm2_w, router_g_w, router_g_b, router_e_w, router_e_b,
                exp_w1, exp_w3, exp_w2, update_ctx):
    sh1, sc1, g1, sh2, sc2, g2 = jnp.split(mod_lat[:, None, :], 6, axis=-1)
    csh1, csc1, cg1, csh2, csc2, cg2 = jnp.split(mod_ctx, 6, axis=-1)
    h = modulate(rms_norm(x, norm1_w), sh1, sc1)
    hc = modulate(rms_norm(xc, norm1_w), csh1, csc1)
    proj = h @ w_in
    if update_ctx:
        proj_c = hc @ w_in
        kv_c = proj_c[..., K_OFF:HY_OFF]
    else:
        kv_c = hc @ w_in[:, K_OFF:HY_OFF]
    kc = rms_norm(split_heads(kv_c[..., :KV_W], N_KV_HEADS), k_norm_w)
    vc = split_heads(kv_c[..., KV_W:], N_KV_HEADS)
    q = rope_2d(rms_norm(split_heads(proj[..., :K_OFF], N_HEADS), q_norm_w), ang_r, ang_c)
    k = rope_2d(rms_norm(split_heads(proj[..., K_OFF:V_OFF], N_KV_HEADS), k_norm_w), ang_r, ang_c)
    v = split_heads(proj[..., V_OFF:HY_OFF], N_KV_HEADS)
    att = windowed_attention(q, k, v, kc, vc, attn_sink)
    hy = hyena_mixer(proj[..., HY_OFF:], hy_conv_w, hy_conv_b, hy_w1, hy_b1, hy_w2, hy_b2, hy_w3,
                     hy_sin_freq, hy_skip)
    mix = jnp.concatenate([rms_norm(att, group_norm_w[:ATT_W]), rms_norm(hy, group_norm_w[ATT_W:])],
                          axis=-1) @ w_out
    x = x + g1 * mix
    x = x + g2 * hierarchical_moe(modulate(rms_norm(x, norm2_w), sh2, sc2), router_g_w, router_g_b,
                                  router_e_w, router_e_b, exp_w1, exp_w3, exp_w2)
    if update_ctx:
        qc = rms_norm(split_heads(proj_c[..., :K_OFF], N_HEADS), q_norm_w)
        att_c = context_attention(qc, kc, vc, attn_sink)
        hy_c = hyena_mixer(proj_c[..., HY_OFF:], hy_conv_w, hy_conv_b, hy_w1, hy_b1, hy_w2, hy_b2, hy_w3,
                           hy_sin_freq, hy_skip)
        mix_c = jnp.concatenate([rms_norm(att_c, group_norm_w[:ATT_W]), rms_norm(hy_c, group_norm_w[ATT_W:])],
                                axis=-1) @ w_out
        xc = xc + cg1 * mix_c
        xc = xc + cg2 * hierarchical_moe(modulate(rms_norm(xc, norm2_w), csh2, csc2), router_g_w, router_g_b,
                                         router_e_w, router_e_b, exp_w1, exp_w3, exp_w2)
    return x, xc


def setup_inputs(seed: int = 0) -> dict:
    key = jax.random.key(seed)
    ks = jax.random.split(key, 30)

    def nrm(k, shape, scale):
        return jax.random.normal(k, shape, jnp.float32) * scale

    hy_out = HYENA_ORDER * 2 * HYENA_W
    return {
        'x': nrm(ks[0], (BATCH, SEQ, D_MODEL), 1.0),
        'c': nrm(ks[1], (BATCH, D_MODEL), 1.0),
        'ctx': nrm(ks[2], (BATCH, CTX_LEN, D_MODEL), 1.0),
        'c_ctx': nrm(ks[3], (D_MODEL,), 1.0),
        'ada_w': nrm(ks[4], (DEPTH, D_MODEL, 6 * D_MODEL), 0.5 * D_MODEL ** -0.5),
        'ada_b': nrm(ks[5], (DEPTH, 6 * D_MODEL), 0.02),
        'norm1_w': 1.0 + nrm(ks[6], (DEPTH, D_MODEL), 0.05),
        'w_in': nrm(ks[7], (DEPTH, D_MODEL, PROJ_W), D_MODEL ** -0.5),
        'q_norm_w': 1.0 + nrm(ks[8], (DEPTH, HEAD_DIM), 0.05),
        'k_norm_w': 1.0 + nrm(ks[9], (DEPTH, HEAD_DIM), 0.05),
        'attn_sink': nrm(ks[10], (DEPTH, N_HEADS), 0.5),
        'hy_conv_w': nrm(ks[11], (DEPTH, SHORT_CONV, HYENA_PROJ_W), SHORT_CONV ** -0.5),
        'hy_conv_b': nrm(ks[12], (DEPTH, HYENA_PROJ_W), 0.02),
        'hy_w1': nrm(ks[13], (DEPTH, HYENA_EMB, HYENA_FILTER_HIDDEN), HYENA_EMB ** -0.5),
        'hy_b1': nrm(ks[14], (DEPTH, HYENA_FILTER_HIDDEN), 0.1),
        'hy_w2': nrm(ks[15], (DEPTH, HYENA_FILTER_HIDDEN, HYENA_FILTER_HIDDEN), HYENA_FILTER_HIDDEN ** -0.5),
        'hy_b2': nrm(ks[16], (DEPTH, HYENA_FILTER_HIDDEN), 0.1),
        'hy_w3': nrm(ks[17], (DEPTH, HYENA_FILTER_HIDDEN, hy_out), 0.05 * HYENA_FILTER_HIDDEN ** -0.5),
        'hy_sin_freq': 1.0 + nrm(ks[18], (DEPTH, 2, HYENA_FILTER_HIDDEN), 0.05),
        'hy_skip': nrm(ks[19], (DEPTH, HYENA_ORDER, HYENA_W), 1.0),
        'group_norm_w': 1.0 + nrm(ks[20], (DEPTH, MIX_W), 0.05),
        'w_out': nrm(ks[21], (DEPTH, MIX_W, D_MODEL), MIX_W ** -0.5),
        'norm2_w': 1.0 + nrm(ks[22], (DEPTH, D_MODEL), 0.05),
        'router_g_w': nrm(ks[23], (DEPTH, D_MODEL, N_GROUPS), D_MODEL ** -0.5),
        'router_g_b': nrm(ks[24], (DEPTH, N_GROUPS), 0.01),
        'router_e_w': nrm(ks[25], (DEPTH, D_MODEL, N_EXPERTS), D_MODEL ** -0.5),
        'router_e_b': nrm(ks[26], (DEPTH, N_EXPERTS), 0.01),
        'exp_w1': nrm(ks[27], (DEPTH, N_EXPERTS, D_MODEL, EXPERT_HIDDEN), D_MODEL ** -0.5),
        'exp_w3': nrm(ks[28], (DEPTH, N_EXPERTS, D_MODEL, EXPERT_HIDDEN), D_MODEL ** -0.5),
        'exp_w2': nrm(ks[29], (DEPTH, N_EXPERTS, EXPERT_HIDDEN, D_MODEL), EXPERT_HIDDEN ** -0.5),
    }


def reference(x, c, ctx, c_ctx, ada_w, ada_b, norm1_w, w_in, q_norm_w, k_norm_w, attn_sink,
              hy_conv_w, hy_conv_b, hy_w1, hy_b1, hy_w2, hy_b2, hy_w3, hy_sin_freq, hy_skip,
              group_norm_w, w_out, norm2_w, router_g_w, router_g_b, router_e_w, router_e_b,
              exp_w1, exp_w3, exp_w2):
    seq_len = x.shape[1]
    n_rows = seq_len // GRID_W
    rows = jnp.repeat(jnp.arange(n_rows, dtype=jnp.float32), GRID_W)
    cols = jnp.tile(jnp.arange(GRID_W, dtype=jnp.float32), n_rows)
    inv_freq = ROPE_THETA ** (-jnp.arange(ROPE_PAIRS, dtype=jnp.float32) / ROPE_PAIRS)
    ang_r = (rows[:, None] * inv_freq[None, :])[:, None, :]
    ang_c = (cols[:, None] * inv_freq[None, :])[:, None, :]
    xc = ctx
    for layer in range(DEPTH):
        mod_lat = jax.nn.silu(c) @ ada_w[layer] + ada_b[layer]
        mod_ctx = jax.nn.silu(c_ctx) @ ada_w[layer] + ada_b[layer]
        x, xc = trunk_layer(x, xc, mod_lat, mod_ctx, ang_r, ang_c, norm1_w[layer], w_in[layer],
                            q_norm_w[layer], k_norm_w[layer], attn_sink[layer], hy_conv_w[layer],
                            hy_conv_b[layer], hy_w1[layer], hy_b1[layer], hy_w2[layer], hy_b2[layer],
                            hy_w3[layer], hy_sin_freq[layer], hy_skip[layer], group_norm_w[layer],
                            w_out[layer], norm2_w[layer], router_g_w[layer], router_g_b[layer],
                            router_e_w[layer], router_e_b[layer], exp_w1[layer], exp_w3[layer],
                            exp_w2[layer], layer < DEPTH - 1)
    return x
```

```python
import functools
import math

import ml_dtypes
import numpy as np

import jax
import jax.numpy as jnp
from jax import lax
from jax.experimental import pallas as pl
from jax.experimental.pallas import tpu as pltpu

F32, BF16, I32 = jnp.float32, jnp.bfloat16, jnp.int32
SDS = jax.ShapeDtypeStruct
BS = pl.BlockSpec

HEAD_DIM = 64
N_KV_HEADS = 2
WINDOW = 128
Q_BLOCK = 128
GRID_W = 64
ROPE_THETA = 10000.0
EPS = 1e-6
NEG_INF = -1e30
HYENA_BANDS = 16
HYENA_ORDER = 2
DECAY_TARGET = 1e-2
FAST_DECAY_PCT = 0.3
SLOW_DECAY_PCT = 1.5
N_GROUPS = 4
EXPERTS_PER_GROUP = 8
N_EXPERTS = N_GROUPS * EXPERTS_PER_GROUP
TOP_K = 2

LANES = 128
SUBLANES = 8
VMEM_LIMIT = 56 * 1024 * 1024

ROW_TILE = 512
MOE_BLOCK = 256
GATHER_TILE = 256
FILTER_TILE = 512

FFT_MINOR = 128
FFT_MAJOR = 64
N_SLABS = FFT_MAJOR // 2 + 1


def _cparams(n_grid):
    return pltpu.CompilerParams(dimension_semantics=("arbitrary",) * n_grid, vmem_limit_bytes=VMEM_LIMIT)


def _mm(a, b):
    return jnp.dot(a, b, preferred_element_type=F32)


def _mm_nt(a, b):
    return lax.dot_general(a, b, (((1,), (1,)), ((), ())), preferred_element_type=F32)


def _split(x):
    hi = x.astype(BF16)
    lo = (x - hi.astype(F32)).astype(BF16)
    return hi, lo


def _mm_split(w_hi, w_lo, x, passes):
    x_hi, x_lo = _split(x)
    out = _mm(w_hi, x_hi) + _mm(w_hi, x_lo)
    if passes == 3:
        out = out + _mm(w_lo, x_hi)
    return out


def _mm3(a, b):
    a_hi, a_lo = _split(a)
    b_hi, b_lo = _split(b)
    return _mm(a_hi, b_hi) + _mm(a_hi, b_lo) + _mm(a_lo, b_hi)


def _silu(x):
    return x * (1.0 / (1.0 + jnp.exp(-x)))


def _rms(x):
    return x * lax.rsqrt(jnp.mean(x * x, axis=-1, keepdims=True) + EPS)


def _np_split(m):
    hi = m.astype(np.float32).astype(ml_dtypes.bfloat16)
    lo = (m - hi.astype(np.float64)).astype(np.float32).astype(ml_dtypes.bfloat16)
    return hi, lo


@functools.lru_cache(maxsize=None)
def _rope_tables(seq_len):
    pos = np.arange(seq_len)
    rows = (pos // GRID_W).astype(np.float64)
    cols = (pos % GRID_W).astype(np.float64)
    pairs = HEAD_DIM // 4
    inv_freq = ROPE_THETA ** (-np.arange(pairs, dtype=np.float64) / pairs)
    ang_r = rows[:, None] * inv_freq[None, :]
    ang_c = cols[:, None] * inv_freq[None, :]
    ang = np.concatenate([ang_r, ang_r, ang_c, ang_c], axis=1)
    sign = np.tile(np.concatenate([-np.ones(pairs), np.ones(pairs)]), 2)
    cos = np.cos(ang)
    sin = np.sin(ang) * sign[None, :]
    reps = LANES // HEAD_DIM
    return np.tile(cos, (1, reps)).astype(np.float32), np.tile(sin, (1, reps)).astype(np.float32)


@functools.lru_cache(maxsize=None)
def _block_diag_mean(width):
    h = np.arange(width) // HEAD_DIM
    return ((h[:, None] == h[None, :]).astype(np.float32) / HEAD_DIM).astype(ml_dtypes.bfloat16)


@functools.lru_cache(maxsize=None)
def _filter_features(seq_len, width):
    pos = np.arange(seq_len, dtype=np.float64)
    t = pos / seq_len
    bands = np.linspace(1e-4, HYENA_BANDS - 1, HYENA_BANDS)
    ang = (2.0 * math.pi / seq_len) * pos[:, None] * bands[None, :]
    feat = np.concatenate([t[:, None], np.cos(ang), -np.sin(ang)], axis=-1)
    out = np.zeros((seq_len, width), np.float32)
    out[:, :feat.shape[1]] = feat
    return out


@functools.lru_cache(maxsize=None)
def _decay_rates(width):
    max_decay = math.log(DECAY_TARGET) / FAST_DECAY_PCT
    min_decay = math.log(DECAY_TARGET) / SLOW_DECAY_PCT
    return np.abs(np.linspace(min_decay, max_decay, width)).astype(np.float32)


@functools.lru_cache(maxsize=None)
def _dft_tables():
    n = FFT_MAJOR * FFT_MINOR
    half = FFT_MAJOR // 2
    n2 = np.arange(half)
    a = np.zeros((FFT_MAJOR, half))
    a[:half] = np.cos(2 * np.pi * np.arange(half)[:, None] * n2[None, :] / FFT_MAJOR)
    a[half] = (-1.0) ** n2
    k2 = np.arange(1, half)
    a[half + 1:] = -np.sin(2 * np.pi * k2[:, None] * n2[None, :] / FFT_MAJOR)
    d = np.zeros((half, FFT_MAJOR))
    d[:, 0] = 1.0 / n
    d[:, 1:half] = 2 * np.cos(2 * np.pi * n2[:, None] * k2[None, :] / FFT_MAJOR) / n
    d[:, half] = (-1.0) ** n2 / n
    d[:, half + 1:] = -2 * np.sin(2 * np.pi * n2[:, None] * k2[None, :] / FFT_MAJOR) / n
    eye = np.eye(SUBLANES)
    ak = np.kron(a, eye)
    dk = np.kron(d, eye)
    s = np.arange(N_SLABS)[:, None, None]
    k1 = np.arange(FFT_MINOR)[None, :, None]
    n1 = np.arange(FFT_MINOR)[None, None, :]
    g = np.exp(-2j * np.pi * n1 * (FFT_MAJOR * k1 + s) / n)
    mf = np.block([[g.real, -g.imag], [g.imag, g.real]])
    gi = np.conj(g).transpose(0, 2, 1)
    mi = np.block([[gi.real, -gi.imag], [gi.imag, gi.real]])
    return {name: _np_split(m) for name, m in (("ak", ak), ("dk", dk), ("mf", mf), ("mi", mi))}


def _ada_body(c_ref, w_ref, b_ref, o_ref):
    o_ref[...] = _mm3(_silu(c_ref[...]), w_ref[...]) + b_ref[...]


def _ada(cc, ada_w, ada_b):
    rows, d = cc.shape
    n = ada_w.shape[1]
    tn = 1536
    return pl.pallas_call(
        _ada_body, out_shape=SDS((rows, n), F32), grid=(n // tn,),
        in_specs=[BS((rows, d), lambda j: (0, 0)), BS((d, tn), lambda j: (0, j)), BS((1, tn), lambda j: (0, j))],
        out_specs=BS((rows, tn), lambda j: (0, j)), compiler_params=_cparams(1), name="ada_mod",
    )(cc, ada_w, ada_b.reshape(1, n))


def _head_rms(t, bd):
    hi, lo = _split(t * t)
    return t * lax.rsqrt(_mm(hi, bd) + _mm(lo, bd) + EPS)


def _rope(t, cos, sin):
    n = t.shape[1]
    quarter = HEAD_DIM // 4
    lane = lax.broadcasted_iota(I32, t.shape, 1)
    up = pltpu.roll(t, n - quarter, axis=1)
    dn = pltpu.roll(t, quarter, axis=1)
    partner = jnp.where((lane & (2 * quarter - 1)) < quarter, up, dn)
    return t * cos + partner * sin


def _inproj_body(x_ref, sh_ref, sc_ref, nw_ref, w_ref, qnw_ref, knw_ref, cos_ref, sin_ref, bdq_ref, bdk_ref,
                 q_ref, k_ref, v_ref, u_ref, *, att_w, kv_w):
    h = _rms(x_ref[0]) * nw_ref[...]
    h = h * (1.0 + sc_ref[0]) + sh_ref[0]
    proj = _mm(h.astype(BF16), w_ref[...])
    cos, sin = cos_ref[...], sin_ref[...]
    reps = att_w // LANES
    cos_q = jnp.concatenate([cos] * reps, axis=1)
    sin_q = jnp.concatenate([sin] * reps, axis=1)
    q = _head_rms(proj[:, :att_w], bdq_ref[...]) * qnw_ref[...]
    q_ref[0] = (_rope(q, cos_q, sin_q) * (HEAD_DIM ** -0.5)).astype(BF16)
    k = _head_rms(proj[:, att_w:att_w + kv_w], bdk_ref[...]) * knw_ref[...]
    k_ref[0] = _rope(k, cos, sin).astype(BF16)
    v_ref[0] = proj[:, att_w + kv_w:att_w + 2 * kv_w].astype(BF16)
    hy_off = att_w + 2 * kv_w
    for j in range(u_ref.shape[1]):
        u_ref[0, j] = proj[:, hy_off + LANES * j:hy_off + LANES * (j + 1)]


def _inproj(x, sh, sc, norm_w, w_bf, qnw, knw, att_w, kv_w):
    bsz, seq, d = x.shape
    tm = ROW_TILE
    hy_tiles = (w_bf.shape[1] - att_w - 2 * kv_w) // LANES
    cos, sin = _rope_tables(seq)
    row = lambda b, i: (b, i, 0)
    fixed2 = lambda b, i: (0, 0)
    per_b = lambda b, i: (b, 0, 0)
    return pl.pallas_call(
        functools.partial(_inproj_body, att_w=att_w, kv_w=kv_w),
        out_shape=(SDS((bsz, seq, att_w), BF16), SDS((bsz, seq, kv_w), BF16), SDS((bsz, seq, kv_w), BF16),
                   SDS((bsz, hy_tiles, seq, LANES), F32)),
        grid=(bsz, seq // tm),
        in_specs=[BS((1, tm, d), row), BS((1, 1, d), per_b), BS((1, 1, d), per_b), BS((1, d), fixed2),
                  BS(w_bf.shape, fixed2), BS((1, att_w), fixed2), BS((1, kv_w), fixed2),
                  BS((tm, LANES), lambda b, i: (i, 0)), BS((tm, LANES), lambda b, i: (i, 0)),
                  BS((att_w, att_w), fixed2), BS((kv_w, kv_w), fixed2)],
        out_specs=(BS((1, tm, att_w), row), BS((1, tm, kv_w), row), BS((1, tm, kv_w), row),
                   BS((1, hy_tiles, tm, LANES), lambda b, i: (b, 0, i, 0))),
        compiler_params=_cparams(2), name="in_proj",
    )(x, sh, sc, norm_w.reshape(1, d), w_bf, qnw, knw, jnp.asarray(cos), jnp.asarray(sin),
      jnp.asarray(_block_diag_mean(att_w)), jnp.asarray(_block_diag_mean(kv_w)))


def _ctxproj_body(x_ref, sh_ref, sc_ref, nw_ref, w_ref, knw_ref, bdk_ref, k_ref, v_ref, *, kv_w):
    h = _rms(x_ref[0]) * nw_ref[...]
    h = h * (1.0 + sc_ref[...]) + sh_ref[...]
    proj = _mm(h.astype(BF16), w_ref[...])
    k_ref[0] = (_head_rms(proj[:, :kv_w], bdk_ref[...]) * knw_ref[...]).astype(BF16)
    v_ref[0] = proj[:, kv_w:].astype(BF16)


def _ctxproj(ctx, sh, sc, norm_w, w_kv_bf, knw, kv_w):
    bsz, n_ctx, d = ctx.shape
    fixed2 = lambda b: (0, 0)
    row = lambda b: (b, 0, 0)
    return pl.pallas_call(
        functools.partial(_ctxproj_body, kv_w=kv_w),
        out_shape=(SDS((bsz, n_ctx, kv_w), BF16), SDS((bsz, n_ctx, kv_w), BF16)), grid=(bsz,),
        in_specs=[BS((1, n_ctx, d), row), BS((1, d), fixed2), BS((1, d), fixed2), BS((1, d), fixed2),
                  BS(w_kv_bf.shape, fixed2), BS((1, kv_w), fixed2), BS((kv_w, kv_w), fixed2)],
        out_specs=(BS((1, n_ctx, kv_w), row), BS((1, n_ctx, kv_w), row)),
        compiler_params=_cparams(1), name="ctx_proj",
    )(ctx, sh, sc, norm_w.reshape(1, d), w_kv_bf, knw, jnp.asarray(_block_diag_mean(kv_w)))


def _attn_body(sink_ref, q_ref, kp_ref, kc_ref, kn_ref, vp_ref, vc_ref, vn_ref, kx_ref, vx_ref, gw_ref, o_ref):
    i = pl.program_id(1)
    last = pl.num_programs(1) - 1
    n_ctx = kx_ref.shape[1]
    n_keys = n_ctx + 3 * Q_BLOCK
    r = lax.broadcasted_iota(I32, (Q_BLOCK, n_keys), 0)
    j = lax.broadcasted_iota(I32, (Q_BLOCK, n_keys), 1) - n_ctx
    in_band = (j >= r) & (j <= r + 2 * WINDOW)
    in_seq = jnp.logical_or(j >= Q_BLOCK, i > 0) & jnp.logical_or(j < 2 * Q_BLOCK, i < last)
    valid = (j < 0) | (in_band & in_seq)
    n_heads = q_ref.shape[2] // HEAD_DIM
    group = n_heads // N_KV_HEADS
    outs = []
    for g in range(N_KV_HEADS):
        sl = slice(g * HEAD_DIM, (g + 1) * HEAD_DIM)
        kb = jnp.concatenate([kx_ref[0][:, sl], kp_ref[0][:, sl], kc_ref[0][:, sl], kn_ref[0][:, sl]], axis=0)
        vb = jnp.concatenate([vx_ref[0][:, sl], vp_ref[0][:, sl], vc_ref[0][:, sl], vn_ref[0][:, sl]], axis=0)
        for hh in range(group):
            h = g * group + hh
            qh = q_ref[0][:, h * HEAD_DIM:(h + 1) * HEAD_DIM]
            s = jnp.where(valid, _mm_nt(qh, kb), NEG_INF)
            sink = sink_ref[h]
            m = jnp.maximum(jnp.max(s, axis=-1, keepdims=True), sink)
            p = jnp.exp(s - m)
            denom = jnp.sum(p, axis=-1, keepdims=True) + jnp.exp(sink - m)
            outs.append(_mm(p.astype(BF16), vb) / denom)
    att = jnp.concatenate(outs, axis=1)
    o_ref[0] = (_rms(att) * gw_ref[...]).astype(BF16)


def _attention(sink, q, k, v, kx, vx, gw):
    bsz, seq, att_w = q.shape
    kv_w = k.shape[2]
    n_ctx = kx.shape[1]
    nb = seq // Q_BLOCK
    cur = lambda b, i: (b, i, 0)
    prev = lambda b, i: (b, jnp.maximum(i - 1, 0), 0)
    nxt = lambda b, i: (b, jnp.minimum(i + 1, nb - 1), 0)
    per_b = lambda b, i: (b, 0, 0)
    kvb = (1, Q_BLOCK, kv_w)
    return pl.pallas_call(
        _attn_body, out_shape=SDS((bsz, seq, att_w), BF16), grid=(bsz, nb),
        in_specs=[BS(memory_space=pltpu.SMEM), BS((1, Q_BLOCK, att_w), cur),
                  BS(kvb, prev), BS(kvb, cur), BS(kvb, nxt), BS(kvb, prev), BS(kvb, cur), BS(kvb, nxt),
                  BS((1, n_ctx, kv_w), per_b), BS((1, n_ctx, kv_w), per_b), BS((1, att_w), lambda b, i: (0, 0))],
        out_specs=BS((1, Q_BLOCK, att_w), cur), compiler_params=_cparams(2), name="window_attn",
    )(sink, q, k, k, k, v, v, v, kx, vx, gw)


def _filter_body(f_ref, w1h, w1l, b1, w2h, w2l, b2, w3h, w3l, sf_ref, dl_ref, o_ref, *, hy_w):
    def mm_w(a, wh, wl):
        a_hi, a_lo = _split(a)
        return _mm(a_hi, wh[...]) + _mm(a_hi, wl[...]) + _mm(a_lo, wh[...])

    f = f_ref[...]
    h = jnp.sin(sf_ref[0:1, :] * (mm_w(f, w1h, w1l) + b1[...]))
    h = jnp.sin(sf_ref[1:2, :] * (mm_w(h, w2h, w2l) + b2[...]))
    h = mm_w(h, w3h, w3l)
    h = h * jnp.exp(-f[:, 0:1] * dl_ref[...])
    tl = f.shape[0]
    row = lax.broadcasted_iota(I32, h.shape, 0) + pl.program_id(0) * tl
    col = lax.broadcasted_iota(I32, h.shape, 1)
    is_bwd = ((col // hy_w) & 1) == 1
    h = jnp.where((row == 0) & is_bwd, 0.0, h)
    for j in range(o_ref.shape[0]):
        o_ref[j] = h[:, LANES * j:LANES * (j + 1)]


def _filter_mlp(seq, w1, b1, w2, b2, w3, sin_freq, hy_w):
    hid = w1.shape[1]
    fw = 64
    feat = jnp.asarray(_filter_features(seq, fw))
    w1p = jnp.zeros((fw, hid), F32).at[:w1.shape[0]].set(w1)
    n_out = w3.shape[1]
    delta = jnp.asarray(np.tile(_decay_rates(hy_w), n_out // hy_w)).reshape(1, n_out)
    tl = FILTER_TILE
    fixed = lambda i: (0, 0)
    ops = []
    for w in (w1p, w2, w3):
        ops.extend(_split(w))
    return pl.pallas_call(
        functools.partial(_filter_body, hy_w=hy_w),
        out_shape=SDS((n_out // LANES, seq, LANES), F32), grid=(seq // tl,),
        in_specs=[BS((tl, fw), lambda i: (i, 0)), BS((fw, hid), fixed), BS((fw, hid), fixed), BS((1, hid), fixed),
                  BS((hid, hid), fixed), BS((hid, hid), fixed), BS((1, hid), fixed),
                  BS((hid, n_out), fixed), BS((hid, n_out), fixed), BS((2, hid), fixed), BS((1, n_out), fixed)],
        out_specs=BS((n_out // LANES, tl, LANES), lambda i: (0, i, 0)),
        compiler_params=_cparams(1), name="hyena_filter_mlp",
    )(feat, ops[0], ops[1], b1.reshape(1, hid), ops[2], ops[3], b2.reshape(1, hid), ops[4], ops[5], sin_freq, delta)


C_TILES = 2
C_GROUP = C_TILES * LANES


def _load_cat(ref, row0, n_rows):
    return jnp.concatenate([ref[c, pl.ds(row0, n_rows), :] for c in range(C_TILES)], axis=1)


def _store_cat(ref, row0, n_rows, val):
    for c in range(C_TILES):
        ref[c, pl.ds(row0, n_rows), :] = val[:, c * LANES:(c + 1) * LANES]


def _fwd_major(src, p_ref, ak, passes):
    half = FFT_MAJOR // 2

    def body(g, carry):
        r0 = pl.multiple_of(g * SUBLANES, SUBLANES)
        st = jnp.concatenate([_load_cat(src, n2 * FFT_MINOR + r0, SUBLANES) for n2 in range(half)], axis=0)
        out = _mm_split(ak[0][...], ak[1][...] if passes == 3 else None, st, passes)
        for p in range(FFT_MAJOR):
            _store_cat(p_ref, p * FFT_MINOR + r0, SUBLANES, out[p * SUBLANES:(p + 1) * SUBLANES])
        return carry

    lax.fori_loop(0, FFT_MINOR // SUBLANES, body, 0)


def _inv_major(p_ref, dst, dk, passes):
    half = FFT_MAJOR // 2

    def body(g, carry):
        r0 = pl.multiple_of(g * SUBLANES, SUBLANES)
        st = jnp.concatenate([_load_cat(p_ref, p * FFT_MINOR + r0, SUBLANES) for p in range(FFT_MAJOR)], axis=0)
        out = _mm_split(dk[0][...], dk[1][...] if passes == 3 else None, st, passes)
        for n2 in range(half):
            _store_cat(dst, n2 * FFT_MINOR + r0, SUBLANES, out[n2 * SUBLANES:(n2 + 1) * SUBLANES])
        return carry

    lax.fori_loop(0, FFT_MINOR // SUBLANES, body, 0)


def _slab_spectrum(p_ref, mf, s, passes, real_slot=None):
    m = FFT_MINOR
    if real_slot is None:
        st = jnp.concatenate([_load_cat(p_ref, s * m, m), _load_cat(p_ref, (FFT_MAJOR // 2 + s) * m, m)], axis=0)
        x = _mm_split(mf[0][s], mf[1][s] if passes == 3 else None, st, passes)
    else:
        st = _load_cat(p_ref, real_slot * m, m)
        x = _mm_split(mf[0][s, :, 0:m], mf[1][s, :, 0:m] if passes == 3 else None, st, passes)
    return x[:m], x[m:]


def _spectrum_body(f_ref, b_ref, akh, akl, mfh, mfl, o_ref, p_ref):
    half = FFT_MAJOR // 2
    ak, mf = (akh, akl), (mfh, mfl)
    for is_bwd, src in ((False, f_ref), (True, b_ref)):
        _fwd_major(src, p_ref, ak, 3)

        def put(s, re, im, is_bwd=is_bwd):
            if is_bwd:
                o_ref[0, 0, s, 0] = o_ref[0, 0, s, 0] + re
                o_ref[0, 0, s, 1] = o_ref[0, 0, s, 1] - im
            else:
                o_ref[0, 0, s, 0] = re
                o_ref[0, 0, s, 1] = im

        put(0, *_slab_spectrum(p_ref, mf, 0, 3, real_slot=0))
        put(half, *_slab_spectrum(p_ref, mf, half, 3, real_slot=half))

        def body(s, carry):
            put(s, *_slab_spectrum(p_ref, mf, s, 3))
            return carry

        lax.fori_loop(1, half, body, 0)


def _spectrum(hf, seq, hy_w):
    t = _dft_tables()
    n_grp = hy_w // C_GROUP
    fixed2 = lambda o, c: (0, 0)
    fixed3 = lambda o, c: (0, 0, 0)
    one = pl.Buffered(1)
    return pl.pallas_call(
        _spectrum_body,
        out_shape=SDS((HYENA_ORDER, n_grp, N_SLABS, 2, FFT_MINOR, C_GROUP), F32), grid=(HYENA_ORDER, n_grp),
        in_specs=[BS((C_TILES, seq, LANES), lambda o, c: (o * 2 * n_grp + c, 0, 0), pipeline_mode=one),
                  BS((C_TILES, seq, LANES), lambda o, c: (o * 2 * n_grp + n_grp + c, 0, 0), pipeline_mode=one),
                  BS(t["ak"][0].shape, fixed2, pipeline_mode=one), BS(t["ak"][0].shape, fixed2, pipeline_mode=one),
                  BS(t["mf"][0].shape, fixed3, pipeline_mode=one), BS(t["mf"][0].shape, fixed3, pipeline_mode=one)],
        out_specs=BS((1, 1, N_SLABS, 2, FFT_MINOR, C_GROUP), lambda o, c: (o, c, 0, 0, 0, 0)),
        scratch_shapes=[pltpu.VMEM((C_TILES, FFT_MAJOR * FFT_MINOR, LANES), F32)],
        compiler_params=_cparams(2), name="hyena_spectrum",
    )(hf, hf, jnp.asarray(t["ak"][0]), jnp.asarray(t["ak"][1]), jnp.asarray(t["mf"][0]), jnp.asarray(t["mf"][1]))


CONV_CHUNK = 128
CONV_PASSES = 2


def _sconv_chunk(ref, c, r0, n_rows, seq, prm_ref, base):
    cl = slice(c * LANES, (c + 1) * LANES)
    cur = ref[c, pl.ds(r0, n_rows), :]
    row = lax.broadcasted_iota(I32, cur.shape, 0)
    before = ref[c, pl.ds(jnp.maximum(r0 - 1, 0), 1), :] * (r0 > 0).astype(F32)
    after = ref[c, pl.ds(jnp.minimum(r0 + n_rows, seq - 1), 1), :] * (r0 + n_rows < seq).astype(F32)
    prev = jnp.where(row == 0, before, pltpu.roll(cur, 1, axis=0))
    nxt = jnp.where(row == n_rows - 1, after, pltpu.roll(cur, n_rows - 1, axis=0))
    w = lambda k: prm_ref[0, 0, base + k:base + k + 1, cl]
    return w(3) + w(0) * prev + w(1) * cur + w(2) * nxt


def _conv_body(z_ref, x_ref, prm_ref, akh, dkh, mfh, mih, h_ref, o_ref, zs_ref, p_ref, *, conv_z):
    seq = z_ref.shape[2]
    half = FFT_MAJOR // 2
    m = FFT_MINOR
    zv, xv = z_ref.at[0], x_ref.at[0]
    ak, dk, mf, mi = (akh, None), (dkh, None), (mfh, None), (mih, None)
    n_chunks = seq // CONV_CHUNK

    if conv_z:
        def prep(i, carry):
            r0 = pl.multiple_of(i * CONV_CHUNK, CONV_CHUNK)
            for c in range(C_TILES):
                zs_ref[c, pl.ds(r0, CONV_CHUNK), :] = _sconv_chunk(zv, c, r0, CONV_CHUNK, seq, prm_ref, 0)
            return carry

        lax.fori_loop(0, n_chunks, prep, 0)
        src = zs_ref
    else:
        src = zv
    _fwd_major(src, p_ref, ak, CONV_PASSES)

    def slab(s, real_slot=None):
        xr, xi = _slab_spectrum(p_ref, mf, s, CONV_PASSES, real_slot)
        hr, hi = h_ref[0, 0, s, 0], h_ref[0, 0, s, 1]
        y = jnp.concatenate([xr * hr - xi * hi, xr * hi + xi * hr], axis=0)
        if real_slot is None:
            out = _mm_split(mi[0][s], None, y, CONV_PASSES)
            _store_cat(p_ref, s * m, m, out[:m])
            _store_cat(p_ref, (half + s) * m, m, out[m:])
        else:
            out = _mm_split(mi[0][s, 0:m, :], None, y, CONV_PASSES)
            _store_cat(p_ref, real_slot * m, m, out)

    slab(0, real_slot=0)
    slab(half, real_slot=half)

    def slab_loop(s, carry):
        slab(s)
        return carry

    lax.fori_loop(1, half, slab_loop, 0)
    _inv_major(p_ref, zs_ref, dk, CONV_PASSES)

    def fin(i, carry):
        r0 = pl.multiple_of(i * CONV_CHUNK, CONV_CHUNK)
        for c in range(C_TILES):
            cl = slice(c * LANES, (c + 1) * LANES)
            gate = _sconv_chunk(xv, c, r0, CONV_CHUNK, seq, prm_ref, 4)
            if conv_z:
                zval = _sconv_chunk(zv, c, r0, CONV_CHUNK, seq, prm_ref, 0)
            else:
                zval = zv[c, pl.ds(r0, CONV_CHUNK), :]
            skip = prm_ref[0, 0, 8:9, cl]
            o_ref[0, c, pl.ds(r0, CONV_CHUNK), :] = gate * (zs_ref[c, pl.ds(r0, CONV_CHUNK), :] + skip * zval)
        return carry

    lax.fori_loop(0, n_chunks, fin, 0)


def _hyena_order(z, z_grp0, x, x_grp0, prm, order, spec, conv_z, n_grp):
    bsz, _, seq, _ = z.shape
    t = _dft_tables()
    one = pl.Buffered(1)
    fixed2 = lambda c, b: (0, 0)
    fixed3 = lambda c, b: (0, 0, 0)
    blk = (1, C_TILES, seq, LANES)
    return pl.pallas_call(
        functools.partial(_conv_body, conv_z=conv_z),
        out_shape=SDS((bsz, n_grp * C_TILES, seq, LANES), F32), grid=(n_grp, bsz),
        in_specs=[BS(blk, lambda c, b: (b, z_grp0 + c, 0, 0), pipeline_mode=one),
                  BS(blk, lambda c, b: (b, x_grp0 + c, 0, 0), pipeline_mode=one),
                  BS((1, 1, 16, C_GROUP), lambda c, b: (order, c, 0, 0)),
                  BS(t["ak"][0].shape, fixed2, pipeline_mode=one), BS(t["dk"][0].shape, fixed2, pipeline_mode=one),
                  BS(t["mf"][0].shape, fixed3, pipeline_mode=one), BS(t["mi"][0].shape, fixed3, pipeline_mode=one),
                  BS((1, 1, N_SLABS, 2, FFT_MINOR, C_GROUP), lambda c, b: (order, c, 0, 0, 0, 0), pipeline_mode=one)],
        out_specs=BS(blk, lambda c, b: (b, c, 0, 0)),
        scratch_shapes=[pltpu.VMEM((C_TILES, seq, LANES), F32),
                        pltpu.VMEM((C_TILES, FFT_MAJOR * FFT_MINOR, LANES), F32)],
        compiler_params=_cparams(2), name=f"hyena_conv{order}",
    )(z, x, prm, jnp.asarray(t["ak"][0]), jnp.asarray(t["dk"][0]), jnp.asarray(t["mf"][0]),
      jnp.asarray(t["mi"][0]), spec)


ROUTER_ROWS = 48
EXPERT_ROW0 = 8


def _outproj_body(att_ref, hy_ref, gwh_ref, wo_ref, x_ref, g1_ref, sh2_ref, sc2_ref, n2w_ref, wrh_ref, wrl_ref,
                  br_ref, tri_ref, x1_ref, h2_ref, ids_ref, gates_ref, cnt_ref, carry_ref):
    @pl.when((pl.program_id(0) == 0) & (pl.program_id(1) == 0))
    def _():
        carry_ref[...] = jnp.zeros_like(carry_ref)

    hy = jnp.concatenate([hy_ref[0, j] for j in range(hy_ref.shape[1])], axis=1)
    hyn = _rms(hy) * gwh_ref[...]
    mix = _mm(jnp.concatenate([att_ref[0], hyn.astype(BF16)], axis=1), wo_ref[...])
    x1 = x_ref[0] + g1_ref[0] * mix
    x1_ref[0] = x1
    h2 = _rms(x1) * n2w_ref[...]
    h2 = h2 * (1.0 + sc2_ref[0]) + sh2_ref[0]
    h2_ref[...] = h2

    h_hi, h_lo = _split(h2)
    lg = _mm_nt(wrh_ref[...], h_hi) + _mm_nt(wrh_ref[...], h_lo) + _mm_nt(wrl_ref[...], h_hi) + br_ref[...]
    tm = lg.shape[1]
    gl = lg[0:N_GROUPS]
    el = lg[EXPERT_ROW0:EXPERT_ROW0 + N_EXPERTS]
    gmax = jnp.max(gl, axis=0, keepdims=True)
    rg = lax.broadcasted_iota(I32, gl.shape, 0).astype(F32)
    g_idx = jnp.min(jnp.where(gl == gmax, rg, float(N_GROUPS)), axis=0, keepdims=True)
    g_val = 1.0 / jnp.sum(jnp.exp(gl - gmax), axis=0, keepdims=True)
    re_i = lax.broadcasted_iota(I32, el.shape, 0)
    re = re_i.astype(F32)
    in_group = (re_i // EXPERTS_PER_GROUP).astype(F32) == g_idx
    elm = jnp.where(in_group, el, NEG_INF)
    m1 = jnp.max(elm, axis=0, keepdims=True)
    i1 = jnp.min(jnp.where(elm == m1, re, float(N_EXPERTS)), axis=0, keepdims=True)
    elm2 = jnp.where(re == i1, NEG_INF, elm)
    m2 = jnp.max(elm2, axis=0, keepdims=True)
    i2 = jnp.min(jnp.where(elm2 == m2, re, float(N_EXPERTS)), axis=0, keepdims=True)
    e2 = jnp.exp(m2 - m1)
    inv = g_val / (1.0 + e2)

    oh1 = (re == i1).astype(F32)
    oh2 = (re == i2).astype(F32)
    oh = oh1 + oh2
    base = _mm(oh.astype(BF16), tri_ref[...]) + carry_ref[:, 0:1]
    rank1 = jnp.sum(oh1 * base, axis=0, keepdims=True)
    rank2 = jnp.sum(oh2 * base, axis=0, keepdims=True)
    carry_ref[...] = carry_ref[...] + jnp.sum(oh, axis=1, keepdims=True)
    cnt_ref[...] = carry_ref[...]

    zi = jnp.zeros((1, tm), I32)
    for k, val in enumerate((i1.astype(I32), i2.astype(I32), rank1.astype(I32), rank2.astype(I32), zi, zi, zi, zi)):
        ids_ref[k:k + 1, :] = val
    zf = jnp.zeros((1, tm), F32)
    for k, val in enumerate((inv, inv * e2, zf, zf, zf, zf, zf, zf)):
        gates_ref[k:k + 1, :] = val


def _outproj_route(att_n, hy, gw_hy, wo_bf, x, g1, sh2, sc2, norm2_w, wr_hi, wr_lo, br):
    bsz, seq, d = x.shape
    tm = ROW_TILE
    n_tok = bsz * seq
    att_w = att_n.shape[2]
    hy_tiles = hy.shape[1]
    tri = jnp.asarray(np.triu(np.ones((tm, tm), np.float32), k=1).astype(ml_dtypes.bfloat16))
    row = lambda b, i: (b, i, 0)
    per_b = lambda b, i: (b, 0, 0)
    fixed2 = lambda b, i: (0, 0)
    tok = lambda b, i: (0, b * (seq // tm) + i)
    return pl.pallas_call(
        _outproj_body,
        out_shape=(SDS((bsz, seq, d), F32), SDS((n_tok, d), F32), SDS((SUBLANES, n_tok), I32),
                   SDS((SUBLANES, n_tok), F32), SDS((N_EXPERTS, LANES), F32)),
        grid=(bsz, seq // tm),
        in_specs=[BS((1, tm, att_w), row), BS((1, hy_tiles, tm, LANES), lambda b, i: (b, 0, i, 0)),
                  BS((1, hy_tiles * LANES), fixed2), BS(wo_bf.shape, fixed2), BS((1, tm, d), row),
                  BS((1, 1, d), per_b), BS((1, 1, d), per_b), BS((1, 1, d), per_b), BS((1, d), fixed2),
                  BS((ROUTER_ROWS, d), fixed2), BS((ROUTER_ROWS, d), fixed2), BS((ROUTER_ROWS, 1), fixed2),
                  BS((tm, tm), fixed2)],
        out_specs=(BS((1, tm, d), row), BS((tm, d), lambda b, i: (b * (seq // tm) + i, 0)),
                   BS((SUBLANES, tm), tok), BS((SUBLANES, tm), tok), BS((N_EXPERTS, LANES), fixed2)),
        scratch_shapes=[pltpu.VMEM((N_EXPERTS, LANES), F32)],
        compiler_params=_cparams(2), name="out_proj_router",
    )(att_n, hy, gw_hy, wo_bf, x, g1, sh2, sc2, norm2_w.reshape(1, d), wr_hi, wr_lo, br, tri)


def _row_copy(src, src_row, dst, dst_row, sem):
    return pltpu.make_async_copy(src.at[pl.ds(src_row, 1), :], dst.at[pl.ds(dst_row, 1), :], sem)


def _dispatch_body(d_ref, h_ref, zero_ref, buf_ref, sem):
    del zero_ref
    n = h_ref.shape[0]

    def issue(t, carry):
        for j in range(TOP_K):
            _row_copy(h_ref, t, buf_ref, d_ref[j, t], sem).start()
        return carry

    lax.fori_loop(0, n, issue, 0)

    def drain(t, carry):
        for j in range(TOP_K):
            _row_copy(h_ref, t, buf_ref, d_ref[j, t], sem).wait()
        return carry

    lax.fori_loop(0, n, drain, 0)


def _dispatch(dest, h2, n_rows):
    n_tok, d = h2.shape
    tm = GATHER_TILE
    return pl.pallas_call(
        _dispatch_body, out_shape=SDS((n_rows, d), F32), grid=(n_tok // tm,),
        in_specs=[BS((TOP_K, tm), lambda i: (0, i), memory_space=pltpu.SMEM), BS((tm, d), lambda i: (i, 0)),
                  BS(memory_space=pl.ANY)],
        out_specs=BS(memory_space=pl.ANY), scratch_shapes=[pltpu.SemaphoreType.DMA(())],
        input_output_aliases={2: 0}, compiler_params=_cparams(1), name="moe_dispatch",
    )(dest, h2, jnp.zeros((n_rows, d), F32))


def _expert_body(be_ref, nu_ref, x_ref, w13_ref, w2_ref, y_ref):
    del be_ref
    i = pl.program_id(0)
    hidden = w2_ref.shape[1]

    @pl.when(i < nu_ref[0])
    def _():
        h = _mm(x_ref[...].astype(BF16), w13_ref[0])
        act = _silu(h[:, :hidden]) * h[:, hidden:]
        y_ref[...] = _mm(act.astype(BF16), w2_ref[0])

    @pl.when(i >= nu_ref[0])
    def _():
        y_ref[...] = jnp.zeros_like(y_ref)


def _experts(block_expert, n_used, buf, w13_bf, w2_bf):
    n_rows, d = buf.shape
    nb = n_rows // MOE_BLOCK
    grid_spec = pltpu.PrefetchScalarGridSpec(
        num_scalar_prefetch=2, grid=(nb,),
        in_specs=[BS((MOE_BLOCK, d), lambda i, be, nu: (i, 0)),
                  BS((1,) + w13_bf.shape[1:], lambda i, be, nu: (be[i], 0, 0)),
                  BS((1,) + w2_bf.shape[1:], lambda i, be, nu: (be[i], 0, 0))],
        out_specs=BS((MOE_BLOCK, d), lambda i, be, nu: (i, 0)))
    return pl.pallas_call(_expert_body, out_shape=SDS((n_rows, d), F32), grid_spec=grid_spec,
                          compiler_params=_cparams(1), name="moe_experts")(block_expert, n_used, buf, w13_bf, w2_bf)


def _combine_body(d_ref, y_ref, x1_ref, g2_ref, gt_ref, o_ref, rows_ref, sem):
    n = x1_ref.shape[0]

    def issue(t, carry):
        for j in range(TOP_K):
            _row_copy(y_ref, d_ref[j, t], rows_ref.at[j], t, sem).start()
        return carry

    lax.fori_loop(0, n, issue, 0)

    def drain(t, carry):
        for j in range(TOP_K):
            _row_copy(y_ref, d_ref[j, t], rows_ref.at[j], t, sem).wait()
        return carry

    lax.fori_loop(0, n, drain, 0)
    gt = gt_ref[...]
    moe = gt[:, 0:1] * rows_ref[0] + gt[:, 1:2] * rows_ref[1]
    o_ref[...] = x1_ref[...] + g2_ref[0] * moe


def _combine(dest, y, x1, g2, gates_t, seq):
    n_tok, d = x1.shape
    tm = GATHER_TILE
    return pl.pallas_call(
        _combine_body, out_shape=SDS((n_tok, d), F32), grid=(n_tok // tm,),
        in_specs=[BS((TOP_K, tm), lambda i: (0, i), memory_space=pltpu.SMEM), BS(memory_space=pl.ANY),
                  BS((tm, d), lambda i: (i, 0)), BS((1, 1, d), lambda i: (i // (seq // tm), 0, 0)),
                  BS((tm, TOP_K), lambda i: (i, 0))],
        out_specs=BS((tm, d), lambda i: (i, 0)),
        scratch_shapes=[pltpu.VMEM((TOP_K, tm, d), F32), pltpu.SemaphoreType.DMA(())],
        compiler_params=_cparams(1), name="moe_combine",
    )(dest, y, x1, g2, gates_t)


def kernel(x, c, ctx, c_ctx, ada_w, ada_b, norm1_w, w_in, q_norm_w, k_norm_w, attn_sink, hy_conv_w, hy_conv_b, hy_w1,
           hy_b1, hy_w2, hy_b2, hy_w3, hy_sin_freq, hy_skip, group_norm_w, w_out, norm2_w, router_g_w, router_g_b,
           router_e_w, router_e_b, exp_w1, exp_w3, exp_w2):
    bsz, seq, d = x.shape
    assert ada_w.shape[0] == 1, "single-layer block"
    att_w = d // 2
    hy_w = d - att_w
    kv_w = N_KV_HEADS * HEAD_DIM
    n_heads = att_w // HEAD_DIM
    n_grp = hy_w // C_GROUP
    assert 2 * seq == FFT_MAJOR * FFT_MINOR and seq % ROW_TILE == 0 and hy_w % C_GROUP == 0
    assert w_in.shape[2] == att_w + 2 * kv_w + (HYENA_ORDER + 1) * hy_w

    pad = (-(bsz + 1)) % SUBLANES
    cc = jnp.concatenate([c, c_ctx[None, :], jnp.zeros((pad, d), F32)], axis=0)
    mod = _ada(cc, ada_w[0], ada_b[0])
    lat = [mod[:bsz, k * d:(k + 1) * d].reshape(bsz, 1, d) for k in range(6)]
    sh1, sc1, g1, sh2, sc2, g2 = lat
    csh1 = mod[bsz:bsz + 1, 0:d]
    csc1 = mod[bsz:bsz + 1, d:2 * d]

    w_in_bf = w_in[0].astype(BF16)
    qnw = jnp.tile(q_norm_w[0], n_heads).reshape(1, att_w)
    knw = jnp.tile(k_norm_w[0], N_KV_HEADS).reshape(1, kv_w)
    q, k, v, u = _inproj(x, sh1, sc1, norm1_w[0], w_in_bf, qnw, knw, att_w, kv_w)
    kx, vx = _ctxproj(ctx, csh1, csc1, norm1_w[0], w_in_bf[:, att_w:att_w + 2 * kv_w], knw, kv_w)

    gw = group_norm_w[0]
    att_n = _attention(attn_sink[0], q, k, v, kx, vx, gw[:att_w].reshape(1, att_w))

    hf = _filter_mlp(seq, hy_w1[0], hy_b1[0], hy_w2[0], hy_b2[0], hy_w3[0], hy_sin_freq[0], hy_w)
    spec = _spectrum(hf, seq, hy_w)
    cw = hy_conv_w[0].reshape(3, HYENA_ORDER + 1, n_grp, C_GROUP)
    cb = hy_conv_b[0].reshape(1, HYENA_ORDER + 1, n_grp, C_GROUP)
    taps = jnp.concatenate([cw, cb], axis=0)
    prm = jnp.zeros((HYENA_ORDER, n_grp, 16, C_GROUP), F32)
    for o in range(HYENA_ORDER):
        prm = prm.at[o, :, 4:8].set(jnp.transpose(taps[:, o + 1], (1, 0, 2)))
        prm = prm.at[o, :, 8].set(hy_skip[0, o].reshape(n_grp, C_GROUP))
    prm = prm.at[0, :, 0:4].set(jnp.transpose(taps[:, 0], (1, 0, 2)))
    z = _hyena_order(u, 0, u, n_grp, prm, 0, spec, True, n_grp)
    hy = _hyena_order(z, 0, u, 2 * n_grp, prm, 1, spec, False, n_grp)

    wr = jnp.zeros((ROUTER_ROWS, d), F32)
    wr = wr.at[0:N_GROUPS].set(router_g_w[0].T).at[EXPERT_ROW0:EXPERT_ROW0 + N_EXPERTS].set(router_e_w[0].T)
    br = jnp.zeros((ROUTER_ROWS, 1), F32)
    br = br.at[0:N_GROUPS, 0].set(router_g_b[0]).at[EXPERT_ROW0:EXPERT_ROW0 + N_EXPERTS, 0].set(router_e_b[0])
    wr_hi, wr_lo = _split(wr)
    x1, h2, ids, gates, counts = _outproj_route(att_n, hy, gw[att_w:].reshape(1, hy_w), w_out[0].astype(BF16), x,
                                                g1, sh2, sc2, norm2_w[0], wr_hi, wr_lo, br)

    n_tok = bsz * seq
    cnt = counts[:, 0].astype(I32)
    padded = (cnt + MOE_BLOCK - 1) // MOE_BLOCK * MOE_BLOCK
    pend = jnp.cumsum(padded)
    pstart = pend - padded
    dest = pstart[ids[0:TOP_K]] + ids[TOP_K:2 * TOP_K]
    n_blocks = (n_tok * TOP_K) // MOE_BLOCK + N_EXPERTS
    block_expert = jnp.minimum(jnp.searchsorted(pend, jnp.arange(n_blocks, dtype=I32) * MOE_BLOCK, side="right"),
                               N_EXPERTS - 1).astype(I32)
    n_used = (pend[-1:] // MOE_BLOCK).astype(I32)

    buf = _dispatch(dest, h2, n_blocks * MOE_BLOCK)
    w13 = jnp.concatenate([exp_w1[0], exp_w3[0]], axis=2).astype(BF16)
    y = _experts(block_expert, n_used, buf, w13, exp_w2[0].astype(BF16))
    out = _combine(dest, y, x1.reshape(n_tok, d), g2, gates[0:TOP_K].T, seq)
    return out.reshape(bsz, seq, d)
```

```python
import functools
import math

import ml_dtypes
import numpy as np

import jax
import jax.numpy as jnp
from jax import lax
from jax.experimental import pallas as pl
from jax.experimental.pallas import tpu as pltpu

F32, BF16, I32 = jnp.float32, jnp.bfloat16, jnp.int32
SDS = jax.ShapeDtypeStruct
BS = pl.BlockSpec

HEAD_DIM = 64
N_KV_HEADS = 2
WINDOW = 128
Q_BLOCK = 128
GRID_W = 64
ROPE_THETA = 10000.0
EPS = 1e-6
NEG_INF = -1e30
HYENA_BANDS = 16
HYENA_ORDER = 2
DECAY_TARGET = 1e-2
FAST_DECAY_PCT = 0.3
SLOW_DECAY_PCT = 1.5
N_GROUPS = 4
EXPERTS_PER_GROUP = 8
N_EXPERTS = N_GROUPS * EXPERTS_PER_GROUP
TOP_K = 2

LANES = 128
SUBLANES = 8
VMEM_LIMIT = 56 * 1024 * 1024

ROW_TILE = 512
MOE_BLOCK = 256
GATHER_TILE = 256
FILTER_TILE = 512

FFT_MINOR = 128
FFT_MAJOR = 64
N_SLABS = FFT_MAJOR // 2 + 1


def _cparams(n_grid):
    return pltpu.CompilerParams(dimension_semantics=("arbitrary",) * n_grid, vmem_limit_bytes=VMEM_LIMIT)


def _mm(a, b):
    return jnp.dot(a, b, preferred_element_type=F32)


def _mm_nt(a, b):
    return lax.dot_general(a, b, (((1,), (1,)), ((), ())), preferred_element_type=F32)


def _split(x):
    hi = x.astype(BF16)
    lo = (x - hi.astype(F32)).astype(BF16)
    return hi, lo


def _mm_split(w_hi, w_lo, x, passes):
    if passes == 1:
        return _mm(w_hi, x.astype(BF16))
    x_hi, x_lo = _split(x)
    out = _mm(w_hi, x_hi) + _mm(w_hi, x_lo)
    if passes == 3:
        out = out + _mm(w_lo, x_hi)
    return out


def _mm3(a, b):
    a_hi, a_lo = _split(a)
    b_hi, b_lo = _split(b)
    return _mm(a_hi, b_hi) + _mm(a_hi, b_lo) + _mm(a_lo, b_hi)


def _silu(x):
    return x * (1.0 / (1.0 + jnp.exp(-x)))


def _rms(x):
    return x * lax.rsqrt(jnp.mean(x * x, axis=-1, keepdims=True) + EPS)


@functools.lru_cache(maxsize=None)
def _rope_tables(seq_len):
    pos = np.arange(seq_len)
    rows = (pos // GRID_W).astype(np.float64)
    cols = (pos % GRID_W).astype(np.float64)
    pairs = HEAD_DIM // 4
    inv_freq = ROPE_THETA ** (-np.arange(pairs, dtype=np.float64) / pairs)
    ang_r = rows[:, None] * inv_freq[None, :]
    ang_c = cols[:, None] * inv_freq[None, :]
    ang = np.concatenate([ang_r, ang_r, ang_c, ang_c], axis=1)
    sign = np.tile(np.concatenate([-np.ones(pairs), np.ones(pairs)]), 2)
    cos = np.cos(ang)
    sin = np.sin(ang) * sign[None, :]
    reps = LANES // HEAD_DIM
    return np.tile(cos, (1, reps)).astype(np.float32), np.tile(sin, (1, reps)).astype(np.float32)


@functools.lru_cache(maxsize=None)
def _block_diag_mean(width):
    h = np.arange(width) // HEAD_DIM
    return ((h[:, None] == h[None, :]).astype(np.float32) / HEAD_DIM).astype(ml_dtypes.bfloat16)


@functools.lru_cache(maxsize=None)
def _filter_features(seq_len, width):
    pos = np.arange(seq_len, dtype=np.float64)
    t = pos / seq_len
    bands = np.linspace(1e-4, HYENA_BANDS - 1, HYENA_BANDS)
    ang = (2.0 * math.pi / seq_len) * pos[:, None] * bands[None, :]
    feat = np.concatenate([t[:, None], np.cos(ang), -np.sin(ang)], axis=-1)
    out = np.zeros((seq_len, width), np.float32)
    out[:, :feat.shape[1]] = feat
    return out


@functools.lru_cache(maxsize=None)
def _decay_rates(width):
    max_decay = math.log(DECAY_TARGET) / FAST_DECAY_PCT
    min_decay = math.log(DECAY_TARGET) / SLOW_DECAY_PCT
    return np.abs(np.linspace(min_decay, max_decay, width)).astype(np.float32)


@functools.lru_cache(maxsize=None)
def _dft_tables():
    n = FFT_MAJOR * FFT_MINOR
    half = FFT_MAJOR // 2
    n2 = np.arange(half)
    a = np.zeros((FFT_MAJOR, half))
    a[:half] = np.cos(2 * np.pi * np.arange(half)[:, None] * n2[None, :] / FFT_MAJOR)
    a[half] = (-1.0) ** n2
    k2 = np.arange(1, half)
    a[half + 1:] = -np.sin(2 * np.pi * k2[:, None] * n2[None, :] / FFT_MAJOR)
    d = np.zeros((half, FFT_MAJOR))
    d[:, 0] = 1.0 / n
    d[:, 1:half] = 2 * np.cos(2 * np.pi * n2[:, None] * k2[None, :] / FFT_MAJOR) / n
    d[:, half] = (-1.0) ** n2 / n
    d[:, half + 1:] = -2 * np.sin(2 * np.pi * n2[:, None] * k2[None, :] / FFT_MAJOR) / n
    eye = np.eye(SUBLANES)
    ak = np.kron(a, eye)
    dk = np.kron(d, eye)
    s = np.arange(N_SLABS)[:, None, None]
    k1 = np.arange(FFT_MINOR)[None, :, None]
    n1 = np.arange(FFT_MINOR)[None, None, :]
    g = np.exp(-2j * np.pi * n1 * (FFT_MAJOR * k1 + s) / n)
    mf = np.block([[g.real, -g.imag], [g.imag, g.real]])
    gi = np.conj(g).transpose(0, 2, 1)
    mi = np.block([[gi.real, -gi.imag], [gi.imag, gi.real]])
    return {name: m.astype(np.float32) for name, m in (("ak", ak), ("dk", dk), ("mf", mf), ("mi", mi))}


def _dft_hi_lo(name):
    return _split(jnp.asarray(_dft_tables()[name]))


def _ada_body(c_ref, w_ref, b_ref, o_ref):
    o_ref[...] = _mm3(_silu(c_ref[...]), w_ref[...]) + b_ref[...]


def _ada(cc, ada_w, ada_b):
    rows, d = cc.shape
    n = ada_w.shape[1]
    tn = 1536
    return pl.pallas_call(
        _ada_body, out_shape=SDS((rows, n), F32), grid=(n // tn,),
        in_specs=[BS((rows, d), lambda j: (0, 0)), BS((d, tn), lambda j: (0, j)), BS((1, tn), lambda j: (0, j))],
        out_specs=BS((rows, tn), lambda j: (0, j)), compiler_params=_cparams(1), name="ada_mod",
    )(cc, ada_w, ada_b.reshape(1, n))


def _head_rms(t, bd):
    hi, lo = _split(t * t)
    return t * lax.rsqrt(_mm(hi, bd) + _mm(lo, bd) + EPS)


def _rope(t, cos, sin):
    n = t.shape[1]
    quarter = HEAD_DIM // 4
    lane = lax.broadcasted_iota(I32, t.shape, 1)
    up = pltpu.roll(t, n - quarter, axis=1)
    dn = pltpu.roll(t, quarter, axis=1)
    partner = jnp.where((lane & (2 * quarter - 1)) < quarter, up, dn)
    return t * cos + partner * sin


def _inproj_body(x_ref, sh_ref, sc_ref, nw_ref, w_ref, qnw_ref, knw_ref, cos_ref, sin_ref, bdq_ref, bdk_ref,
                 q_ref, k_ref, v_ref, u_ref, *, att_w, kv_w):
    h = _rms(x_ref[0]) * nw_ref[...]
    h = h * (1.0 + sc_ref[0]) + sh_ref[0]
    proj = _mm(h.astype(BF16), w_ref[...])
    cos, sin = cos_ref[...], sin_ref[...]
    reps = att_w // LANES
    cos_q = jnp.concatenate([cos] * reps, axis=1)
    sin_q = jnp.concatenate([sin] * reps, axis=1)
    q = _head_rms(proj[:, :att_w], bdq_ref[...]) * qnw_ref[...]
    q_ref[0] = (_rope(q, cos_q, sin_q) * (HEAD_DIM ** -0.5)).astype(BF16)
    k = _head_rms(proj[:, att_w:att_w + kv_w], bdk_ref[...]) * knw_ref[...]
    k_ref[0] = _rope(k, cos, sin).astype(BF16)
    v_ref[0] = proj[:, att_w + kv_w:att_w + 2 * kv_w].astype(BF16)
    hy_off = att_w + 2 * kv_w
    for j in range(u_ref.shape[1]):
        u_ref[0, j] = proj[:, hy_off + LANES * j:hy_off + LANES * (j + 1)]


def _inproj(x, sh, sc, norm_w, w_bf, qnw, knw, att_w, kv_w):
    bsz, seq, d = x.shape
    tm = ROW_TILE
    hy_tiles = (w_bf.shape[1] - att_w - 2 * kv_w) // LANES
    cos, sin = _rope_tables(seq)
    row = lambda b, i: (b, i, 0)
    fixed2 = lambda b, i: (0, 0)
    per_b = lambda b, i: (b, 0, 0)
    return pl.pallas_call(
        functools.partial(_inproj_body, att_w=att_w, kv_w=kv_w),
        out_shape=(SDS((bsz, seq, att_w), BF16), SDS((bsz, seq, kv_w), BF16), SDS((bsz, seq, kv_w), BF16),
                   SDS((bsz, hy_tiles, seq, LANES), F32)),
        grid=(bsz, seq // tm),
        in_specs=[BS((1, tm, d), row), BS((1, 1, d), per_b), BS((1, 1, d), per_b), BS((1, d), fixed2),
                  BS(w_bf.shape, fixed2), BS((1, att_w), fixed2), BS((1, kv_w), fixed2),
                  BS((tm, LANES), lambda b, i: (i, 0)), BS((tm, LANES), lambda b, i: (i, 0)),
                  BS((att_w, att_w), fixed2), BS((kv_w, kv_w), fixed2)],
        out_specs=(BS((1, tm, att_w), row), BS((1, tm, kv_w), row), BS((1, tm, kv_w), row),
                   BS((1, hy_tiles, tm, LANES), lambda b, i: (b, 0, i, 0))),
        compiler_params=_cparams(2), name="in_proj",
    )(x, sh, sc, norm_w.reshape(1, d), w_bf, qnw, knw, jnp.asarray(cos), jnp.asarray(sin),
      jnp.asarray(_block_diag_mean(att_w)), jnp.asarray(_block_diag_mean(kv_w)))


def _ctxproj_body(x_ref, sh_ref, sc_ref, nw_ref, w_ref, knw_ref, bdk_ref, k_ref, v_ref, *, kv_w):
    h = _rms(x_ref[0]) * nw_ref[...]
    h = h * (1.0 + sc_ref[...]) + sh_ref[...]
    proj = _mm(h.astype(BF16), w_ref[...])
    k_ref[0] = (_head_rms(proj[:, :kv_w], bdk_ref[...]) * knw_ref[...]).astype(BF16)
    v_ref[0] = proj[:, kv_w:].astype(BF16)


def _ctxproj(ctx, sh, sc, norm_w, w_kv_bf, knw, kv_w):
    bsz, n_ctx, d = ctx.shape
    fixed2 = lambda b: (0, 0)
    row = lambda b: (b, 0, 0)
    return pl.pallas_call(
        functools.partial(_ctxproj_body, kv_w=kv_w),
        out_shape=(SDS((bsz, n_ctx, kv_w), BF16), SDS((bsz, n_ctx, kv_w), BF16)), grid=(bsz,),
        in_specs=[BS((1, n_ctx, d), row), BS((1, d), fixed2), BS((1, d), fixed2), BS((1, d), fixed2),
                  BS(w_kv_bf.shape, fixed2), BS((1, kv_w), fixed2), BS((kv_w, kv_w), fixed2)],
        out_specs=(BS((1, n_ctx, kv_w), row), BS((1, n_ctx, kv_w), row)),
        compiler_params=_cparams(1), name="ctx_proj",
    )(ctx, sh, sc, norm_w.reshape(1, d), w_kv_bf, knw, jnp.asarray(_block_diag_mean(kv_w)))


def _attn_body(sink_ref, q_ref, kp_ref, kc_ref, kn_ref, vp_ref, vc_ref, vn_ref, kx_ref, vx_ref, gw_ref, o_ref):
    i = pl.program_id(1)
    last = pl.num_programs(1) - 1
    n_ctx = kx_ref.shape[1]
    n_keys = n_ctx + 3 * Q_BLOCK
    n_heads = q_ref.shape[2] // HEAD_DIM
    group = n_heads // N_KV_HEADS
    rows = group * Q_BLOCK
    r = lax.broadcasted_iota(I32, (rows, n_keys), 0) & (Q_BLOCK - 1)
    j = lax.broadcasted_iota(I32, (rows, n_keys), 1) - n_ctx
    in_band = (j >= r) & (j <= r + 2 * WINDOW)
    in_seq = jnp.logical_or(j >= Q_BLOCK, i > 0) & jnp.logical_or(j < 2 * Q_BLOCK, i < last)
    valid = (j < 0) | (in_band & in_seq)
    head_of_row = lax.broadcasted_iota(I32, (rows, 1), 0) // Q_BLOCK
    outs = []
    for g in range(N_KV_HEADS):
        sl = slice(g * HEAD_DIM, (g + 1) * HEAD_DIM)
        kb = jnp.concatenate([kx_ref[0][:, sl], kp_ref[0][:, sl], kc_ref[0][:, sl], kn_ref[0][:, sl]], axis=0)
        vb = jnp.concatenate([vx_ref[0][:, sl], vp_ref[0][:, sl], vc_ref[0][:, sl], vn_ref[0][:, sl]], axis=0)
        heads = range(g * group, (g + 1) * group)
        q4 = jnp.concatenate([q_ref[0][:, h * HEAD_DIM:(h + 1) * HEAD_DIM] for h in heads], axis=0)
        sink = jnp.zeros((rows, 1), F32)
        for hh, h in enumerate(heads):
            sink = jnp.where(head_of_row == hh, sink_ref[h], sink)
        s = jnp.where(valid, _mm_nt(q4, kb), NEG_INF)
        m = jnp.maximum(jnp.max(s, axis=-1, keepdims=True), sink)
        p = jnp.exp(s - m)
        denom = jnp.sum(p, axis=-1, keepdims=True) + jnp.exp(sink - m)
        o4 = _mm(p.astype(BF16), vb) / denom
        for hh in range(group):
            outs.append(o4[hh * Q_BLOCK:(hh + 1) * Q_BLOCK])
    att = jnp.concatenate(outs, axis=1)
    o_ref[0] = (_rms(att) * gw_ref[...]).astype(BF16)


def _attention(sink, q, k, v, kx, vx, gw):
    bsz, seq, att_w = q.shape
    kv_w = k.shape[2]
    n_ctx = kx.shape[1]
    nb = seq // Q_BLOCK
    cur = lambda b, i: (b, i, 0)
    prev = lambda b, i: (b, jnp.maximum(i - 1, 0), 0)
    nxt = lambda b, i: (b, jnp.minimum(i + 1, nb - 1), 0)
    per_b = lambda b, i: (b, 0, 0)
    kvb = (1, Q_BLOCK, kv_w)
    return pl.pallas_call(
        _attn_body, out_shape=SDS((bsz, seq, att_w), BF16), grid=(bsz, nb),
        in_specs=[BS(memory_space=pltpu.SMEM), BS((1, Q_BLOCK, att_w), cur),
                  BS(kvb, prev), BS(kvb, cur), BS(kvb, nxt), BS(kvb, prev), BS(kvb, cur), BS(kvb, nxt),
                  BS((1, n_ctx, kv_w), per_b), BS((1, n_ctx, kv_w), per_b), BS((1, att_w), lambda b, i: (0, 0))],
        out_specs=BS((1, Q_BLOCK, att_w), cur), compiler_params=_cparams(2), name="window_attn",
    )(sink, q, k, k, k, v, v, v, kx, vx, gw)


def _filter_body(f_ref, w1h, w1l, b1, w2h, w2l, b2, w3h, w3l, sf_ref, dl_ref, o_ref, *, hy_w):
    def mm_w(a, wh, wl):
        a_hi, a_lo = _split(a)
        return _mm(a_hi, wh[...]) + _mm(a_hi, wl[...]) + _mm(a_lo, wh[...])

    f = f_ref[...]
    h = jnp.sin(sf_ref[0:1, :] * (mm_w(f, w1h, w1l) + b1[...]))
    h = jnp.sin(sf_ref[1:2, :] * (mm_w(h, w2h, w2l) + b2[...]))
    h = mm_w(h, w3h, w3l)
    h = h * jnp.exp(-f[:, 0:1] * dl_ref[...])
    tl = f.shape[0]
    row = lax.broadcasted_iota(I32, h.shape, 0) + pl.program_id(0) * tl
    col = lax.broadcasted_iota(I32, h.shape, 1)
    is_bwd = ((col // hy_w) & 1) == 1
    h = jnp.where((row == 0) & is_bwd, 0.0, h)
    for j in range(o_ref.shape[0]):
        o_ref[j] = h[:, LANES * j:LANES * (j + 1)]


def _filter_mlp(seq, w1, b1, w2, b2, w3, sin_freq, hy_w):
    hid = w1.shape[1]
    fw = 64
    feat = jnp.asarray(_filter_features(seq, fw))
    w1p = jnp.zeros((fw, hid), F32).at[:w1.shape[0]].set(w1)
    n_out = w3.shape[1]
    delta = jnp.asarray(np.tile(_decay_rates(hy_w), n_out // hy_w)).reshape(1, n_out)
    tl = FILTER_TILE
    fixed = lambda i: (0, 0)
    ops = []
    for w in (w1p, w2, w3):
        ops.extend(_split(w))
    return pl.pallas_call(
        functools.partial(_filter_body, hy_w=hy_w),
        out_shape=SDS((n_out // LANES, seq, LANES), F32), grid=(seq // tl,),
        in_specs=[BS((tl, fw), lambda i: (i, 0)), BS((fw, hid), fixed), BS((fw, hid), fixed), BS((1, hid), fixed),
                  BS((hid, hid), fixed), BS((hid, hid), fixed), BS((1, hid), fixed),
                  BS((hid, n_out), fixed), BS((hid, n_out), fixed), BS((2, hid), fixed), BS((1, n_out), fixed)],
        out_specs=BS((n_out // LANES, tl, LANES), lambda i: (0, i, 0)),
        compiler_params=_cparams(1), name="hyena_filter_mlp",
    )(feat, ops[0], ops[1], b1.reshape(1, hid), ops[2], ops[3], b2.reshape(1, hid), ops[4], ops[5], sin_freq, delta)


C_TILES = 2
C_GROUP = C_TILES * LANES


def _load_cat(ref, row0, n_rows):
    return jnp.concatenate([ref[c, pl.ds(row0, n_rows), :] for c in range(C_TILES)], axis=1)


def _store_cat(ref, row0, n_rows, val):
    for c in range(C_TILES):
        ref[c, pl.ds(row0, n_rows), :] = val[:, c * LANES:(c + 1) * LANES]


def _fwd_major(src, p_ref, ak, passes):
    half = FFT_MAJOR // 2

    def body(g, carry):
        r0 = pl.multiple_of(g * SUBLANES, SUBLANES)
        st = jnp.concatenate([_load_cat(src, n2 * FFT_MINOR + r0, SUBLANES) for n2 in range(half)], axis=0)
        out = _mm_split(ak[0][...], ak[1][...] if passes == 3 else None, st, passes)
        for p in range(FFT_MAJOR):
            _store_cat(p_ref, p * FFT_MINOR + r0, SUBLANES, out[p * SUBLANES:(p + 1) * SUBLANES])
        return carry

    lax.fori_loop(0, FFT_MINOR // SUBLANES, body, 0, unroll=2)


def _inv_major(p_ref, dst, dk, passes):
    half = FFT_MAJOR // 2

    def body(g, carry):
        r0 = pl.multiple_of(g * SUBLANES, SUBLANES)
        st = jnp.concatenate([_load_cat(p_ref, p * FFT_MINOR + r0, SUBLANES) for p in range(FFT_MAJOR)], axis=0)
        out = _mm_split(dk[0][...], dk[1][...] if passes == 3 else None, st, passes)
        for n2 in range(half):
            _store_cat(dst, n2 * FFT_MINOR + r0, SUBLANES, out[n2 * SUBLANES:(n2 + 1) * SUBLANES])
        return carry

    lax.fori_loop(0, FFT_MINOR // SUBLANES, body, 0, unroll=2)


def _slab_spectrum(p_ref, mf, s, passes, real_slot=None):
    m = FFT_MINOR
    if real_slot is None:
        st = jnp.concatenate([_load_cat(p_ref, s * m, m), _load_cat(p_ref, (FFT_MAJOR // 2 + s) * m, m)], axis=0)
        x = _mm_split(mf[0][s], mf[1][s] if passes == 3 else None, st, passes)
    else:
        st = _load_cat(p_ref, real_slot * m, m)
        x = _mm_split(mf[0][s, :, 0:m], mf[1][s, :, 0:m] if passes == 3 else None, st, passes)
    return x[:m], x[m:]


def _spectrum_body(f_ref, b_ref, akh, akl, mfh, mfl, o_ref, p_ref):
    half = FFT_MAJOR // 2
    ak, mf = (akh, akl), (mfh, mfl)
    for is_bwd, src in ((False, f_ref), (True, b_ref)):
        _fwd_major(src, p_ref, ak, 3)

        def put(s, re, im, is_bwd=is_bwd):
            if is_bwd:
                o_ref[0, 0, s, 0] = o_ref[0, 0, s, 0] + re
                o_ref[0, 0, s, 1] = o_ref[0, 0, s, 1] - im
            else:
                o_ref[0, 0, s, 0] = re
                o_ref[0, 0, s, 1] = im

        put(0, *_slab_spectrum(p_ref, mf, 0, 3, real_slot=0))
        put(half, *_slab_spectrum(p_ref, mf, half, 3, real_slot=half))

        def body(s, carry):
            put(s, *_slab_spectrum(p_ref, mf, s, 3))
            return carry

        lax.fori_loop(1, half, body, 0, unroll=2)


def _spectrum(hf, seq, hy_w):
    ak, mf = _dft_hi_lo("ak"), _dft_hi_lo("mf")
    n_grp = hy_w // C_GROUP
    fixed2 = lambda o, c: (0, 0)
    fixed3 = lambda o, c: (0, 0, 0)
    one = pl.Buffered(1)
    return pl.pallas_call(
        _spectrum_body,
        out_shape=SDS((HYENA_ORDER, n_grp, N_SLABS, 2, FFT_MINOR, C_GROUP), F32), grid=(HYENA_ORDER, n_grp),
        in_specs=[BS((C_TILES, seq, LANES), lambda o, c: (o * 2 * n_grp + c, 0, 0), pipeline_mode=one),
                  BS((C_TILES, seq, LANES), lambda o, c: (o * 2 * n_grp + n_grp + c, 0, 0), pipeline_mode=one),
                  BS(ak[0].shape, fixed2, pipeline_mode=one), BS(ak[0].shape, fixed2, pipeline_mode=one),
                  BS(mf[0].shape, fixed3, pipeline_mode=one), BS(mf[0].shape, fixed3, pipeline_mode=one)],
        out_specs=BS((1, 1, N_SLABS, 2, FFT_MINOR, C_GROUP), lambda o, c: (o, c, 0, 0, 0, 0)),
        scratch_shapes=[pltpu.VMEM((C_TILES, FFT_MAJOR * FFT_MINOR, LANES), F32)],
        compiler_params=_cparams(2), name="hyena_spectrum",
    )(hf, hf, ak[0], ak[1], mf[0], mf[1])


CONV_CHUNK = 128
CONV_PASSES = 1


def _sconv_chunk(ref, c, r0, n_rows, seq, prm_ref, base):
    cl = slice(c * LANES, (c + 1) * LANES)
    cur = ref[c, pl.ds(r0, n_rows), :]
    row = lax.broadcasted_iota(I32, cur.shape, 0)
    before = ref[c, pl.ds(jnp.maximum(r0 - 1, 0), 1), :] * (r0 > 0).astype(F32)
    after = ref[c, pl.ds(jnp.minimum(r0 + n_rows, seq - 1), 1), :] * (r0 + n_rows < seq).astype(F32)
    prev = jnp.where(row == 0, before, pltpu.roll(cur, 1, axis=0))
    nxt = jnp.where(row == n_rows - 1, after, pltpu.roll(cur, n_rows - 1, axis=0))
    w = lambda k: prm_ref[0, 0, base + k:base + k + 1, cl]
    return w(3) + w(0) * prev + w(1) * cur + w(2) * nxt


def _conv_body(z_ref, x_ref, prm_ref, akh, dkh, mfh, mih, h_ref, o_ref, zs_ref, p_ref, *, conv_z):
    seq = z_ref.shape[2]
    half = FFT_MAJOR // 2
    m = FFT_MINOR
    zv, xv = z_ref.at[0], x_ref.at[0]
    ak, dk, mf, mi = (akh, None), (dkh, None), (mfh, None), (mih, None)
    n_chunks = seq // CONV_CHUNK

    if conv_z:
        def prep(i, carry):
            r0 = pl.multiple_of(i * CONV_CHUNK, CONV_CHUNK)
            for c in range(C_TILES):
                zs_ref[c, pl.ds(r0, CONV_CHUNK), :] = _sconv_chunk(zv, c, r0, CONV_CHUNK, seq, prm_ref, 0)
            return carry

        lax.fori_loop(0, n_chunks, prep, 0)
        src = zs_ref
    else:
        src = zv
    _fwd_major(src, p_ref, ak, CONV_PASSES)

    def slab(s, real_slot=None):
        xr, xi = _slab_spectrum(p_ref, mf, s, CONV_PASSES, real_slot)
        hr, hi = h_ref[0, 0, s, 0], h_ref[0, 0, s, 1]
        y = jnp.concatenate([xr * hr - xi * hi, xr * hi + xi * hr], axis=0)
        if real_slot is None:
            out = _mm_split(mi[0][s], None, y, CONV_PASSES)
            _store_cat(p_ref, s * m, m, out[:m])
            _store_cat(p_ref, (half + s) * m, m, out[m:])
        else:
            out = _mm_split(mi[0][s, 0:m, :], None, y, CONV_PASSES)
            _store_cat(p_ref, real_slot * m, m, out)

    slab(0, real_slot=0)
    slab(half, real_slot=half)

    def slab_loop(s, carry):
        slab(s)
        return carry

    lax.fori_loop(1, half, slab_loop, 0, unroll=2)
    _inv_major(p_ref, zs_ref, dk, CONV_PASSES)

    def fin(i, carry):
        r0 = pl.multiple_of(i * CONV_CHUNK, CONV_CHUNK)
        for c in range(C_TILES):
            cl = slice(c * LANES, (c + 1) * LANES)
            gate = _sconv_chunk(xv, c, r0, CONV_CHUNK, seq, prm_ref, 4)
            if conv_z:
                zval = _sconv_chunk(zv, c, r0, CONV_CHUNK, seq, prm_ref, 0)
            else:
                zval = zv[c, pl.ds(r0, CONV_CHUNK), :]
            skip = prm_ref[0, 0, 8:9, cl]
            o_ref[0, c, pl.ds(r0, CONV_CHUNK), :] = gate * (zs_ref[c, pl.ds(r0, CONV_CHUNK), :] + skip * zval)
        return carry

    lax.fori_loop(0, n_chunks, fin, 0, unroll=2)


def _hyena_order(z, z_grp0, x, x_grp0, prm, order, spec, conv_z, n_grp):
    bsz, _, seq, _ = z.shape
    ak, dk, mf, mi = (_dft_hi_lo(n)[0] for n in ("ak", "dk", "mf", "mi"))
    one = pl.Buffered(1)
    fixed2 = lambda c, b: (0, 0)
    fixed3 = lambda c, b: (0, 0, 0)
    blk = (1, C_TILES, seq, LANES)
    return pl.pallas_call(
        functools.partial(_conv_body, conv_z=conv_z),
        out_shape=SDS((bsz, n_grp * C_TILES, seq, LANES), F32), grid=(n_grp, bsz),
        in_specs=[BS(blk, lambda c, b: (b, z_grp0 + c, 0, 0), pipeline_mode=one),
                  BS(blk, lambda c, b: (b, x_grp0 + c, 0, 0), pipeline_mode=one),
                  BS((1, 1, 16, C_GROUP), lambda c, b: (order, c, 0, 0)),
                  BS(ak.shape, fixed2, pipeline_mode=one), BS(dk.shape, fixed2, pipeline_mode=one),
                  BS(mf.shape, fixed3, pipeline_mode=one), BS(mi.shape, fixed3, pipeline_mode=one),
                  BS((1, 1, N_SLABS, 2, FFT_MINOR, C_GROUP), lambda c, b: (order, c, 0, 0, 0, 0), pipeline_mode=one)],
        out_specs=BS(blk, lambda c, b: (b, c, 0, 0)),
        scratch_shapes=[pltpu.VMEM((C_TILES, seq, LANES), F32),
                        pltpu.VMEM((C_TILES, FFT_MAJOR * FFT_MINOR, LANES), F32)],
        compiler_params=_cparams(2), name=f"hyena_conv{order}",
    )(z, x, prm, ak, dk, mf, mi, spec)


ROUTER_ROWS = 48
EXPERT_ROW0 = 8


def _outproj_body(att_ref, hy_ref, gwh_ref, wo_ref, x_ref, g1_ref, sh2_ref, sc2_ref, n2w_ref, wrh_ref, wrl_ref,
                  br_ref, tri_ref, x1_ref, h2_ref, ids_ref, gates_ref, cnt_ref, carry_ref):
    @pl.when((pl.program_id(0) == 0) & (pl.program_id(1) == 0))
    def _():
        carry_ref[...] = jnp.zeros_like(carry_ref)

    hy = jnp.concatenate([hy_ref[0, j] for j in range(hy_ref.shape[1])], axis=1)
    hyn = _rms(hy) * gwh_ref[...]
    mix = _mm(jnp.concatenate([att_ref[0], hyn.astype(BF16)], axis=1), wo_ref[...])
    x1 = x_ref[0] + g1_ref[0] * mix
    x1_ref[0] = x1
    h2 = _rms(x1) * n2w_ref[...]
    h2 = h2 * (1.0 + sc2_ref[0]) + sh2_ref[0]
    h2_ref[...] = h2

    h_hi, h_lo = _split(h2)
    lg = _mm_nt(wrh_ref[...], h_hi) + _mm_nt(wrh_ref[...], h_lo) + _mm_nt(wrl_ref[...], h_hi) + br_ref[...]
    tm = lg.shape[1]
    gl = lg[0:N_GROUPS]
    el = lg[EXPERT_ROW0:EXPERT_ROW0 + N_EXPERTS]
    gmax = jnp.max(gl, axis=0, keepdims=True)
    rg = lax.broadcasted_iota(I32, gl.shape, 0).astype(F32)
    g_idx = jnp.min(jnp.where(gl == gmax, rg, float(N_GROUPS)), axis=0, keepdims=True)
    g_val = 1.0 / jnp.sum(jnp.exp(gl - gmax), axis=0, keepdims=True)
    re_i = lax.broadcasted_iota(I32, el.shape, 0)
    re = re_i.astype(F32)
    in_group = (re_i // EXPERTS_PER_GROUP).astype(F32) == g_idx
    elm = jnp.where(in_group, el, NEG_INF)
    m1 = jnp.max(elm, axis=0, keepdims=True)
    i1 = jnp.min(jnp.where(elm == m1, re, float(N_EXPERTS)), axis=0, keepdims=True)
    elm2 = jnp.where(re == i1, NEG_INF, elm)
    m2 = jnp.max(elm2, axis=0, keepdims=True)
    i2 = jnp.min(jnp.where(elm2 == m2, re, float(N_EXPERTS)), axis=0, keepdims=True)
    e2 = jnp.exp(m2 - m1)
    inv = g_val / (1.0 + e2)

    oh1 = (re == i1).astype(F32)
    oh2 = (re == i2).astype(F32)
    oh = oh1 + oh2
    base = _mm(oh.astype(BF16), tri_ref[...]) + carry_ref[:, 0:1]
    rank1 = jnp.sum(oh1 * base, axis=0, keepdims=True)
    rank2 = jnp.sum(oh2 * base, axis=0, keepdims=True)
    carry_ref[...] = carry_ref[...] + jnp.sum(oh, axis=1, keepdims=True)
    cnt_ref[...] = carry_ref[...]

    zi = jnp.zeros((1, tm), I32)
    for k, val in enumerate((i1.astype(I32), i2.astype(I32), rank1.astype(I32), rank2.astype(I32), zi, zi, zi, zi)):
        ids_ref[k:k + 1, :] = val
    zf = jnp.zeros((1, tm), F32)
    for k, val in enumerate((inv, inv * e2, zf, zf, zf, zf, zf, zf)):
        gates_ref[k:k + 1, :] = val


def _outproj_route(att_n, hy, gw_hy, wo_bf, x, g1, sh2, sc2, norm2_w, wr_hi, wr_lo, br):
    bsz, seq, d = x.shape
    tm = ROW_TILE
    n_tok = bsz * seq
    att_w = att_n.shape[2]
    hy_tiles = hy.shape[1]
    tri = jnp.asarray(np.triu(np.ones((tm, tm), np.float32), k=1).astype(ml_dtypes.bfloat16))
    row = lambda b, i: (b, i, 0)
    per_b = lambda b, i: (b, 0, 0)
    fixed2 = lambda b, i: (0, 0)
    tok = lambda b, i: (0, b * (seq // tm) + i)
    return pl.pallas_call(
        _outproj_body,
        out_shape=(SDS((bsz, seq, d), F32), SDS((n_tok, d), F32), SDS((SUBLANES, n_tok), I32),
                   SDS((SUBLANES, n_tok), F32), SDS((N_EXPERTS, LANES), F32)),
        grid=(bsz, seq // tm),
        in_specs=[BS((1, tm, att_w), row), BS((1, hy_tiles, tm, LANES), lambda b, i: (b, 0, i, 0)),
                  BS((1, hy_tiles * LANES), fixed2), BS(wo_bf.shape, fixed2), BS((1, tm, d), row),
                  BS((1, 1, d), per_b), BS((1, 1, d), per_b), BS((1, 1, d), per_b), BS((1, d), fixed2),
                  BS((ROUTER_ROWS, d), fixed2), BS((ROUTER_ROWS, d), fixed2), BS((ROUTER_ROWS, 1), fixed2),
                  BS((tm, tm), fixed2)],
        out_specs=(BS((1, tm, d), row), BS((tm, d), lambda b, i: (b * (seq // tm) + i, 0)),
                   BS((SUBLANES, tm), tok), BS((SUBLANES, tm), tok), BS((N_EXPERTS, LANES), fixed2)),
        scratch_shapes=[pltpu.VMEM((N_EXPERTS, LANES), F32)],
        compiler_params=_cparams(2), name="out_proj_router",
    )(att_n, hy, gw_hy, wo_bf, x, g1, sh2, sc2, norm2_w.reshape(1, d), wr_hi, wr_lo, br, tri)


def _row_copy(src, src_row, dst, dst_row, sem):
    return pltpu.make_async_copy(src.at[pl.ds(src_row, 1), :], dst.at[pl.ds(dst_row, 1), :], sem)


def _dispatch_body(d_ref, h_ref, zero_ref, buf_ref, sem):
    del zero_ref
    n = h_ref.shape[0]

    def issue(t, carry):
        for j in range(TOP_K):
            _row_copy(h_ref, t, buf_ref, d_ref[j, t], sem).start()
        return carry

    lax.fori_loop(0, n, issue, 0)

    def drain(t, carry):
        for j in range(TOP_K):
            _row_copy(h_ref, t, buf_ref, d_ref[j, t], sem).wait()
        return carry

    lax.fori_loop(0, n, drain, 0)


def _dispatch(dest, h2, n_rows):
    n_tok, d = h2.shape
    tm = GATHER_TILE
    return pl.pallas_call(
        _dispatch_body, out_shape=SDS((n_rows, d), F32), grid=(n_tok // tm,),
        in_specs=[BS((TOP_K, tm), lambda i: (0, i), memory_space=pltpu.SMEM), BS((tm, d), lambda i: (i, 0)),
                  BS(memory_space=pl.ANY)],
        out_specs=BS(memory_space=pl.ANY), scratch_shapes=[pltpu.SemaphoreType.DMA(())],
        input_output_aliases={2: 0}, compiler_params=_cparams(1), name="moe_dispatch",
    )(dest, h2, jnp.zeros((n_rows, d), F32))


def _expert_body(be_ref, nu_ref, x_ref, w13_ref, w2_ref, y_ref):
    del be_ref
    i = pl.program_id(0)
    hidden = w2_ref.shape[1]

    @pl.when(i < nu_ref[0])
    def _():
        h = _mm(x_ref[...].astype(BF16), w13_ref[0])
        act = _silu(h[:, :hidden]) * h[:, hidden:]
        y_ref[...] = _mm(act.astype(BF16), w2_ref[0])

    @pl.when(i >= nu_ref[0])
    def _():
        y_ref[...] = jnp.zeros_like(y_ref)


def _experts(block_expert, n_used, buf, w13_bf, w2_bf):
    n_rows, d = buf.shape
    nb = n_rows // MOE_BLOCK
    grid_spec = pltpu.PrefetchScalarGridSpec(
        num_scalar_prefetch=2, grid=(nb,),
        in_specs=[BS((MOE_BLOCK, d), lambda i, be, nu: (i, 0)),
                  BS((1,) + w13_bf.shape[1:], lambda i, be, nu: (be[i], 0, 0)),
                  BS((1,) + w2_bf.shape[1:], lambda i, be, nu: (be[i], 0, 0))],
        out_specs=BS((MOE_BLOCK, d), lambda i, be, nu: (i, 0)))
    return pl.pallas_call(_expert_body, out_shape=SDS((n_rows, d), F32), grid_spec=grid_spec,
                          compiler_params=_cparams(1), name="moe_experts")(block_expert, n_used, buf, w13_bf, w2_bf)


def _combine_body(d_ref, y_ref, x1_ref, g2_ref, gt_ref, o_ref, rows_ref, sem):
    n = x1_ref.shape[0]

    def issue(t, carry):
        for j in range(TOP_K):
            _row_copy(y_ref, d_ref[j, t], rows_ref.at[j], t, sem).start()
        return carry

    lax.fori_loop(0, n, issue, 0)

    def drain(t, carry):
        for j in range(TOP_K):
            _row_copy(y_ref, d_ref[j, t], rows_ref.at[j], t, sem).wait()
        return carry

    lax.fori_loop(0, n, drain, 0)
    gt = gt_ref[...]
    moe = gt[:, 0:1] * rows_ref[0] + gt[:, 1:2] * rows_ref[1]
    o_ref[...] = x1_ref[...] + g2_ref[0] * moe


def _combine(dest, y, x1, g2, gates_t, seq):
    n_tok, d = x1.shape
    tm = GATHER_TILE
    return pl.pallas_call(
        _combine_body, out_shape=SDS((n_tok, d), F32), grid=(n_tok // tm,),
        in_specs=[BS((TOP_K, tm), lambda i: (0, i), memory_space=pltpu.SMEM), BS(memory_space=pl.ANY),
                  BS((tm, d), lambda i: (i, 0)), BS((1, 1, d), lambda i: (i // (seq // tm), 0, 0)),
                  BS((tm, TOP_K), lambda i: (i, 0))],
        out_specs=BS((tm, d), lambda i: (i, 0)),
        scratch_shapes=[pltpu.VMEM((TOP_K, tm, d), F32), pltpu.SemaphoreType.DMA(())],
        compiler_params=_cparams(1), name="moe_combine",
    )(dest, y, x1, g2, gates_t)


def kernel(x, c, ctx, c_ctx, ada_w, ada_b, norm1_w, w_in, q_norm_w, k_norm_w, attn_sink, hy_conv_w, hy_conv_b, hy_w1,
           hy_b1, hy_w2, hy_b2, hy_w3, hy_sin_freq, hy_skip, group_norm_w, w_out, norm2_w, router_g_w, router_g_b,
           router_e_w, router_e_b, exp_w1, exp_w3, exp_w2):
    bsz, seq, d = x.shape
    assert ada_w.shape[0] == 1, "single-layer block"
    att_w = d // 2
    hy_w = d - att_w
    kv_w = N_KV_HEADS * HEAD_DIM
    n_heads = att_w // HEAD_DIM
    n_grp = hy_w // C_GROUP
    assert 2 * seq == FFT_MAJOR * FFT_MINOR and seq % ROW_TILE == 0 and hy_w % C_GROUP == 0
    assert w_in.shape[2] == att_w + 2 * kv_w + (HYENA_ORDER + 1) * hy_w

    pad = (-(bsz + 1)) % SUBLANES
    cc = jnp.concatenate([c, c_ctx[None, :], jnp.zeros((pad, d), F32)], axis=0)
    mod = _ada(cc, ada_w[0], ada_b[0])
    lat = [mod[:bsz, k * d:(k + 1) * d].reshape(bsz, 1, d) for k in range(6)]
    sh1, sc1, g1, sh2, sc2, g2 = lat
    csh1 = mod[bsz:bsz + 1, 0:d]
    csc1 = mod[bsz:bsz + 1, d:2 * d]

    w_in_bf = w_in[0].astype(BF16)
    qnw = jnp.tile(q_norm_w[0], n_heads).reshape(1, att_w)
    knw = jnp.tile(k_norm_w[0], N_KV_HEADS).reshape(1, kv_w)
    q, k, v, u = _inproj(x, sh1, sc1, norm1_w[0], w_in_bf, qnw, knw, att_w, kv_w)
    kx, vx = _ctxproj(ctx, csh1, csc1, norm1_w[0], w_in_bf[:, att_w:att_w + 2 * kv_w], knw, kv_w)

    gw = group_norm_w[0]
    att_n = _attention(attn_sink[0], q, k, v, kx, vx, gw[:att_w].reshape(1, att_w))

    hf = _filter_mlp(seq, hy_w1[0], hy_b1[0], hy_w2[0], hy_b2[0], hy_w3[0], hy_sin_freq[0], hy_w)
    spec = _spectrum(hf, seq, hy_w)
    cw = hy_conv_w[0].reshape(3, HYENA_ORDER + 1, n_grp, C_GROUP)
    cb = hy_conv_b[0].reshape(1, HYENA_ORDER + 1, n_grp, C_GROUP)
    taps = jnp.concatenate([cw, cb], axis=0)
    prm = jnp.zeros((HYENA_ORDER, n_grp, 16, C_GROUP), F32)
    for o in range(HYENA_ORDER):
        prm = prm.at[o, :, 4:8].set(jnp.transpose(taps[:, o + 1], (1, 0, 2)))
        prm = prm.at[o, :, 8].set(hy_skip[0, o].reshape(n_grp, C_GROUP))
    prm = prm.at[0, :, 0:4].set(jnp.transpose(taps[:, 0], (1, 0, 2)))
    z = _hyena_order(u, 0, u, n_grp, prm, 0, spec, True, n_grp)
    hy = _hyena_order(z, 0, u, 2 * n_grp, prm, 1, spec, False, n_grp)

    wr = jnp.zeros((ROUTER_ROWS, d), F32)
    wr = wr.at[0:N_GROUPS].set(router_g_w[0].T).at[EXPERT_ROW0:EXPERT_ROW0 + N_EXPERTS].set(router_e_w[0].T)
    br = jnp.zeros((ROUTER_ROWS, 1), F32)
    br = br.at[0:N_GROUPS, 0].set(router_g_b[0]).at[EXPERT_ROW0:EXPERT_ROW0 + N_EXPERTS, 0].set(router_e_b[0])
    wr_hi, wr_lo = _split(wr)
    x1, h2, ids, gates, counts = _outproj_route(att_n, hy, gw[att_w:].reshape(1, hy_w), w_out[0].astype(BF16), x,
                                                g1, sh2, sc2, norm2_w[0], wr_hi, wr_lo, br)

    n_tok = bsz * seq
    cnt = counts[:, 0].astype(I32)
    padded = (cnt + MOE_BLOCK - 1) // MOE_BLOCK * MOE_BLOCK
    pend = jnp.cumsum(padded)
    pstart = pend - padded
    e_iota = jnp.arange(N_EXPERTS, dtype=I32)
    is_e = ids[0:TOP_K, :, None] == e_iota
    dest = jnp.sum(jnp.where(is_e, pstart, 0), axis=-1) + ids[TOP_K:2 * TOP_K]
    n_blocks = (n_tok * TOP_K) // MOE_BLOCK + N_EXPERTS
    block_row0 = jnp.arange(n_blocks, dtype=I32) * MOE_BLOCK
    block_expert = jnp.minimum(jnp.sum((block_row0[:, None] >= pend[None, :]).astype(I32), axis=-1), N_EXPERTS - 1)
    n_used = (pend[-1:] // MOE_BLOCK).astype(I32)

    buf = _dispatch(dest, h2, n_blocks * MOE_BLOCK)
    w13 = jnp.concatenate([exp_w1[0], exp_w3[0]], axis=2).astype(BF16)
    y = _experts(block_expert, n_used, buf, w13, exp_w2[0].astype(BF16))
    out = _combine(dest, y, x1.reshape(n_tok, d), g2, gates[0:TOP_K].T, seq)
    return out.reshape(bsz, seq, d)
```

```python
import functools
import math

import ml_dtypes
import numpy as np

import jax
import jax.numpy as jnp
from jax import lax
from jax.experimental import pallas as pl
from jax.experimental.pallas import tpu as pltpu

F32, BF16, I32 = jnp.float32, jnp.bfloat16, jnp.int32
SDS = jax.ShapeDtypeStruct
BS = pl.BlockSpec

HEAD_DIM = 64
N_KV_HEADS = 2
WINDOW = 128
Q_BLOCK = 128
GRID_W = 64
ROPE_THETA = 10000.0
EPS = 1e-6
NEG_INF = -1e30
HYENA_BANDS = 16
HYENA_ORDER = 2
DECAY_TARGET = 1e-2
FAST_DECAY_PCT = 0.3
SLOW_DECAY_PCT = 1.5
N_GROUPS = 4
EXPERTS_PER_GROUP = 8
N_EXPERTS = N_GROUPS * EXPERTS_PER_GROUP
TOP_K = 2

LANES = 128
SUBLANES = 8
VMEM_LIMIT = 56 * 1024 * 1024

ROW_TILE = 512
MOE_BLOCK = 512
GATHER_TILE = 1024
FILTER_TILE = 512

FFT_MINOR = 128
FFT_MAJOR = 64
N_SLABS = FFT_MAJOR // 2 + 1


def _cparams(n_grid):
    return pltpu.CompilerParams(dimension_semantics=("arbitrary",) * n_grid, vmem_limit_bytes=VMEM_LIMIT)


def _mm(a, b):
    return jnp.dot(a, b, preferred_element_type=F32)


def _mm_nt(a, b):
    return lax.dot_general(a, b, (((1,), (1,)), ((), ())), preferred_element_type=F32)


def _split(x):
    hi = x.astype(BF16)
    lo = (x - hi.astype(F32)).astype(BF16)
    return hi, lo


def _mm_split(w_hi, w_lo, x, passes):
    if passes == 1:
        return _mm(w_hi, x.astype(BF16))
    x_hi, x_lo = _split(x)
    out = _mm(w_hi, x_hi) + _mm(w_hi, x_lo)
    if passes == 3:
        out = out + _mm(w_lo, x_hi)
    return out


def _mm3(a, b):
    a_hi, a_lo = _split(a)
    b_hi, b_lo = _split(b)
    return _mm(a_hi, b_hi) + _mm(a_hi, b_lo) + _mm(a_lo, b_hi)


def _silu(x):
    return x * (1.0 / (1.0 + jnp.exp(-x)))


def _rms(x):
    return x * lax.rsqrt(jnp.mean(x * x, axis=-1, keepdims=True) + EPS)


@functools.lru_cache(maxsize=None)
def _rope_tables(seq_len):
    pos = np.arange(seq_len)
    rows = (pos // GRID_W).astype(np.float64)
    cols = (pos % GRID_W).astype(np.float64)
    pairs = HEAD_DIM // 4
    inv_freq = ROPE_THETA ** (-np.arange(pairs, dtype=np.float64) / pairs)
    ang_r = rows[:, None] * inv_freq[None, :]
    ang_c = cols[:, None] * inv_freq[None, :]
    ang = np.concatenate([ang_r, ang_r, ang_c, ang_c], axis=1)
    sign = np.tile(np.concatenate([-np.ones(pairs), np.ones(pairs)]), 2)
    cos = np.cos(ang)
    sin = np.sin(ang) * sign[None, :]
    reps = LANES // HEAD_DIM
    return np.tile(cos, (1, reps)).astype(np.float32), np.tile(sin, (1, reps)).astype(np.float32)


@functools.lru_cache(maxsize=None)
def _block_diag_mean(width):
    h = np.arange(width) // HEAD_DIM
    return ((h[:, None] == h[None, :]).astype(np.float32) / HEAD_DIM).astype(ml_dtypes.bfloat16)


@functools.lru_cache(maxsize=None)
def _filter_features(seq_len, width):
    pos = np.arange(seq_len, dtype=np.float64)
    t = pos / seq_len
    bands = np.linspace(1e-4, HYENA_BANDS - 1, HYENA_BANDS)
    ang = (2.0 * math.pi / seq_len) * pos[:, None] * bands[None, :]
    feat = np.concatenate([t[:, None], np.cos(ang), -np.sin(ang)], axis=-1)
    out = np.zeros((seq_len, width), np.float32)
    out[:, :feat.shape[1]] = feat
    return out


@functools.lru_cache(maxsize=None)
def _decay_rates(width):
    max_decay = math.log(DECAY_TARGET) / FAST_DECAY_PCT
    min_decay = math.log(DECAY_TARGET) / SLOW_DECAY_PCT
    return np.abs(np.linspace(min_decay, max_decay, width)).astype(np.float32)


@functools.lru_cache(maxsize=None)
def _dft_tables():
    n = FFT_MAJOR * FFT_MINOR
    half = FFT_MAJOR // 2
    n2 = np.arange(half)
    a = np.zeros((FFT_MAJOR, half))
    a[:half] = np.cos(2 * np.pi * np.arange(half)[:, None] * n2[None, :] / FFT_MAJOR)
    a[half] = (-1.0) ** n2
    k2 = np.arange(1, half)
    a[half + 1:] = -np.sin(2 * np.pi * k2[:, None] * n2[None, :] / FFT_MAJOR)
    d = np.zeros((half, FFT_MAJOR))
    d[:, 0] = 1.0 / n
    d[:, 1:half] = 2 * np.cos(2 * np.pi * n2[:, None] * k2[None, :] / FFT_MAJOR) / n
    d[:, half] = (-1.0) ** n2 / n
    d[:, half + 1:] = -2 * np.sin(2 * np.pi * n2[:, None] * k2[None, :] / FFT_MAJOR) / n
    eye = np.eye(SUBLANES)
    ak = np.kron(a, eye)
    dk = np.kron(d, eye)
    s = np.arange(N_SLABS)[:, None, None]
    k1 = np.arange(FFT_MINOR)[None, :, None]
    n1 = np.arange(FFT_MINOR)[None, None, :]
    g = np.exp(-2j * np.pi * n1 * (FFT_MAJOR * k1 + s) / n)
    mf = np.block([[g.real, -g.imag], [g.imag, g.real]])
    gi = np.conj(g).transpose(0, 2, 1)
    mi = np.block([[gi.real, -gi.imag], [gi.imag, gi.real]])
    return {name: m.astype(np.float32) for name, m in (("ak", ak), ("dk", dk), ("mf", mf), ("mi", mi))}


def _dft_hi_lo(name):
    return _split(jnp.asarray(_dft_tables()[name]))


def _ada_body(c_ref, w_ref, b_ref, o_ref):
    o_ref[...] = _mm3(_silu(c_ref[...]), w_ref[...]) + b_ref[...]


def _ada(cc, ada_w, ada_b):
    rows, d = cc.shape
    n = ada_w.shape[1]
    tn = 1536
    return pl.pallas_call(
        _ada_body, out_shape=SDS((rows, n), F32), grid=(n // tn,),
        in_specs=[BS((rows, d), lambda j: (0, 0)), BS((d, tn), lambda j: (0, j)), BS((1, tn), lambda j: (0, j))],
        out_specs=BS((rows, tn), lambda j: (0, j)), compiler_params=_cparams(1), name="ada_mod",
    )(cc, ada_w, ada_b.reshape(1, n))


def _head_rms(t, bd):
    hi, lo = _split(t * t)
    return t * lax.rsqrt(_mm(hi, bd) + _mm(lo, bd) + EPS)


def _rope(t, cos, sin):
    n = t.shape[1]
    quarter = HEAD_DIM // 4
    lane = lax.broadcasted_iota(I32, t.shape, 1)
    up = pltpu.roll(t, n - quarter, axis=1)
    dn = pltpu.roll(t, quarter, axis=1)
    partner = jnp.where((lane & (2 * quarter - 1)) < quarter, up, dn)
    return t * cos + partner * sin


def _inproj_body(x_ref, sh_ref, sc_ref, nw_ref, w_ref, qnw_ref, knw_ref, cos_ref, sin_ref, bdq_ref, bdk_ref,
                 q_ref, k_ref, v_ref, u_ref, *, att_w, kv_w):
    h = _rms(x_ref[0]) * nw_ref[...]
    h = h * (1.0 + sc_ref[0]) + sh_ref[0]
    proj = _mm(h.astype(BF16), w_ref[...])
    cos, sin = cos_ref[...], sin_ref[...]
    reps = att_w // LANES
    cos_q = jnp.concatenate([cos] * reps, axis=1)
    sin_q = jnp.concatenate([sin] * reps, axis=1)
    q = _head_rms(proj[:, :att_w], bdq_ref[...]) * qnw_ref[...]
    q_ref[0] = (_rope(q, cos_q, sin_q) * (HEAD_DIM ** -0.5)).astype(BF16)
    k = _head_rms(proj[:, att_w:att_w + kv_w], bdk_ref[...]) * knw_ref[...]
    k_ref[0] = _rope(k, cos, sin).astype(BF16)
    v_ref[0] = proj[:, att_w + kv_w:att_w + 2 * kv_w].astype(BF16)
    hy_off = att_w + 2 * kv_w
    for j in range(u_ref.shape[1]):
        u_ref[0, j] = proj[:, hy_off + LANES * j:hy_off + LANES * (j + 1)]


def _inproj(x, sh, sc, norm_w, w_bf, qnw, knw, att_w, kv_w):
    bsz, seq, d = x.shape
    tm = ROW_TILE
    hy_tiles = (w_bf.shape[1] - att_w - 2 * kv_w) // LANES
    cos, sin = _rope_tables(seq)
    row = lambda b, i: (b, i, 0)
    fixed2 = lambda b, i: (0, 0)
    per_b = lambda b, i: (b, 0, 0)
    return pl.pallas_call(
        functools.partial(_inproj_body, att_w=att_w, kv_w=kv_w),
        out_shape=(SDS((bsz, seq, att_w), BF16), SDS((bsz, seq, kv_w), BF16), SDS((bsz, seq, kv_w), BF16),
                   SDS((bsz, hy_tiles, seq, LANES), F32)),
        grid=(bsz, seq // tm),
        in_specs=[BS((1, tm, d), row), BS((1, 1, d), per_b), BS((1, 1, d), per_b), BS((1, d), fixed2),
                  BS(w_bf.shape, fixed2), BS((1, att_w), fixed2), BS((1, kv_w), fixed2),
                  BS((tm, LANES), lambda b, i: (i, 0)), BS((tm, LANES), lambda b, i: (i, 0)),
                  BS((att_w, att_w), fixed2), BS((kv_w, kv_w), fixed2)],
        out_specs=(BS((1, tm, att_w), row), BS((1, tm, kv_w), row), BS((1, tm, kv_w), row),
                   BS((1, hy_tiles, tm, LANES), lambda b, i: (b, 0, i, 0))),
        compiler_params=_cparams(2), name="in_proj",
    )(x, sh, sc, norm_w.reshape(1, d), w_bf, qnw, knw, jnp.asarray(cos), jnp.asarray(sin),
      jnp.asarray(_block_diag_mean(att_w)), jnp.asarray(_block_diag_mean(kv_w)))


def _ctxproj_body(x_ref, sh_ref, sc_ref, nw_ref, w_ref, knw_ref, bdk_ref, k_ref, v_ref, *, kv_w):
    h = _rms(x_ref[0]) * nw_ref[...]
    h = h * (1.0 + sc_ref[...]) + sh_ref[...]
    proj = _mm(h.astype(BF16), w_ref[...])
    k_ref[0] = (_head_rms(proj[:, :kv_w], bdk_ref[...]) * knw_ref[...]).astype(BF16)
    v_ref[0] = proj[:, kv_w:].astype(BF16)


def _ctxproj(ctx, sh, sc, norm_w, w_kv_bf, knw, kv_w):
    bsz, n_ctx, d = ctx.shape
    fixed2 = lambda b: (0, 0)
    row = lambda b: (b, 0, 0)
    return pl.pallas_call(
        functools.partial(_ctxproj_body, kv_w=kv_w),
        out_shape=(SDS((bsz, n_ctx, kv_w), BF16), SDS((bsz, n_ctx, kv_w), BF16)), grid=(bsz,),
        in_specs=[BS((1, n_ctx, d), row), BS((1, d), fixed2), BS((1, d), fixed2), BS((1, d), fixed2),
                  BS(w_kv_bf.shape, fixed2), BS((1, kv_w), fixed2), BS((kv_w, kv_w), fixed2)],
        out_specs=(BS((1, n_ctx, kv_w), row), BS((1, n_ctx, kv_w), row)),
        compiler_params=_cparams(1), name="ctx_proj",
    )(ctx, sh, sc, norm_w.reshape(1, d), w_kv_bf, knw, jnp.asarray(_block_diag_mean(kv_w)))


def _attn_body(sink_ref, q_ref, kp_ref, kc_ref, kn_ref, vp_ref, vc_ref, vn_ref, kx_ref, vx_ref, gw_ref, o_ref):
    i = pl.program_id(1)
    last = pl.num_programs(1) - 1
    n_ctx = kx_ref.shape[1]
    n_keys = n_ctx + 3 * Q_BLOCK
    n_heads = q_ref.shape[2] // HEAD_DIM
    group = n_heads // N_KV_HEADS
    rows = group * Q_BLOCK
    r = lax.broadcasted_iota(I32, (rows, n_keys), 0) & (Q_BLOCK - 1)
    j = lax.broadcasted_iota(I32, (rows, n_keys), 1) - n_ctx
    in_band = (j >= r) & (j <= r + 2 * WINDOW)
    in_seq = jnp.logical_or(j >= Q_BLOCK, i > 0) & jnp.logical_or(j < 2 * Q_BLOCK, i < last)
    valid = (j < 0) | (in_band & in_seq)
    head_of_row = lax.broadcasted_iota(I32, (rows, 1), 0) // Q_BLOCK
    outs = []
    for g in range(N_KV_HEADS):
        sl = slice(g * HEAD_DIM, (g + 1) * HEAD_DIM)
        kb = jnp.concatenate([kx_ref[0][:, sl], kp_ref[0][:, sl], kc_ref[0][:, sl], kn_ref[0][:, sl]], axis=0)
        vb = jnp.concatenate([vx_ref[0][:, sl], vp_ref[0][:, sl], vc_ref[0][:, sl], vn_ref[0][:, sl]], axis=0)
        heads = range(g * group, (g + 1) * group)
        q4 = jnp.concatenate([q_ref[0][:, h * HEAD_DIM:(h + 1) * HEAD_DIM] for h in heads], axis=0)
        sink = jnp.zeros((rows, 1), F32)
        for hh, h in enumerate(heads):
            sink = jnp.where(head_of_row == hh, sink_ref[h], sink)
        s = jnp.where(valid, _mm_nt(q4, kb), NEG_INF)
        m = jnp.maximum(jnp.max(s, axis=-1, keepdims=True), sink)
        p = jnp.exp(s - m)
        denom = jnp.sum(p, axis=-1, keepdims=True) + jnp.exp(sink - m)
        o4 = _mm(p.astype(BF16), vb) / denom
        for hh in range(group):
            outs.append(o4[hh * Q_BLOCK:(hh + 1) * Q_BLOCK])
    att = jnp.concatenate(outs, axis=1)
    o_ref[0] = (_rms(att) * gw_ref[...]).astype(BF16)


def _attention(sink, q, k, v, kx, vx, gw):
    bsz, seq, att_w = q.shape
    kv_w = k.shape[2]
    n_ctx = kx.shape[1]
    nb = seq // Q_BLOCK
    cur = lambda b, i: (b, i, 0)
    prev = lambda b, i: (b, jnp.maximum(i - 1, 0), 0)
    nxt = lambda b, i: (b, jnp.minimum(i + 1, nb - 1), 0)
    per_b = lambda b, i: (b, 0, 0)
    kvb = (1, Q_BLOCK, kv_w)
    return pl.pallas_call(
        _attn_body, out_shape=SDS((bsz, seq, att_w), BF16), grid=(bsz, nb),
        in_specs=[BS(memory_space=pltpu.SMEM), BS((1, Q_BLOCK, att_w), cur),
                  BS(kvb, prev), BS(kvb, cur), BS(kvb, nxt), BS(kvb, prev), BS(kvb, cur), BS(kvb, nxt),
                  BS((1, n_ctx, kv_w), per_b), BS((1, n_ctx, kv_w), per_b), BS((1, att_w), lambda b, i: (0, 0))],
        out_specs=BS((1, Q_BLOCK, att_w), cur), compiler_params=_cparams(2), name="window_attn",
    )(sink, q, k, k, k, v, v, v, kx, vx, gw)


def _filter_body(f_ref, w1h, w1l, b1, w2h, w2l, b2, w3h, w3l, sf_ref, dl_ref, o_ref, *, hy_w):
    def mm_w(a, wh, wl):
        a_hi, a_lo = _split(a)
        return _mm(a_hi, wh[...]) + _mm(a_hi, wl[...]) + _mm(a_lo, wh[...])

    f = f_ref[...]
    h = jnp.sin(sf_ref[0:1, :] * (mm_w(f, w1h, w1l) + b1[...]))
    h = jnp.sin(sf_ref[1:2, :] * (mm_w(h, w2h, w2l) + b2[...]))
    h = mm_w(h, w3h, w3l)
    h = h * jnp.exp(-f[:, 0:1] * dl_ref[...])
    tl = f.shape[0]
    row = lax.broadcasted_iota(I32, h.shape, 0) + pl.program_id(0) * tl
    col = lax.broadcasted_iota(I32, h.shape, 1)
    is_bwd = ((col // hy_w) & 1) == 1
    h = jnp.where((row == 0) & is_bwd, 0.0, h)
    for j in range(o_ref.shape[0]):
        o_ref[j] = h[:, LANES * j:LANES * (j + 1)]


def _filter_mlp(seq, w1, b1, w2, b2, w3, sin_freq, hy_w):
    hid = w1.shape[1]
    fw = 64
    feat = jnp.asarray(_filter_features(seq, fw))
    w1p = jnp.zeros((fw, hid), F32).at[:w1.shape[0]].set(w1)
    n_out = w3.shape[1]
    delta = jnp.asarray(np.tile(_decay_rates(hy_w), n_out // hy_w)).reshape(1, n_out)
    tl = FILTER_TILE
    fixed = lambda i: (0, 0)
    ops = []
    for w in (w1p, w2, w3):
        ops.extend(_split(w))
    return pl.pallas_call(
        functools.partial(_filter_body, hy_w=hy_w),
        out_shape=SDS((n_out // LANES, seq, LANES), F32), grid=(seq // tl,),
        in_specs=[BS((tl, fw), lambda i: (i, 0)), BS((fw, hid), fixed), BS((fw, hid), fixed), BS((1, hid), fixed),
                  BS((hid, hid), fixed), BS((hid, hid), fixed), BS((1, hid), fixed),
                  BS((hid, n_out), fixed), BS((hid, n_out), fixed), BS((2, hid), fixed), BS((1, n_out), fixed)],
        out_specs=BS((n_out // LANES, tl, LANES), lambda i: (0, i, 0)),
        compiler_params=_cparams(1), name="hyena_filter_mlp",
    )(feat, ops[0], ops[1], b1.reshape(1, hid), ops[2], ops[3], b2.reshape(1, hid), ops[4], ops[5], sin_freq, delta)


C_TILES = 2
C_GROUP = C_TILES * LANES


def _load_cat(ref, row0, n_rows):
    return jnp.concatenate([ref[c, pl.ds(row0, n_rows), :] for c in range(C_TILES)], axis=1)


def _store_cat(ref, row0, n_rows, val):
    for c in range(C_TILES):
        ref[c, pl.ds(row0, n_rows), :] = val[:, c * LANES:(c + 1) * LANES]


def _fwd_major(src, p_ref, ak, passes):
    half = FFT_MAJOR // 2

    def body(g, carry):
        r0 = pl.multiple_of(g * SUBLANES, SUBLANES)
        st = jnp.concatenate([_load_cat(src, n2 * FFT_MINOR + r0, SUBLANES) for n2 in range(half)], axis=0)
        out = _mm_split(ak[0][...], ak[1][...] if passes == 3 else None, st, passes)
        for p in range(FFT_MAJOR):
            _store_cat(p_ref, p * FFT_MINOR + r0, SUBLANES, out[p * SUBLANES:(p + 1) * SUBLANES])
        return carry

    lax.fori_loop(0, FFT_MINOR // SUBLANES, body, 0, unroll=2)


def _inv_major(p_ref, dst, dk, passes):
    half = FFT_MAJOR // 2

    def body(g, carry):
        r0 = pl.multiple_of(g * SUBLANES, SUBLANES)
        st = jnp.concatenate([_load_cat(p_ref, p * FFT_MINOR + r0, SUBLANES) for p in range(FFT_MAJOR)], axis=0)
        out = _mm_split(dk[0][...], dk[1][...] if passes == 3 else None, st, passes)
        for n2 in range(half):
            _store_cat(dst, n2 * FFT_MINOR + r0, SUBLANES, out[n2 * SUBLANES:(n2 + 1) * SUBLANES])
        return carry

    lax.fori_loop(0, FFT_MINOR // SUBLANES, body, 0, unroll=2)


def _slab_spectrum(p_ref, mf, s, passes, real_slot=None):
    m = FFT_MINOR
    if real_slot is None:
        st = jnp.concatenate([_load_cat(p_ref, s * m, m), _load_cat(p_ref, (FFT_MAJOR // 2 + s) * m, m)], axis=0)
        x = _mm_split(mf[0][s], mf[1][s] if passes == 3 else None, st, passes)
    else:
        st = _load_cat(p_ref, real_slot * m, m)
        x = _mm_split(mf[0][s, :, 0:m], mf[1][s, :, 0:m] if passes == 3 else None, st, passes)
    return x[:m], x[m:]


def _spectrum_body(f_ref, b_ref, akh, akl, mfh, mfl, o_ref, p_ref):
    half = FFT_MAJOR // 2
    ak, mf = (akh, akl), (mfh, mfl)
    for is_bwd, src in ((False, f_ref), (True, b_ref)):
        _fwd_major(src, p_ref, ak, 3)

        def put(s, re, im, is_bwd=is_bwd):
            if is_bwd:
                o_ref[0, 0, s, 0] = o_ref[0, 0, s, 0] + re
                o_ref[0, 0, s, 1] = o_ref[0, 0, s, 1] - im
            else:
                o_ref[0, 0, s, 0] = re
                o_ref[0, 0, s, 1] = im

        put(0, *_slab_spectrum(p_ref, mf, 0, 3, real_slot=0))
        put(half, *_slab_spectrum(p_ref, mf, half, 3, real_slot=half))

        def body(s, carry):
            put(s, *_slab_spectrum(p_ref, mf, s, 3))
            return carry

        lax.fori_loop(1, half, body, 0, unroll=2)


def _spectrum(hf, seq, hy_w):
    ak, mf = _dft_hi_lo("ak"), _dft_hi_lo("mf")
    n_grp = hy_w // C_GROUP
    fixed2 = lambda o, c: (0, 0)
    fixed3 = lambda o, c: (0, 0, 0)
    one = pl.Buffered(1)
    return pl.pallas_call(
        _spectrum_body,
        out_shape=SDS((HYENA_ORDER, n_grp, N_SLABS, 2, FFT_MINOR, C_GROUP), F32), grid=(HYENA_ORDER, n_grp),
        in_specs=[BS((C_TILES, seq, LANES), lambda o, c: (o * 2 * n_grp + c, 0, 0), pipeline_mode=one),
                  BS((C_TILES, seq, LANES), lambda o, c: (o * 2 * n_grp + n_grp + c, 0, 0), pipeline_mode=one),
                  BS(ak[0].shape, fixed2, pipeline_mode=one), BS(ak[0].shape, fixed2, pipeline_mode=one),
                  BS(mf[0].shape, fixed3, pipeline_mode=one), BS(mf[0].shape, fixed3, pipeline_mode=one)],
        out_specs=BS((1, 1, N_SLABS, 2, FFT_MINOR, C_GROUP), lambda o, c: (o, c, 0, 0, 0, 0)),
        scratch_shapes=[pltpu.VMEM((C_TILES, FFT_MAJOR * FFT_MINOR, LANES), F32)],
        compiler_params=_cparams(2), name="hyena_spectrum",
    )(hf, hf, ak[0], ak[1], mf[0], mf[1])


CONV_CHUNK = 128
CONV_PASSES = 1


def _sconv_chunk(ref, c, r0, n_rows, seq, prm_ref, base):
    cl = slice(c * LANES, (c + 1) * LANES)
    cur = ref[c, pl.ds(r0, n_rows), :]
    row = lax.broadcasted_iota(I32, cur.shape, 0)
    before = ref[c, pl.ds(jnp.maximum(r0 - 1, 0), 1), :] * (r0 > 0).astype(F32)
    after = ref[c, pl.ds(jnp.minimum(r0 + n_rows, seq - 1), 1), :] * (r0 + n_rows < seq).astype(F32)
    prev = jnp.where(row == 0, before, pltpu.roll(cur, 1, axis=0))
    nxt = jnp.where(row == n_rows - 1, after, pltpu.roll(cur, n_rows - 1, axis=0))
    w = lambda k: prm_ref[0, 0, base + k:base + k + 1, cl]
    return w(3) + w(0) * prev + w(1) * cur + w(2) * nxt


def _conv_body(z_ref, x_ref, prm_ref, akh, dkh, mfh, mih, h_ref, o_ref, zs_ref, p_ref, *, conv_z):
    seq = z_ref.shape[2]
    half = FFT_MAJOR // 2
    m = FFT_MINOR
    zv, xv = z_ref.at[0], x_ref.at[0]
    ak, dk, mf, mi = (akh, None), (dkh, None), (mfh, None), (mih, None)
    n_chunks = seq // CONV_CHUNK

    if conv_z:
        def prep(i, carry):
            r0 = pl.multiple_of(i * CONV_CHUNK, CONV_CHUNK)
            for c in range(C_TILES):
                zs_ref[c, pl.ds(r0, CONV_CHUNK), :] = _sconv_chunk(zv, c, r0, CONV_CHUNK, seq, prm_ref, 0)
            return carry

        lax.fori_loop(0, n_chunks, prep, 0)
        src = zs_ref
    else:
        src = zv
    _fwd_major(src, p_ref, ak, CONV_PASSES)

    def slab(s, real_slot=None):
        xr, xi = _slab_spectrum(p_ref, mf, s, CONV_PASSES, real_slot)
        hr, hi = h_ref[0, 0, s, 0], h_ref[0, 0, s, 1]
        y = jnp.concatenate([xr * hr - xi * hi, xr * hi + xi * hr], axis=0)
        if real_slot is None:
            out = _mm_split(mi[0][s], None, y, CONV_PASSES)
            _store_cat(p_ref, s * m, m, out[:m])
            _store_cat(p_ref, (half + s) * m, m, out[m:])
        else:
            out = _mm_split(mi[0][s, 0:m, :], None, y, CONV_PASSES)
            _store_cat(p_ref, real_slot * m, m, out)

    slab(0, real_slot=0)
    slab(half, real_slot=half)

    def slab_loop(s, carry):
        slab(s)
        return carry

    lax.fori_loop(1, half, slab_loop, 0, unroll=4)
    _inv_major(p_ref, zs_ref, dk, CONV_PASSES)

    def fin(i, carry):
        r0 = pl.multiple_of(i * CONV_CHUNK, CONV_CHUNK)
        for c in range(C_TILES):
            cl = slice(c * LANES, (c + 1) * LANES)
            gate = _sconv_chunk(xv, c, r0, CONV_CHUNK, seq, prm_ref, 4)
            if conv_z:
                zval = _sconv_chunk(zv, c, r0, CONV_CHUNK, seq, prm_ref, 0)
            else:
                zval = zv[c, pl.ds(r0, CONV_CHUNK), :]
            skip = prm_ref[0, 0, 8:9, cl]
            o_ref[0, c, pl.ds(r0, CONV_CHUNK), :] = gate * (zs_ref[c, pl.ds(r0, CONV_CHUNK), :] + skip * zval)
        return carry

    lax.fori_loop(0, n_chunks, fin, 0, unroll=2)


def _hyena_order(z, z_grp0, x, x_grp0, prm, order, spec, conv_z, n_grp):
    bsz, _, seq, _ = z.shape
    ak, dk, mf, mi = (_dft_hi_lo(n)[0] for n in ("ak", "dk", "mf", "mi"))
    one = pl.Buffered(1)
    fixed2 = lambda c, b: (0, 0)
    fixed3 = lambda c, b: (0, 0, 0)
    blk = (1, C_TILES, seq, LANES)
    return pl.pallas_call(
        functools.partial(_conv_body, conv_z=conv_z),
        out_shape=SDS((bsz, n_grp * C_TILES, seq, LANES), F32), grid=(n_grp, bsz),
        in_specs=[BS(blk, lambda c, b: (b, z_grp0 + c, 0, 0), pipeline_mode=one),
                  BS(blk, lambda c, b: (b, x_grp0 + c, 0, 0), pipeline_mode=one),
                  BS((1, 1, 16, C_GROUP), lambda c, b: (order, c, 0, 0)),
                  BS(ak.shape, fixed2, pipeline_mode=one), BS(dk.shape, fixed2, pipeline_mode=one),
                  BS(mf.shape, fixed3, pipeline_mode=one), BS(mi.shape, fixed3, pipeline_mode=one),
                  BS((1, 1, N_SLABS, 2, FFT_MINOR, C_GROUP), lambda c, b: (order, c, 0, 0, 0, 0), pipeline_mode=one)],
        out_specs=BS(blk, lambda c, b: (b, c, 0, 0)),
        scratch_shapes=[pltpu.VMEM((C_TILES, seq, LANES), F32),
                        pltpu.VMEM((C_TILES, FFT_MAJOR * FFT_MINOR, LANES), F32)],
        compiler_params=_cparams(2), name=f"hyena_conv{order}",
    )(z, x, prm, ak, dk, mf, mi, spec)


ROUTER_ROWS = 48
EXPERT_ROW0 = 8


def _outproj_body(att_ref, hy_ref, gwh_ref, wo_ref, x_ref, g1_ref, sh2_ref, sc2_ref, n2w_ref, wrh_ref, wrl_ref,
                  br_ref, tri_ref, x1_ref, h2_ref, ids_ref, gates_ref, cnt_ref, carry_ref):
    @pl.when((pl.program_id(0) == 0) & (pl.program_id(1) == 0))
    def _():
        carry_ref[...] = jnp.zeros_like(carry_ref)

    hy = jnp.concatenate([hy_ref[0, j] for j in range(hy_ref.shape[1])], axis=1)
    hyn = _rms(hy) * gwh_ref[...]
    mix = _mm(jnp.concatenate([att_ref[0], hyn.astype(BF16)], axis=1), wo_ref[...])
    x1 = x_ref[0] + g1_ref[0] * mix
    x1_ref[0] = x1
    h2 = _rms(x1) * n2w_ref[...]
    h2 = h2 * (1.0 + sc2_ref[0]) + sh2_ref[0]
    h2_ref[...] = h2

    h_hi, h_lo = _split(h2)
    lg = _mm_nt(wrh_ref[...], h_hi) + _mm_nt(wrh_ref[...], h_lo) + _mm_nt(wrl_ref[...], h_hi) + br_ref[...]
    tm = lg.shape[1]
    gl = lg[0:N_GROUPS]
    el = lg[EXPERT_ROW0:EXPERT_ROW0 + N_EXPERTS]
    gmax = jnp.max(gl, axis=0, keepdims=True)
    rg = lax.broadcasted_iota(I32, gl.shape, 0).astype(F32)
    g_idx = jnp.min(jnp.where(gl == gmax, rg, float(N_GROUPS)), axis=0, keepdims=True)
    g_val = 1.0 / jnp.sum(jnp.exp(gl - gmax), axis=0, keepdims=True)
    re_i = lax.broadcasted_iota(I32, el.shape, 0)
    re = re_i.astype(F32)
    in_group = (re_i // EXPERTS_PER_GROUP).astype(F32) == g_idx
    elm = jnp.where(in_group, el, NEG_INF)
    m1 = jnp.max(elm, axis=0, keepdims=True)
    i1 = jnp.min(jnp.where(elm == m1, re, float(N_EXPERTS)), axis=0, keepdims=True)
    elm2 = jnp.where(re == i1, NEG_INF, elm)
    m2 = jnp.max(elm2, axis=0, keepdims=True)
    i2 = jnp.min(jnp.where(elm2 == m2, re, float(N_EXPERTS)), axis=0, keepdims=True)
    e2 = jnp.exp(m2 - m1)
    inv = g_val / (1.0 + e2)

    oh1 = (re == i1).astype(F32)
    oh2 = (re == i2).astype(F32)
    oh = oh1 + oh2
    base = _mm(oh.astype(BF16), tri_ref[...]) + carry_ref[:, 0:1]
    rank1 = jnp.sum(oh1 * base, axis=0, keepdims=True)
    rank2 = jnp.sum(oh2 * base, axis=0, keepdims=True)
    carry_ref[...] = carry_ref[...] + jnp.sum(oh, axis=1, keepdims=True)
    cnt_ref[...] = carry_ref[...]

    zi = jnp.zeros((1, tm), I32)
    for k, val in enumerate((i1.astype(I32), i2.astype(I32), rank1.astype(I32), rank2.astype(I32), zi, zi, zi, zi)):
        ids_ref[k:k + 1, :] = val
    zf = jnp.zeros((1, tm), F32)
    for k, val in enumerate((inv, inv * e2, zf, zf, zf, zf, zf, zf)):
        gates_ref[k:k + 1, :] = val


def _outproj_route(att_n, hy, gw_hy, wo_bf, x, g1, sh2, sc2, norm2_w, wr_hi, wr_lo, br):
    bsz, seq, d = x.shape
    tm = ROW_TILE
    n_tok = bsz * seq
    att_w = att_n.shape[2]
    hy_tiles = hy.shape[1]
    tri = jnp.asarray(np.triu(np.ones((tm, tm), np.float32), k=1).astype(ml_dtypes.bfloat16))
    row = lambda b, i: (b, i, 0)
    per_b = lambda b, i: (b, 0, 0)
    fixed2 = lambda b, i: (0, 0)
    tok = lambda b, i: (0, b * (seq // tm) + i)
    return pl.pallas_call(
        _outproj_body,
        out_shape=(SDS((bsz, seq, d), F32), SDS((n_tok, d), F32), SDS((SUBLANES, n_tok), I32),
                   SDS((SUBLANES, n_tok), F32), SDS((N_EXPERTS, LANES), F32)),
        grid=(bsz, seq // tm),
        in_specs=[BS((1, tm, att_w), row), BS((1, hy_tiles, tm, LANES), lambda b, i: (b, 0, i, 0)),
                  BS((1, hy_tiles * LANES), fixed2), BS(wo_bf.shape, fixed2), BS((1, tm, d), row),
                  BS((1, 1, d), per_b), BS((1, 1, d), per_b), BS((1, 1, d), per_b), BS((1, d), fixed2),
                  BS((ROUTER_ROWS, d), fixed2), BS((ROUTER_ROWS, d), fixed2), BS((ROUTER_ROWS, 1), fixed2),
                  BS((tm, tm), fixed2)],
        out_specs=(BS((1, tm, d), row), BS((tm, d), lambda b, i: (b * (seq // tm) + i, 0)),
                   BS((SUBLANES, tm), tok), BS((SUBLANES, tm), tok), BS((N_EXPERTS, LANES), fixed2)),
        scratch_shapes=[pltpu.VMEM((N_EXPERTS, LANES), F32)],
        compiler_params=_cparams(2), name="out_proj_router",
    )(att_n, hy, gw_hy, wo_bf, x, g1, sh2, sc2, norm2_w.reshape(1, d), wr_hi, wr_lo, br, tri)


ISSUE_UNROLL = 8


def _row_copy(src, src_row, dst, dst_row, sem):
    return pltpu.make_async_copy(src.at[pl.ds(src_row, 1), :], dst.at[pl.ds(dst_row, 1), :], sem)


def _dispatch_body(d_ref, h_ref, zero_ref, buf_ref, sem):
    del zero_ref
    n = h_ref.shape[0]

    def issue(t, carry):
        for j in range(TOP_K):
            _row_copy(h_ref, t, buf_ref, d_ref[j, t], sem).start()
        return carry

    lax.fori_loop(0, n, issue, 0, unroll=ISSUE_UNROLL)
    for j in range(TOP_K):
        pltpu.make_async_copy(h_ref, buf_ref.at[pl.ds(0, n), :], sem).wait()


def _dispatch(dest, h2, n_rows):
    n_tok, d = h2.shape
    tm = GATHER_TILE
    return pl.pallas_call(
        _dispatch_body, out_shape=SDS((n_rows, d), F32), grid=(n_tok // tm,),
        in_specs=[BS((TOP_K, tm), lambda i: (0, i), memory_space=pltpu.SMEM), BS((tm, d), lambda i: (i, 0)),
                  BS(memory_space=pl.ANY)],
        out_specs=BS(memory_space=pl.ANY), scratch_shapes=[pltpu.SemaphoreType.DMA(())],
        input_output_aliases={2: 0}, compiler_params=_cparams(1), name="moe_dispatch",
    )(dest, h2, jnp.zeros((n_rows, d), F32))


def _expert_body(be_ref, nu_ref, x_ref, w13_ref, w2_ref, y_ref):
    del be_ref
    i = pl.program_id(0)
    hidden = w2_ref.shape[1]

    @pl.when(i < nu_ref[0])
    def _():
        h = _mm(x_ref[...].astype(BF16), w13_ref[0])
        act = _silu(h[:, :hidden]) * h[:, hidden:]
        y_ref[...] = _mm(act.astype(BF16), w2_ref[0])

    @pl.when(i >= nu_ref[0])
    def _():
        y_ref[...] = jnp.zeros_like(y_ref)


def _experts(block_expert, n_used, buf, w13_bf, w2_bf):
    n_rows, d = buf.shape
    nb = n_rows // MOE_BLOCK
    grid_spec = pltpu.PrefetchScalarGridSpec(
        num_scalar_prefetch=2, grid=(nb,),
        in_specs=[BS((MOE_BLOCK, d), lambda i, be, nu: (i, 0)),
                  BS((1,) + w13_bf.shape[1:], lambda i, be, nu: (be[i], 0, 0)),
                  BS((1,) + w2_bf.shape[1:], lambda i, be, nu: (be[i], 0, 0))],
        out_specs=BS((MOE_BLOCK, d), lambda i, be, nu: (i, 0)))
    return pl.pallas_call(_expert_body, out_shape=SDS((n_rows, d), F32), grid_spec=grid_spec,
                          compiler_params=_cparams(1), name="moe_experts")(block_expert, n_used, buf, w13_bf, w2_bf)


def _combine_body(d_ref, y_ref, x1_ref, g2_ref, gt_ref, o_ref, rows_ref, sem):
    n = x1_ref.shape[0]

    def issue(t, carry):
        for j in range(TOP_K):
            _row_copy(y_ref, d_ref[j, t], rows_ref.at[j], t, sem).start()
        return carry

    lax.fori_loop(0, n, issue, 0, unroll=ISSUE_UNROLL)
    for j in range(TOP_K):
        pltpu.make_async_copy(y_ref.at[pl.ds(0, n), :], rows_ref.at[j], sem).wait()
    gt = gt_ref[...]
    moe = gt[:, 0:1] * rows_ref[0] + gt[:, 1:2] * rows_ref[1]
    o_ref[...] = x1_ref[...] + g2_ref[0] * moe


def _combine(dest, y, x1, g2, gates_t, seq):
    n_tok, d = x1.shape
    tm = GATHER_TILE
    return pl.pallas_call(
        _combine_body, out_shape=SDS((n_tok, d), F32), grid=(n_tok // tm,),
        in_specs=[BS((TOP_K, tm), lambda i: (0, i), memory_space=pltpu.SMEM), BS(memory_space=pl.ANY),
                  BS((tm, d), lambda i: (i, 0)), BS((1, 1, d), lambda i: (i // (seq // tm), 0, 0)),
                  BS((tm, TOP_K), lambda i: (i, 0))],
        out_specs=BS((tm, d), lambda i: (i, 0)),
        scratch_shapes=[pltpu.VMEM((TOP_K, tm, d), F32), pltpu.SemaphoreType.DMA(())],
        compiler_params=_cparams(1), name="moe_combine",
    )(dest, y, x1, g2, gates_t)


def kernel(x, c, ctx, c_ctx, ada_w, ada_b, norm1_w, w_in, q_norm_w, k_norm_w, attn_sink, hy_conv_w, hy_conv_b, hy_w1,
           hy_b1, hy_w2, hy_b2, hy_w3, hy_sin_freq, hy_skip, group_norm_w, w_out, norm2_w, router_g_w, router_g_b,
           router_e_w, router_e_b, exp_w1, exp_w3, exp_w2):
    bsz, seq, d = x.shape
    assert ada_w.shape[0] == 1, "single-layer block"
    att_w = d // 2
    hy_w = d - att_w
    kv_w = N_KV_HEADS * HEAD_DIM
    n_heads = att_w // HEAD_DIM
    n_grp = hy_w // C_GROUP
    assert 2 * seq == FFT_MAJOR * FFT_MINOR and seq % ROW_TILE == 0 and hy_w % C_GROUP == 0
    assert w_in.shape[2] == att_w + 2 * kv_w + (HYENA_ORDER + 1) * hy_w

    pad = (-(bsz + 1)) % SUBLANES
    cc = jnp.concatenate([c, c_ctx[None, :], jnp.zeros((pad, d), F32)], axis=0)
    mod = _ada(cc, ada_w[0], ada_b[0])
    lat = [mod[:bsz, k * d:(k + 1) * d].reshape(bsz, 1, d) for k in range(6)]
    sh1, sc1, g1, sh2, sc2, g2 = lat
    csh1 = mod[bsz:bsz + 1, 0:d]
    csc1 = mod[bsz:bsz + 1, d:2 * d]

    w_in_bf = w_in[0].astype(BF16)
    qnw = jnp.tile(q_norm_w[0], n_heads).reshape(1, att_w)
    knw = jnp.tile(k_norm_w[0], N_KV_HEADS).reshape(1, kv_w)
    q, k, v, u = _inproj(x, sh1, sc1, norm1_w[0], w_in_bf, qnw, knw, att_w, kv_w)
    kx, vx = _ctxproj(ctx, csh1, csc1, norm1_w[0], w_in_bf[:, att_w:att_w + 2 * kv_w], knw, kv_w)

    gw = group_norm_w[0]
    att_n = _attention(attn_sink[0], q, k, v, kx, vx, gw[:att_w].reshape(1, att_w))

    hf = _filter_mlp(seq, hy_w1[0], hy_b1[0], hy_w2[0], hy_b2[0], hy_w3[0], hy_sin_freq[0], hy_w)
    spec = _spectrum(hf, seq, hy_w)
    cw = hy_conv_w[0].reshape(3, HYENA_ORDER + 1, n_grp, C_GROUP)
    cb = hy_conv_b[0].reshape(1, HYENA_ORDER + 1, n_grp, C_GROUP)
    taps = jnp.concatenate([cw, cb], axis=0)
    prm = jnp.zeros((HYENA_ORDER, n_grp, 16, C_GROUP), F32)
    for o in range(HYENA_ORDER):
        prm = prm.at[o, :, 4:8].set(jnp.transpose(taps[:, o + 1], (1, 0, 2)))
        prm = prm.at[o, :, 8].set(hy_skip[0, o].reshape(n_grp, C_GROUP))
    prm = prm.at[0, :, 0:4].set(jnp.transpose(taps[:, 0], (1, 0, 2)))
    z = _hyena_order(u, 0, u, n_grp, prm, 0, spec, True, n_grp)
    hy = _hyena_order(z, 0, u, 2 * n_grp, prm, 1, spec, False, n_grp)

    wr = jnp.zeros((ROUTER_ROWS, d), F32)
    wr = wr.at[0:N_GROUPS].set(router_g_w[0].T).at[EXPERT_ROW0:EXPERT_ROW0 + N_EXPERTS].set(router_e_w[0].T)
    br = jnp.zeros((ROUTER_ROWS, 1), F32)
    br = br.at[0:N_GROUPS, 0].set(router_g_b[0]).at[EXPERT_ROW0:EXPERT_ROW0 + N_EXPERTS, 0].set(router_e_b[0])
    wr_hi, wr_lo = _split(wr)
    x1, h2, ids, gates, counts = _outproj_route(att_n, hy, gw[att_w:].reshape(1, hy_w), w_out[0].astype(BF16), x,
                                                g1, sh2, sc2, norm2_w[0], wr_hi, wr_lo, br)

    n_tok = bsz * seq
    cnt = counts[:, 0].astype(I32)
    padded = (cnt + MOE_BLOCK - 1) // MOE_BLOCK * MOE_BLOCK
    pend = jnp.cumsum(padded)
    pstart = pend - padded
    e_iota = jnp.arange(N_EXPERTS, dtype=I32)
    is_e = ids[0:TOP_K, :, None] == e_iota
    dest = jnp.sum(jnp.where(is_e, pstart, 0), axis=-1) + ids[TOP_K:2 * TOP_K]
    n_blocks = (n_tok * TOP_K) // MOE_BLOCK + N_EXPERTS
    block_row0 = jnp.arange(n_blocks, dtype=I32) * MOE_BLOCK
    block_expert = jnp.minimum(jnp.sum((block_row0[:, None] >= pend[None, :]).astype(I32), axis=-1), N_EXPERTS - 1)
    n_used = (pend[-1:] // MOE_BLOCK).astype(I32)

    buf = _dispatch(dest, h2, n_blocks * MOE_BLOCK)
    w13 = jnp.concatenate([exp_w1[0], exp_w3[0]], axis=2).astype(BF16)
    y = _experts(block_expert, n_used, buf, w13, exp_w2[0].astype(BF16))
    out = _combine(dest, y, x1.reshape(n_tok, d), g2, gates[0:TOP_K].T, seq)
    return out.reshape(bsz, seq, d)
```

```python
import functools
import math

import ml_dtypes
import numpy as np

import jax
import jax.numpy as jnp
from jax import lax
from jax.experimental import pallas as pl
from jax.experimental.pallas import tpu as pltpu

F32, BF16, I32 = jnp.float32, jnp.bfloat16, jnp.int32
SDS = jax.ShapeDtypeStruct
BS = pl.BlockSpec

HEAD_DIM = 64
N_KV_HEADS = 2
WINDOW = 128
Q_BLOCK = 128
GRID_W = 64
ROPE_THETA = 10000.0
EPS = 1e-6
NEG_INF = -1e30
LOG2E = math.log2(math.e)
HYENA_BANDS = 16
HYENA_ORDER = 2
DECAY_TARGET = 1e-2
FAST_DECAY_PCT = 0.3
SLOW_DECAY_PCT = 1.5
N_GROUPS = 4
EXPERTS_PER_GROUP = 8
N_EXPERTS = N_GROUPS * EXPERTS_PER_GROUP
TOP_K = 2

LANES = 128
SUBLANES = 8
VMEM_LIMIT = 60 * 1024 * 1024

ROW_TILE = 512
MOE_BLOCK = 512
GATHER_TILE = 1024
FILTER_TILE = 512

FFT_MINOR = 128
FFT_MAJOR = 64
N_SLABS = FFT_MAJOR // 2 + 1


def _cparams(n_grid):
    return pltpu.CompilerParams(dimension_semantics=("arbitrary",) * n_grid, vmem_limit_bytes=VMEM_LIMIT)


def _mm(a, b):
    return jnp.dot(a, b, preferred_element_type=F32)


def _mm_nt(a, b):
    return lax.dot_general(a, b, (((1,), (1,)), ((), ())), preferred_element_type=F32)


def _split(x):
    hi = x.astype(BF16)
    lo = (x - hi.astype(F32)).astype(BF16)
    return hi, lo


def _mm_split(w_hi, w_lo, x, passes):
    if passes == 1:
        return _mm(w_hi, x.astype(BF16))
    x_hi, x_lo = _split(x)
    out = _mm(w_hi, x_hi) + _mm(w_hi, x_lo)
    if passes == 3:
        out = out + _mm(w_lo, x_hi)
    return out


def _mm3(a, b):
    a_hi, a_lo = _split(a)
    b_hi, b_lo = _split(b)
    return _mm(a_hi, b_hi) + _mm(a_hi, b_lo) + _mm(a_lo, b_hi)


def _silu(x):
    return x * (1.0 / (1.0 + jnp.exp(-x)))


def _rms(x):
    return x * lax.rsqrt(jnp.mean(x * x, axis=-1, keepdims=True) + EPS)


@functools.lru_cache(maxsize=None)
def _rope_tables(seq_len):
    pos = np.arange(seq_len)
    rows = (pos // GRID_W).astype(np.float64)
    cols = (pos % GRID_W).astype(np.float64)
    pairs = HEAD_DIM // 4
    inv_freq = ROPE_THETA ** (-np.arange(pairs, dtype=np.float64) / pairs)
    ang_r = rows[:, None] * inv_freq[None, :]
    ang_c = cols[:, None] * inv_freq[None, :]
    ang = np.concatenate([ang_r, ang_r, ang_c, ang_c], axis=1)
    sign = np.tile(np.concatenate([-np.ones(pairs), np.ones(pairs)]), 2)
    cos = np.cos(ang)
    sin = np.sin(ang) * sign[None, :]
    reps = LANES // HEAD_DIM
    return np.tile(cos, (1, reps)).astype(np.float32), np.tile(sin, (1, reps)).astype(np.float32)


@functools.lru_cache(maxsize=None)
def _block_diag_mean(width):
    h = np.arange(width) // HEAD_DIM
    return ((h[:, None] == h[None, :]).astype(np.float32) / HEAD_DIM).astype(ml_dtypes.bfloat16)


@functools.lru_cache(maxsize=None)
def _filter_features(seq_len, width):
    pos = np.arange(seq_len, dtype=np.float64)
    t = pos / seq_len
    bands = np.linspace(1e-4, HYENA_BANDS - 1, HYENA_BANDS)
    ang = (2.0 * math.pi / seq_len) * pos[:, None] * bands[None, :]
    feat = np.concatenate([t[:, None], np.cos(ang), -np.sin(ang)], axis=-1)
    out = np.zeros((seq_len, width), np.float32)
    out[:, :feat.shape[1]] = feat
    return out


@functools.lru_cache(maxsize=None)
def _decay_rates(width):
    max_decay = math.log(DECAY_TARGET) / FAST_DECAY_PCT
    min_decay = math.log(DECAY_TARGET) / SLOW_DECAY_PCT
    return np.abs(np.linspace(min_decay, max_decay, width)).astype(np.float32)


@functools.lru_cache(maxsize=None)
def _dft_tables():
    n = FFT_MAJOR * FFT_MINOR
    half = FFT_MAJOR // 2
    n2 = np.arange(half)
    a = np.zeros((FFT_MAJOR, half))
    a[:half] = np.cos(2 * np.pi * np.arange(half)[:, None] * n2[None, :] / FFT_MAJOR)
    a[half] = (-1.0) ** n2
    k2 = np.arange(1, half)
    a[half + 1:] = -np.sin(2 * np.pi * k2[:, None] * n2[None, :] / FFT_MAJOR)
    d = np.zeros((half, FFT_MAJOR))
    d[:, 0] = 1.0 / n
    d[:, 1:half] = 2 * np.cos(2 * np.pi * n2[:, None] * k2[None, :] / FFT_MAJOR) / n
    d[:, half] = (-1.0) ** n2 / n
    d[:, half + 1:] = -2 * np.sin(2 * np.pi * n2[:, None] * k2[None, :] / FFT_MAJOR) / n
    eye = np.eye(SUBLANES)
    ak = np.kron(a, eye)
    dk = np.kron(d, eye)
    s = np.arange(N_SLABS)[:, None, None]
    k1 = np.arange(FFT_MINOR)[None, :, None]
    n1 = np.arange(FFT_MINOR)[None, None, :]
    g = np.exp(-2j * np.pi * n1 * (FFT_MAJOR * k1 + s) / n)
    mf = np.block([[g.real, -g.imag], [g.imag, g.real]])
    gi = np.conj(g).transpose(0, 2, 1)
    mi = np.block([[gi.real, -gi.imag], [gi.imag, gi.real]])
    return {name: m.astype(np.float32) for name, m in (("ak", ak), ("dk", dk), ("mf", mf), ("mi", mi))}


def _dft_hi_lo(name):
    return _split(jnp.asarray(_dft_tables()[name]))


def _ada_body(c_ref, w_ref, b_ref, o_ref):
    o_ref[...] = _mm3(_silu(c_ref[...]), w_ref[...]) + b_ref[...]


def _ada(cc, ada_w, ada_b):
    rows, d = cc.shape
    n = ada_w.shape[1]
    tn = 1536
    return pl.pallas_call(
        _ada_body, out_shape=SDS((rows, n), F32), grid=(n // tn,),
        in_specs=[BS((rows, d), lambda j: (0, 0)), BS((d, tn), lambda j: (0, j)), BS((1, tn), lambda j: (0, j))],
        out_specs=BS((rows, tn), lambda j: (0, j)), compiler_params=_cparams(1), name="ada_mod",
    )(cc, ada_w, ada_b.reshape(1, n))


def _head_rms(t, bd):
    hi, lo = _split(t * t)
    return t * lax.rsqrt(_mm(hi, bd) + _mm(lo, bd) + EPS)


def _rope(t, cos, sin):
    n = t.shape[1]
    quarter = HEAD_DIM // 4
    lane = lax.broadcasted_iota(I32, t.shape, 1)
    up = pltpu.roll(t, n - quarter, axis=1)
    dn = pltpu.roll(t, quarter, axis=1)
    partner = jnp.where((lane & (2 * quarter - 1)) < quarter, up, dn)
    return t * cos + partner * sin


def _inproj_body(x_ref, sh_ref, sc_ref, nw_ref, w_ref, qnw_ref, knw_ref, cos_ref, sin_ref, bdq_ref, bdk_ref,
                 q_ref, k_ref, v_ref, u_ref, *, att_w, kv_w):
    h = _rms(x_ref[0]) * nw_ref[...]
    h = h * (1.0 + sc_ref[0]) + sh_ref[0]
    proj = _mm(h.astype(BF16), w_ref[...])
    cos, sin = cos_ref[...], sin_ref[...]
    reps = att_w // LANES
    cos_q = jnp.concatenate([cos] * reps, axis=1)
    sin_q = jnp.concatenate([sin] * reps, axis=1)
    q = _head_rms(proj[:, :att_w], bdq_ref[...]) * qnw_ref[...]
    q_ref[0] = (_rope(q, cos_q, sin_q) * (HEAD_DIM ** -0.5 * LOG2E)).astype(BF16)
    k = _head_rms(proj[:, att_w:att_w + kv_w], bdk_ref[...]) * knw_ref[...]
    k_ref[0] = _rope(k, cos, sin).astype(BF16)
    v_ref[0] = proj[:, att_w + kv_w:att_w + 2 * kv_w].astype(BF16)
    hy_off = att_w + 2 * kv_w
    for j in range(u_ref.shape[1]):
        u_ref[0, j] = proj[:, hy_off + LANES * j:hy_off + LANES * (j + 1)]


def _inproj(x, sh, sc, norm_w, w_bf, qnw, knw, att_w, kv_w):
    bsz, seq, d = x.shape
    tm = ROW_TILE
    hy_tiles = (w_bf.shape[1] - att_w - 2 * kv_w) // LANES
    cos, sin = _rope_tables(seq)
    row = lambda b, i: (b, i, 0)
    fixed2 = lambda b, i: (0, 0)
    per_b = lambda b, i: (b, 0, 0)
    return pl.pallas_call(
        functools.partial(_inproj_body, att_w=att_w, kv_w=kv_w),
        out_shape=(SDS((bsz, seq, att_w), BF16), SDS((bsz, seq, kv_w), BF16), SDS((bsz, seq, kv_w), BF16),
                   SDS((bsz, hy_tiles, seq, LANES), F32)),
        grid=(bsz, seq // tm),
        in_specs=[BS((1, tm, d), row), BS((1, 1, d), per_b), BS((1, 1, d), per_b), BS((1, d), fixed2),
                  BS(w_bf.shape, fixed2), BS((1, att_w), fixed2), BS((1, kv_w), fixed2),
                  BS((tm, LANES), lambda b, i: (i, 0)), BS((tm, LANES), lambda b, i: (i, 0)),
                  BS((att_w, att_w), fixed2), BS((kv_w, kv_w), fixed2)],
        out_specs=(BS((1, tm, att_w), row), BS((1, tm, kv_w), row), BS((1, tm, kv_w), row),
                   BS((1, hy_tiles, tm, LANES), lambda b, i: (b, 0, i, 0))),
        compiler_params=_cparams(2), name="in_proj",
    )(x, sh, sc, norm_w.reshape(1, d), w_bf, qnw, knw, jnp.asarray(cos), jnp.asarray(sin),
      jnp.asarray(_block_diag_mean(att_w)), jnp.asarray(_block_diag_mean(kv_w)))


def _ctxproj_body(x_ref, sh_ref, sc_ref, nw_ref, w_ref, knw_ref, bdk_ref, k_ref, v_ref, *, kv_w):
    h = _rms(x_ref[0]) * nw_ref[...]
    h = h * (1.0 + sc_ref[...]) + sh_ref[...]
    proj = _mm(h.astype(BF16), w_ref[...])
    k_ref[0] = (_head_rms(proj[:, :kv_w], bdk_ref[...]) * knw_ref[...]).astype(BF16)
    v_ref[0] = proj[:, kv_w:].astype(BF16)


def _ctxproj(ctx, sh, sc, norm_w, w_kv_bf, knw, kv_w):
    bsz, n_ctx, d = ctx.shape
    fixed2 = lambda b: (0, 0)
    row = lambda b: (b, 0, 0)
    return pl.pallas_call(
        functools.partial(_ctxproj_body, kv_w=kv_w),
        out_shape=(SDS((bsz, n_ctx, kv_w), BF16), SDS((bsz, n_ctx, kv_w), BF16)), grid=(bsz,),
        in_specs=[BS((1, n_ctx, d), row), BS((1, d), fixed2), BS((1, d), fixed2), BS((1, d), fixed2),
                  BS(w_kv_bf.shape, fixed2), BS((1, kv_w), fixed2), BS((kv_w, kv_w), fixed2)],
        out_specs=(BS((1, n_ctx, kv_w), row), BS((1, n_ctx, kv_w), row)),
        compiler_params=_cparams(1), name="ctx_proj",
    )(ctx, sh, sc, norm_w.reshape(1, d), w_kv_bf, knw, jnp.asarray(_block_diag_mean(kv_w)))


def _attn_body(sink_ref, q_ref, kp_ref, kc_ref, kn_ref, vp_ref, vc_ref, vn_ref, kx_ref, vx_ref, gw_ref, o_ref):
    i = pl.program_id(1)
    last = pl.num_programs(1) - 1
    n_ctx = kx_ref.shape[1]
    n_heads = q_ref.shape[2] // HEAD_DIM
    group = n_heads // N_KV_HEADS
    rows = group * Q_BLOCK
    n_band = 3 * Q_BLOCK
    r = lax.broadcasted_iota(I32, (rows, n_band), 0) & (Q_BLOCK - 1)
    j = lax.broadcasted_iota(I32, (rows, n_band), 1)
    in_band = (j >= r) & (j <= r + 2 * WINDOW)
    in_seq = jnp.logical_or(j >= Q_BLOCK, i > 0) & jnp.logical_or(j < 2 * Q_BLOCK, i < last)
    valid = in_band & in_seq
    head_of_row = lax.broadcasted_iota(I32, (rows, 1), 0) // Q_BLOCK
    ones = jnp.ones((n_ctx + n_band, HEAD_DIM), BF16)
    outs = []
    for g in range(N_KV_HEADS):
        sl = slice(g * HEAD_DIM, (g + 1) * HEAD_DIM)
        kb = jnp.concatenate([kx_ref[0][:, sl], kp_ref[0][:, sl], kc_ref[0][:, sl], kn_ref[0][:, sl]], axis=0)
        vb = jnp.concatenate([vx_ref[0][:, sl], vp_ref[0][:, sl], vc_ref[0][:, sl], vn_ref[0][:, sl]], axis=0)
        vb = jnp.concatenate([vb, ones], axis=1)
        heads = range(g * group, (g + 1) * group)
        q4 = jnp.concatenate([q_ref[0][:, h * HEAD_DIM:(h + 1) * HEAD_DIM] for h in heads], axis=0)
        sink = jnp.zeros((rows, 1), F32)
        for hh, h in enumerate(heads):
            sink = jnp.where(head_of_row == hh, sink_ref[h] * LOG2E, sink)
        s = _mm_nt(q4, kb)
        s_ctx = s[:, :n_ctx]
        s_band = jnp.where(valid, s[:, n_ctx:], NEG_INF)
        m = jnp.maximum(jnp.maximum(jnp.max(s_ctx, axis=-1, keepdims=True), jnp.max(s_band, axis=-1, keepdims=True)),
                        sink)
        p_ctx = jnp.exp2((s_ctx - m).astype(BF16))
        p_band = jnp.exp2((s_band - m).astype(BF16))
        o_ext = _mm(p_ctx, vb[:n_ctx]) + _mm(p_band, vb[n_ctx:])
        denom = o_ext[:, HEAD_DIM:HEAD_DIM + 1] + jnp.exp2(sink - m)
        o4 = o_ext[:, :HEAD_DIM] / denom
        for hh in range(group):
            outs.append(o4[hh * Q_BLOCK:(hh + 1) * Q_BLOCK])
    att = jnp.concatenate(outs, axis=1)
    o_ref[0] = (_rms(att) * gw_ref[...]).astype(BF16)


def _attention(sink, q, k, v, kx, vx, gw):
    bsz, seq, att_w = q.shape
    kv_w = k.shape[2]
    n_ctx = kx.shape[1]
    nb = seq // Q_BLOCK
    cur = lambda b, i: (b, i, 0)
    prev = lambda b, i: (b, jnp.maximum(i - 1, 0), 0)
    nxt = lambda b, i: (b, jnp.minimum(i + 1, nb - 1), 0)
    per_b = lambda b, i: (b, 0, 0)
    kvb = (1, Q_BLOCK, kv_w)
    return pl.pallas_call(
        _attn_body, out_shape=SDS((bsz, seq, att_w), BF16), grid=(bsz, nb),
        in_specs=[BS(memory_space=pltpu.SMEM), BS((1, Q_BLOCK, att_w), cur),
                  BS(kvb, prev), BS(kvb, cur), BS(kvb, nxt), BS(kvb, prev), BS(kvb, cur), BS(kvb, nxt),
                  BS((1, n_ctx, kv_w), per_b), BS((1, n_ctx, kv_w), per_b), BS((1, att_w), lambda b, i: (0, 0))],
        out_specs=BS((1, Q_BLOCK, att_w), cur), compiler_params=_cparams(2), name="window_attn",
    )(sink, q, k, k, k, v, v, v, kx, vx, gw)


def _filter_body(f_ref, w1h, w1l, b1, w2h, w2l, b2, w3h, w3l, sf_ref, dl_ref, o_ref, *, hy_w):
    def mm_w(a, wh, wl):
        a_hi, a_lo = _split(a)
        return _mm(a_hi, wh[...]) + _mm(a_hi, wl[...]) + _mm(a_lo, wh[...])

    f = f_ref[...]
    h = jnp.sin(sf_ref[0:1, :] * (mm_w(f, w1h, w1l) + b1[...]))
    h = jnp.sin(sf_ref[1:2, :] * (mm_w(h, w2h, w2l) + b2[...]))
    h = mm_w(h, w3h, w3l)
    h = h * jnp.exp(-f[:, 0:1] * dl_ref[...])
    tl = f.shape[0]
    row = lax.broadcasted_iota(I32, h.shape, 0) + pl.program_id(0) * tl
    col = lax.broadcasted_iota(I32, h.shape, 1)
    is_bwd = ((col // hy_w) & 1) == 1
    h = jnp.where((row == 0) & is_bwd, 0.0, h)
    for j in range(o_ref.shape[0]):
        o_ref[j] = h[:, LANES * j:LANES * (j + 1)]


def _filter_mlp(seq, w1, b1, w2, b2, w3, sin_freq, hy_w):
    hid = w1.shape[1]
    fw = 64
    feat = jnp.asarray(_filter_features(seq, fw))
    w1p = jnp.zeros((fw, hid), F32).at[:w1.shape[0]].set(w1)
    n_out = w3.shape[1]
    delta = jnp.asarray(np.tile(_decay_rates(hy_w), n_out // hy_w)).reshape(1, n_out)
    tl = FILTER_TILE
    fixed = lambda i: (0, 0)
    ops = []
    for w in (w1p, w2, w3):
        ops.extend(_split(w))
    return pl.pallas_call(
        functools.partial(_filter_body, hy_w=hy_w),
        out_shape=SDS((n_out // LANES, seq, LANES), F32), grid=(seq // tl,),
        in_specs=[BS((tl, fw), lambda i: (i, 0)), BS((fw, hid), fixed), BS((fw, hid), fixed), BS((1, hid), fixed),
                  BS((hid, hid), fixed), BS((hid, hid), fixed), BS((1, hid), fixed),
                  BS((hid, n_out), fixed), BS((hid, n_out), fixed), BS((2, hid), fixed), BS((1, n_out), fixed)],
        out_specs=BS((n_out // LANES, tl, LANES), lambda i: (0, i, 0)),
        compiler_params=_cparams(1), name="hyena_filter_mlp",
    )(feat, ops[0], ops[1], b1.reshape(1, hid), ops[2], ops[3], b2.reshape(1, hid), ops[4], ops[5], sin_freq, delta)


C_TILES = 2
C_GROUP = C_TILES * LANES


def _load_cat(ref, row0, n_rows):
    return jnp.concatenate([ref[c, pl.ds(row0, n_rows), :] for c in range(C_TILES)], axis=1)


def _store_cat(ref, row0, n_rows, val):
    for c in range(C_TILES):
        ref[c, pl.ds(row0, n_rows), :] = val[:, c * LANES:(c + 1) * LANES]


def _fwd_major(src, p_ref, ak, passes):
    half = FFT_MAJOR // 2

    def body(g, carry):
        r0 = pl.multiple_of(g * SUBLANES, SUBLANES)
        st = jnp.concatenate([_load_cat(src, n2 * FFT_MINOR + r0, SUBLANES) for n2 in range(half)], axis=0)
        out = _mm_split(ak[0][...], ak[1][...] if passes == 3 else None, st, passes)
        for p in range(FFT_MAJOR):
            _store_cat(p_ref, p * FFT_MINOR + r0, SUBLANES, out[p * SUBLANES:(p + 1) * SUBLANES])
        return carry

    lax.fori_loop(0, FFT_MINOR // SUBLANES, body, 0, unroll=2)


def _inv_major(p_ref, dst, dk, passes):
    half = FFT_MAJOR // 2

    def body(g, carry):
        r0 = pl.multiple_of(g * SUBLANES, SUBLANES)
        st = jnp.concatenate([_load_cat(p_ref, p * FFT_MINOR + r0, SUBLANES) for p in range(FFT_MAJOR)], axis=0)
        out = _mm_split(dk[0][...], dk[1][...] if passes == 3 else None, st, passes)
        for n2 in range(half):
            _store_cat(dst, n2 * FFT_MINOR + r0, SUBLANES, out[n2 * SUBLANES:(n2 + 1) * SUBLANES])
        return carry

    lax.fori_loop(0, FFT_MINOR // SUBLANES, body, 0, unroll=2)


def _slab_spectrum(p_ref, mf, s, passes, real_slot=None):
    m = FFT_MINOR
    if real_slot is None:
        st = jnp.concatenate([_load_cat(p_ref, s * m, m), _load_cat(p_ref, (FFT_MAJOR // 2 + s) * m, m)], axis=0)
        x = _mm_split(mf[0][s], mf[1][s] if passes == 3 else None, st, passes)
    else:
        st = _load_cat(p_ref, real_slot * m, m)
        x = _mm_split(mf[0][s, :, 0:m], mf[1][s, :, 0:m] if passes == 3 else None, st, passes)
    return x[:m], x[m:]


def _spectrum_body(f_ref, b_ref, akh, akl, mfh, mfl, o_ref, p_ref):
    half = FFT_MAJOR // 2
    ak, mf = (akh, akl), (mfh, mfl)
    for is_bwd, src in ((False, f_ref), (True, b_ref)):
        _fwd_major(src, p_ref, ak, 3)

        def put(s, re, im, is_bwd=is_bwd):
            if is_bwd:
                o_ref[0, 0, s, 0] = o_ref[0, 0, s, 0] + re
                o_ref[0, 0, s, 1] = o_ref[0, 0, s, 1] - im
            else:
                o_ref[0, 0, s, 0] = re
                o_ref[0, 0, s, 1] = im

        put(0, *_slab_spectrum(p_ref, mf, 0, 3, real_slot=0))
        put(half, *_slab_spectrum(p_ref, mf, half, 3, real_slot=half))

        def body(s, carry):
            put(s, *_slab_spectrum(p_ref, mf, s, 3))
            return carry

        lax.fori_loop(1, half, body, 0, unroll=2)


def _spectrum(hf, seq, hy_w):
    ak, mf = _dft_hi_lo("ak"), _dft_hi_lo("mf")
    n_grp = hy_w // C_GROUP
    fixed2 = lambda o, c: (0, 0)
    fixed3 = lambda o, c: (0, 0, 0)
    one = pl.Buffered(1)
    return pl.pallas_call(
        _spectrum_body,
        out_shape=SDS((HYENA_ORDER, n_grp, N_SLABS, 2, FFT_MINOR, C_GROUP), F32), grid=(HYENA_ORDER, n_grp),
        in_specs=[BS((C_TILES, seq, LANES), lambda o, c: (o * 2 * n_grp + c, 0, 0), pipeline_mode=one),
                  BS((C_TILES, seq, LANES), lambda o, c: (o * 2 * n_grp + n_grp + c, 0, 0), pipeline_mode=one),
                  BS(ak[0].shape, fixed2, pipeline_mode=one), BS(ak[0].shape, fixed2, pipeline_mode=one),
                  BS(mf[0].shape, fixed3, pipeline_mode=one), BS(mf[0].shape, fixed3, pipeline_mode=one)],
        out_specs=BS((1, 1, N_SLABS, 2, FFT_MINOR, C_GROUP), lambda o, c: (o, c, 0, 0, 0, 0)),
        scratch_shapes=[pltpu.VMEM((C_TILES, FFT_MAJOR * FFT_MINOR, LANES), F32)],
        compiler_params=_cparams(2), name="hyena_spectrum",
    )(hf, hf, ak[0], ak[1], mf[0], mf[1])


CONV_CHUNK = 128
CONV_PASSES = 1


def _sconv_chunk(ref, c, r0, n_rows, seq, prm_ref, base):
    cl = slice(c * LANES, (c + 1) * LANES)
    cur = ref[c, pl.ds(r0, n_rows), :]
    row = lax.broadcasted_iota(I32, cur.shape, 0)
    before = ref[c, pl.ds(jnp.maximum(r0 - 1, 0), 1), :] * (r0 > 0).astype(F32)
    after = ref[c, pl.ds(jnp.minimum(r0 + n_rows, seq - 1), 1), :] * (r0 + n_rows < seq).astype(F32)
    prev = jnp.where(row == 0, before, pltpu.roll(cur, 1, axis=0))
    nxt = jnp.where(row == n_rows - 1, after, pltpu.roll(cur, n_rows - 1, axis=0))
    w = lambda k: prm_ref[0, 0, base + k:base + k + 1, cl]
    return w(3) + w(0) * prev + w(1) * cur + w(2) * nxt


def _conv_body(z_ref, x_ref, prm_ref, akh, dkh, mfh, mih, h_ref, o_ref, zs_ref, p_ref, *, conv_z):
    seq = z_ref.shape[2]
    half = FFT_MAJOR // 2
    m = FFT_MINOR
    zv, xv = z_ref.at[0], x_ref.at[0]
    ak, dk, mf, mi = (akh, None), (dkh, None), (mfh, None), (mih, None)
    n_chunks = seq // CONV_CHUNK

    if conv_z:
        def prep(i, carry):
            r0 = pl.multiple_of(i * CONV_CHUNK, CONV_CHUNK)
            for c in range(C_TILES):
                zs_ref[c, pl.ds(r0, CONV_CHUNK), :] = _sconv_chunk(zv, c, r0, CONV_CHUNK, seq, prm_ref, 0)
            return carry

        lax.fori_loop(0, n_chunks, prep, 0)
        src = zs_ref
    else:
        src = zv
    _fwd_major(src, p_ref, ak, CONV_PASSES)

    def slab(s, real_slot=None):
        xr, xi = _slab_spectrum(p_ref, mf, s, CONV_PASSES, real_slot)
        hr, hi = h_ref[0, 0, s, 0], h_ref[0, 0, s, 1]
        y = jnp.concatenate([xr * hr - xi * hi, xr * hi + xi * hr], axis=0)
        if real_slot is None:
            out = _mm_split(mi[0][s], None, y, CONV_PASSES)
            _store_cat(p_ref, s * m, m, out[:m])
            _store_cat(p_ref, (half + s) * m, m, out[m:])
        else:
            out = _mm_split(mi[0][s, 0:m, :], None, y, CONV_PASSES)
            _store_cat(p_ref, real_slot * m, m, out)

    slab(0, real_slot=0)
    slab(half, real_slot=half)

    def slab_loop(s, carry):
        slab(s)
        return carry

    lax.fori_loop(1, half, slab_loop, 0, unroll=4)
    _inv_major(p_ref, zs_ref, dk, CONV_PASSES)

    def fin(i, carry):
        r0 = pl.multiple_of(i * CONV_CHUNK, CONV_CHUNK)
        for c in range(C_TILES):
            cl = slice(c * LANES, (c + 1) * LANES)
            gate = _sconv_chunk(xv, c, r0, CONV_CHUNK, seq, prm_ref, 4)
            if conv_z:
                zval = _sconv_chunk(zv, c, r0, CONV_CHUNK, seq, prm_ref, 0)
            else:
                zval = zv[c, pl.ds(r0, CONV_CHUNK), :]
            skip = prm_ref[0, 0, 8:9, cl]
            o_ref[0, c, pl.ds(r0, CONV_CHUNK), :] = gate * (zs_ref[c, pl.ds(r0, CONV_CHUNK), :] + skip * zval)
        return carry

    lax.fori_loop(0, n_chunks, fin, 0, unroll=2)


def _hyena_order(z, z_grp0, x, x_grp0, prm, order, spec, conv_z, n_grp):
    bsz, _, seq, _ = z.shape
    ak, dk, mf, mi = (_dft_hi_lo(n)[0] for n in ("ak", "dk", "mf", "mi"))
    one = pl.Buffered(1)
    fixed2 = lambda c, b: (0, 0)
    fixed3 = lambda c, b: (0, 0, 0)
    blk = (1, C_TILES, seq, LANES)
    return pl.pallas_call(
        functools.partial(_conv_body, conv_z=conv_z),
        out_shape=SDS((bsz, n_grp * C_TILES, seq, LANES), F32), grid=(n_grp, bsz),
        in_specs=[BS(blk, lambda c, b: (b, z_grp0 + c, 0, 0)),
                  BS(blk, lambda c, b: (b, x_grp0 + c, 0, 0)),
                  BS((1, 1, 16, C_GROUP), lambda c, b: (order, c, 0, 0)),
                  BS(ak.shape, fixed2, pipeline_mode=one), BS(dk.shape, fixed2, pipeline_mode=one),
                  BS(mf.shape, fixed3, pipeline_mode=one), BS(mi.shape, fixed3, pipeline_mode=one),
                  BS((1, 1, N_SLABS, 2, FFT_MINOR, C_GROUP), lambda c, b: (order, c, 0, 0, 0, 0), pipeline_mode=one)],
        out_specs=BS(blk, lambda c, b: (b, c, 0, 0)),
        scratch_shapes=[pltpu.VMEM((C_TILES, seq, LANES), F32),
                        pltpu.VMEM((C_TILES, FFT_MAJOR * FFT_MINOR, LANES), F32)],
        compiler_params=_cparams(2), name=f"hyena_conv{order}",
    )(z, x, prm, ak, dk, mf, mi, spec)


ROUTER_ROWS = 48
EXPERT_ROW0 = 8


def _outproj_body(att_ref, hy_ref, gwh_ref, wo_ref, x_ref, g1_ref, sh2_ref, sc2_ref, n2w_ref, wrh_ref, wrl_ref,
                  br_ref, tri_ref, x1_ref, h2_ref, ids_ref, gates_ref, cnt_ref, carry_ref):
    @pl.when((pl.program_id(0) == 0) & (pl.program_id(1) == 0))
    def _():
        carry_ref[...] = jnp.zeros_like(carry_ref)

    hy = jnp.concatenate([hy_ref[0, j] for j in range(hy_ref.shape[1])], axis=1)
    hyn = _rms(hy) * gwh_ref[...]
    mix = _mm(jnp.concatenate([att_ref[0], hyn.astype(BF16)], axis=1), wo_ref[...])
    x1 = x_ref[0] + g1_ref[0] * mix
    x1_ref[0] = x1
    h2 = _rms(x1) * n2w_ref[...]
    h2 = h2 * (1.0 + sc2_ref[0]) + sh2_ref[0]
    h2_ref[...] = h2

    h_hi, h_lo = _split(h2)
    lg = _mm_nt(wrh_ref[...], h_hi) + _mm_nt(wrh_ref[...], h_lo) + _mm_nt(wrl_ref[...], h_hi) + br_ref[...]
    tm = lg.shape[1]
    gl = lg[0:N_GROUPS]
    el = lg[EXPERT_ROW0:EXPERT_ROW0 + N_EXPERTS]
    gmax = jnp.max(gl, axis=0, keepdims=True)
    rg = lax.broadcasted_iota(I32, gl.shape, 0).astype(F32)
    g_idx = jnp.min(jnp.where(gl == gmax, rg, float(N_GROUPS)), axis=0, keepdims=True)
    g_val = 1.0 / jnp.sum(jnp.exp(gl - gmax), axis=0, keepdims=True)
    re_i = lax.broadcasted_iota(I32, el.shape, 0)
    re = re_i.astype(F32)
    in_group = (re_i // EXPERTS_PER_GROUP).astype(F32) == g_idx
    elm = jnp.where(in_group, el, NEG_INF)
    m1 = jnp.max(elm, axis=0, keepdims=True)
    i1 = jnp.min(jnp.where(elm == m1, re, float(N_EXPERTS)), axis=0, keepdims=True)
    elm2 = jnp.where(re == i1, NEG_INF, elm)
    m2 = jnp.max(elm2, axis=0, keepdims=True)
    i2 = jnp.min(jnp.where(elm2 == m2, re, float(N_EXPERTS)), axis=0, keepdims=True)
    e2 = jnp.exp(m2 - m1)
    inv = g_val / (1.0 + e2)

    oh1 = (re == i1).astype(F32)
    oh2 = (re == i2).astype(F32)
    oh = oh1 + oh2
    base = _mm(oh.astype(BF16), tri_ref[...]) + carry_ref[:, 0:1]
    rank1 = jnp.sum(oh1 * base, axis=0, keepdims=True)
    rank2 = jnp.sum(oh2 * base, axis=0, keepdims=True)
    carry_ref[...] = carry_ref[...] + jnp.sum(oh, axis=1, keepdims=True)
    cnt_ref[...] = carry_ref[...]

    zi = jnp.zeros((1, tm), I32)
    for k, val in enumerate((i1.astype(I32), i2.astype(I32), rank1.astype(I32), rank2.astype(I32), zi, zi, zi, zi)):
        ids_ref[k:k + 1, :] = val
    zf = jnp.zeros((1, tm), F32)
    for k, val in enumerate((inv, inv * e2, zf, zf, zf, zf, zf, zf)):
        gates_ref[k:k + 1, :] = val


def _outproj_route(att_n, hy, gw_hy, wo_bf, x, g1, sh2, sc2, norm2_w, wr_hi, wr_lo, br):
    bsz, seq, d = x.shape
    tm = ROW_TILE
    n_tok = bsz * seq
    att_w = att_n.shape[2]
    hy_tiles = hy.shape[1]
    tri = jnp.asarray(np.triu(np.ones((tm, tm), np.float32), k=1).astype(ml_dtypes.bfloat16))
    row = lambda b, i: (b, i, 0)
    per_b = lambda b, i: (b, 0, 0)
    fixed2 = lambda b, i: (0, 0)
    tok = lambda b, i: (0, b * (seq // tm) + i)
    return pl.pallas_call(
        _outproj_body,
        out_shape=(SDS((bsz, seq, d), F32), SDS((n_tok, d), F32), SDS((SUBLANES, n_tok), I32),
                   SDS((SUBLANES, n_tok), F32), SDS((N_EXPERTS, LANES), F32)),
        grid=(bsz, seq // tm),
        in_specs=[BS((1, tm, att_w), row), BS((1, hy_tiles, tm, LANES), lambda b, i: (b, 0, i, 0)),
                  BS((1, hy_tiles * LANES), fixed2), BS(wo_bf.shape, fixed2), BS((1, tm, d), row),
                  BS((1, 1, d), per_b), BS((1, 1, d), per_b), BS((1, 1, d), per_b), BS((1, d), fixed2),
                  BS((ROUTER_ROWS, d), fixed2), BS((ROUTER_ROWS, d), fixed2), BS((ROUTER_ROWS, 1), fixed2),
                  BS((tm, tm), fixed2)],
        out_specs=(BS((1, tm, d), row), BS((tm, d), lambda b, i: (b * (seq // tm) + i, 0)),
                   BS((SUBLANES, tm), tok), BS((SUBLANES, tm), tok), BS((N_EXPERTS, LANES), fixed2)),
        scratch_shapes=[pltpu.VMEM((N_EXPERTS, LANES), F32)],
        compiler_params=_cparams(2), name="out_proj_router",
    )(att_n, hy, gw_hy, wo_bf, x, g1, sh2, sc2, norm2_w.reshape(1, d), wr_hi, wr_lo, br, tri)


ISSUE_UNROLL = 8


def _row_copy(src, src_row, dst, dst_row, sem):
    return pltpu.make_async_copy(src.at[pl.ds(src_row, 1), :], dst.at[pl.ds(dst_row, 1), :], sem)


def _dispatch_body(d_ref, h_ref, zero_ref, buf_ref, sem):
    del zero_ref
    n = h_ref.shape[0]

    def issue(t, carry):
        for j in range(TOP_K):
            _row_copy(h_ref, t, buf_ref, d_ref[j, t], sem).start()
        return carry

    lax.fori_loop(0, n, issue, 0, unroll=ISSUE_UNROLL)
    for j in range(TOP_K):
        pltpu.make_async_copy(h_ref, buf_ref.at[pl.ds(0, n), :], sem).wait()


def _dispatch(dest, h2, n_rows):
    n_tok, d = h2.shape
    tm = GATHER_TILE
    return pl.pallas_call(
        _dispatch_body, out_shape=SDS((n_rows, d), F32), grid=(n_tok // tm,),
        in_specs=[BS((TOP_K, tm), lambda i: (0, i), memory_space=pltpu.SMEM), BS((tm, d), lambda i: (i, 0)),
                  BS(memory_space=pl.ANY)],
        out_specs=BS(memory_space=pl.ANY), scratch_shapes=[pltpu.SemaphoreType.DMA(())],
        input_output_aliases={2: 0}, compiler_params=_cparams(1), name="moe_dispatch",
    )(dest, h2, jnp.zeros((n_rows, d), F32))


def _expert_body(be_ref, nu_ref, x_ref, w13_ref, w2_ref, y_ref):
    del be_ref
    i = pl.program_id(0)
    hidden = w2_ref.shape[1]

    @pl.when(i < nu_ref[0])
    def _():
        h = _mm(x_ref[...].astype(BF16), w13_ref[0])
        act = _silu(h[:, :hidden]) * h[:, hidden:]
        y_ref[...] = _mm(act.astype(BF16), w2_ref[0])

    @pl.when(i >= nu_ref[0])
    def _():
        y_ref[...] = jnp.zeros_like(y_ref)


def _experts(block_expert, n_used, buf, w13_bf, w2_bf):
    n_rows, d = buf.shape
    nb = n_rows // MOE_BLOCK
    grid_spec = pltpu.PrefetchScalarGridSpec(
        num_scalar_prefetch=2, grid=(nb,),
        in_specs=[BS((MOE_BLOCK, d), lambda i, be, nu: (i, 0)),
                  BS((1,) + w13_bf.shape[1:], lambda i, be, nu: (be[i], 0, 0)),
                  BS((1,) + w2_bf.shape[1:], lambda i, be, nu: (be[i], 0, 0))],
        out_specs=BS((MOE_BLOCK, d), lambda i, be, nu: (i, 0)))
    return pl.pallas_call(_expert_body, out_shape=SDS((n_rows, d), F32), grid_spec=grid_spec,
                          compiler_params=_cparams(1), name="moe_experts")(block_expert, n_used, buf, w13_bf, w2_bf)


def _combine_body(d_ref, y_ref, x1_ref, g2_ref, gt_ref, o_ref, rows_ref, sem):
    n = x1_ref.shape[0]

    def issue(t, carry):
        for j in range(TOP_K):
            _row_copy(y_ref, d_ref[j, t], rows_ref.at[j], t, sem).start()
        return carry

    lax.fori_loop(0, n, issue, 0, unroll=ISSUE_UNROLL)
    for j in range(TOP_K):
        pltpu.make_async_copy(y_ref.at[pl.ds(0, n), :], rows_ref.at[j], sem).wait()
    gt = gt_ref[...]
    moe = gt[:, 0:1] * rows_ref[0] + gt[:, 1:2] * rows_ref[1]
    o_ref[...] = x1_ref[...] + g2_ref[0] * moe


def _combine(dest, y, x1, g2, gates_t, seq):
    n_tok, d = x1.shape
    tm = GATHER_TILE
    return pl.pallas_call(
        _combine_body, out_shape=SDS((n_tok, d), F32), grid=(n_tok // tm,),
        in_specs=[BS((TOP_K, tm), lambda i: (0, i), memory_space=pltpu.SMEM), BS(memory_space=pl.ANY),
                  BS((tm, d), lambda i: (i, 0)), BS((1, 1, d), lambda i: (i // (seq // tm), 0, 0)),
                  BS((tm, TOP_K), lambda i: (i, 0))],
        out_specs=BS((tm, d), lambda i: (i, 0)),
        scratch_shapes=[pltpu.VMEM((TOP_K, tm, d), F32), pltpu.SemaphoreType.DMA(())],
        compiler_params=_cparams(1), name="moe_combine",
    )(dest, y, x1, g2, gates_t)


def kernel(x, c, ctx, c_ctx, ada_w, ada_b, norm1_w, w_in, q_norm_w, k_norm_w, attn_sink, hy_conv_w, hy_conv_b, hy_w1,
           hy_b1, hy_w2, hy_b2, hy_w3, hy_sin_freq, hy_skip, group_norm_w, w_out, norm2_w, router_g_w, router_g_b,
           router_e_w, router_e_b, exp_w1, exp_w3, exp_w2):
    bsz, seq, d = x.shape
    assert ada_w.shape[0] == 1, "single-layer block"
    att_w = d // 2
    hy_w = d - att_w
    kv_w = N_KV_HEADS * HEAD_DIM
    n_heads = att_w // HEAD_DIM
    n_grp = hy_w // C_GROUP
    assert 2 * seq == FFT_MAJOR * FFT_MINOR and seq % ROW_TILE == 0 and hy_w % C_GROUP == 0
    assert w_in.shape[2] == att_w + 2 * kv_w + (HYENA_ORDER + 1) * hy_w

    pad = (-(bsz + 1)) % SUBLANES
    cc = jnp.concatenate([c, c_ctx[None, :], jnp.zeros((pad, d), F32)], axis=0)
    mod = _ada(cc, ada_w[0], ada_b[0])
    lat = [mod[:bsz, k * d:(k + 1) * d].reshape(bsz, 1, d) for k in range(6)]
    sh1, sc1, g1, sh2, sc2, g2 = lat
    csh1 = mod[bsz:bsz + 1, 0:d]
    csc1 = mod[bsz:bsz + 1, d:2 * d]

    w_in_bf = w_in[0].astype(BF16)
    qnw = jnp.tile(q_norm_w[0], n_heads).reshape(1, att_w)
    knw = jnp.tile(k_norm_w[0], N_KV_HEADS).reshape(1, kv_w)
    q, k, v, u = _inproj(x, sh1, sc1, norm1_w[0], w_in_bf, qnw, knw, att_w, kv_w)
    kx, vx = _ctxproj(ctx, csh1, csc1, norm1_w[0], w_in_bf[:, att_w:att_w + 2 * kv_w], knw, kv_w)

    gw = group_norm_w[0]
    att_n = _attention(attn_sink[0], q, k, v, kx, vx, gw[:att_w].reshape(1, att_w))

    hf = _filter_mlp(seq, hy_w1[0], hy_b1[0], hy_w2[0], hy_b2[0], hy_w3[0], hy_sin_freq[0], hy_w)
    spec = _spectrum(hf, seq, hy_w)
    cw = hy_conv_w[0].reshape(3, HYENA_ORDER + 1, n_grp, C_GROUP)
    cb = hy_conv_b[0].reshape(1, HYENA_ORDER + 1, n_grp, C_GROUP)
    taps = jnp.concatenate([cw, cb], axis=0)
    prm = jnp.zeros((HYENA_ORDER, n_grp, 16, C_GROUP), F32)
    for o in range(HYENA_ORDER):
        prm = prm.at[o, :, 4:8].set(jnp.transpose(taps[:, o + 1], (1, 0, 2)))
        prm = prm.at[o, :, 8].set(hy_skip[0, o].reshape(n_grp, C_GROUP))
    prm = prm.at[0, :, 0:4].set(jnp.transpose(taps[:, 0], (1, 0, 2)))
    z = _hyena_order(u, 0, u, n_grp, prm, 0, spec, True, n_grp)
    hy = _hyena_order(z, 0, u, 2 * n_grp, prm, 1, spec, False, n_grp)

    wr = jnp.zeros((ROUTER_ROWS, d), F32)
    wr = wr.at[0:N_GROUPS].set(router_g_w[0].T).at[EXPERT_ROW0:EXPERT_ROW0 + N_EXPERTS].set(router_e_w[0].T)
    br = jnp.zeros((ROUTER_ROWS, 1), F32)
    br = br.at[0:N_GROUPS, 0].set(router_g_b[0]).at[EXPERT_ROW0:EXPERT_ROW0 + N_EXPERTS, 0].set(router_e_b[0])
    wr_hi, wr_lo = _split(wr)
    x1, h2, ids, gates, counts = _outproj_route(att_n, hy, gw[att_w:].reshape(1, hy_w), w_out[0].astype(BF16), x,
                                                g1, sh2, sc2, norm2_w[0], wr_hi, wr_lo, br)

    n_tok = bsz * seq
    cnt = counts[:, 0].astype(I32)
    padded = (cnt + MOE_BLOCK - 1) // MOE_BLOCK * MOE_BLOCK
    pend = jnp.cumsum(padded)
    pstart = pend - padded
    e_iota = jnp.arange(N_EXPERTS, dtype=I32)
    is_e = ids[0:TOP_K, :, None] == e_iota
    dest = jnp.sum(jnp.where(is_e, pstart, 0), axis=-1) + ids[TOP_K:2 * TOP_K]
    n_blocks = (n_tok * TOP_K) // MOE_BLOCK + N_EXPERTS
    block_row0 = jnp.arange(n_blocks, dtype=I32) * MOE_BLOCK
    block_expert = jnp.minimum(jnp.sum((block_row0[:, None] >= pend[None, :]).astype(I32), axis=-1), N_EXPERTS - 1)
    n_used = (pend[-1:] // MOE_BLOCK).astype(I32)

    buf = _dispatch(dest, h2, n_blocks * MOE_BLOCK)
    w13 = jnp.concatenate([exp_w1[0], exp_w3[0]], axis=2).astype(BF16)
    y = _experts(block_expert, n_used, buf, w13, exp_w2[0].astype(BF16))
    out = _combine(dest, y, x1.reshape(n_tok, d), g2, gates[0:TOP_K].T, seq)
    return out.reshape(bsz, seq, d)
```

```python
import functools
import math

import ml_dtypes
import numpy as np

import jax
import jax.numpy as jnp
from jax import lax
from jax.experimental import pallas as pl
from jax.experimental.pallas import tpu as pltpu

F32, BF16, I32 = jnp.float32, jnp.bfloat16, jnp.int32
SDS = jax.ShapeDtypeStruct
BS = pl.BlockSpec

HEAD_DIM = 64
N_KV_HEADS = 2
WINDOW = 128
Q_BLOCK = 128
GRID_W = 64
ROPE_THETA = 10000.0
EPS = 1e-6
NEG_INF = -1e30
LOG2E = math.log2(math.e)
HYENA_BANDS = 16
HYENA_ORDER = 2
DECAY_TARGET = 1e-2
FAST_DECAY_PCT = 0.3
SLOW_DECAY_PCT = 1.5
N_GROUPS = 4
EXPERTS_PER_GROUP = 8
N_EXPERTS = N_GROUPS * EXPERTS_PER_GROUP
TOP_K = 2

LANES = 128
SUBLANES = 8
VMEM_LIMIT = 60 * 1024 * 1024

ROW_TILE = 512
MOE_BLOCK = 512
GATHER_TILE = 1024
FILTER_TILE = 512

FFT_MINOR = 128
FFT_MAJOR = 64
N_SLABS = FFT_MAJOR // 2 + 1


def _cparams(n_grid):
    return pltpu.CompilerParams(dimension_semantics=("arbitrary",) * n_grid, vmem_limit_bytes=VMEM_LIMIT)


def _mm(a, b):
    return jnp.dot(a, b, preferred_element_type=F32)


def _mm_nt(a, b):
    return lax.dot_general(a, b, (((1,), (1,)), ((), ())), preferred_element_type=F32)


def _split(x):
    hi = x.astype(BF16)
    lo = (x - hi.astype(F32)).astype(BF16)
    return hi, lo


def _mm_split(w_hi, w_lo, x, passes):
    if passes == 1:
        return _mm(w_hi, x.astype(BF16))
    x_hi, x_lo = _split(x)
    out = _mm(w_hi, x_hi) + _mm(w_hi, x_lo)
    if passes == 3:
        out = out + _mm(w_lo, x_hi)
    return out


def _mm3(a, b):
    a_hi, a_lo = _split(a)
    b_hi, b_lo = _split(b)
    return _mm(a_hi, b_hi) + _mm(a_hi, b_lo) + _mm(a_lo, b_hi)


def _silu(x):
    return x * (1.0 / (1.0 + jnp.exp(-x)))


def _rms(x):
    return x * lax.rsqrt(jnp.mean(x * x, axis=-1, keepdims=True) + EPS)


@functools.lru_cache(maxsize=None)
def _rope_tables(seq_len):
    pos = np.arange(seq_len)
    rows = (pos // GRID_W).astype(np.float64)
    cols = (pos % GRID_W).astype(np.float64)
    pairs = HEAD_DIM // 4
    inv_freq = ROPE_THETA ** (-np.arange(pairs, dtype=np.float64) / pairs)
    ang_r = rows[:, None] * inv_freq[None, :]
    ang_c = cols[:, None] * inv_freq[None, :]
    ang = np.concatenate([ang_r, ang_r, ang_c, ang_c], axis=1)
    sign = np.tile(np.concatenate([-np.ones(pairs), np.ones(pairs)]), 2)
    cos = np.cos(ang)
    sin = np.sin(ang) * sign[None, :]
    reps = LANES // HEAD_DIM
    return np.tile(cos, (1, reps)).astype(np.float32), np.tile(sin, (1, reps)).astype(np.float32)


@functools.lru_cache(maxsize=None)
def _block_diag_mean(width):
    h = np.arange(width) // HEAD_DIM
    return ((h[:, None] == h[None, :]).astype(np.float32) / HEAD_DIM).astype(ml_dtypes.bfloat16)


@functools.lru_cache(maxsize=None)
def _filter_features(seq_len, width):
    pos = np.arange(seq_len, dtype=np.float64)
    t = pos / seq_len
    bands = np.linspace(1e-4, HYENA_BANDS - 1, HYENA_BANDS)
    ang = (2.0 * math.pi / seq_len) * pos[:, None] * bands[None, :]
    feat = np.concatenate([t[:, None], np.cos(ang), -np.sin(ang)], axis=-1)
    out = np.zeros((seq_len, width), np.float32)
    out[:, :feat.shape[1]] = feat
    return out


@functools.lru_cache(maxsize=None)
def _decay_rates(width):
    max_decay = math.log(DECAY_TARGET) / FAST_DECAY_PCT
    min_decay = math.log(DECAY_TARGET) / SLOW_DECAY_PCT
    return np.abs(np.linspace(min_decay, max_decay, width)).astype(np.float32)


@functools.lru_cache(maxsize=None)
def _dft_tables():
    n = FFT_MAJOR * FFT_MINOR
    half = FFT_MAJOR // 2
    n2 = np.arange(half)
    a = np.zeros((FFT_MAJOR, half))
    a[:half] = np.cos(2 * np.pi * np.arange(half)[:, None] * n2[None, :] / FFT_MAJOR)
    a[half] = (-1.0) ** n2
    k2 = np.arange(1, half)
    a[half + 1:] = -np.sin(2 * np.pi * k2[:, None] * n2[None, :] / FFT_MAJOR)
    d = np.zeros((half, FFT_MAJOR))
    d[:, 0] = 1.0 / n
    d[:, 1:half] = 2 * np.cos(2 * np.pi * n2[:, None] * k2[None, :] / FFT_MAJOR) / n
    d[:, half] = (-1.0) ** n2 / n
    d[:, half + 1:] = -2 * np.sin(2 * np.pi * n2[:, None] * k2[None, :] / FFT_MAJOR) / n
    eye = np.eye(SUBLANES)
    ak = np.kron(a, eye)
    dk = np.kron(d, eye)
    s = np.arange(N_SLABS)[:, None, None]
    k1 = np.arange(FFT_MINOR)[None, :, None]
    n1 = np.arange(FFT_MINOR)[None, None, :]
    g = np.exp(-2j * np.pi * n1 * (FFT_MAJOR * k1 + s) / n)
    mf = np.block([[g.real, -g.imag], [g.imag, g.real]])
    gi = np.conj(g).transpose(0, 2, 1)
    mi = np.block([[gi.real, -gi.imag], [gi.imag, gi.real]])
    return {name: m.astype(np.float32) for name, m in (("ak", ak), ("dk", dk), ("mf", mf), ("mi", mi))}


def _dft_hi_lo(name):
    return _split(jnp.asarray(_dft_tables()[name]))


def _ada_body(c_ref, w_ref, b_ref, o_ref):
    o_ref[...] = _mm3(_silu(c_ref[...]), w_ref[...]) + b_ref[...]


def _ada(cc, ada_w, ada_b):
    rows, d = cc.shape
    n = ada_w.shape[1]
    tn = 1536
    return pl.pallas_call(
        _ada_body, out_shape=SDS((rows, n), F32), grid=(n // tn,),
        in_specs=[BS((rows, d), lambda j: (0, 0)), BS((d, tn), lambda j: (0, j)), BS((1, tn), lambda j: (0, j))],
        out_specs=BS((rows, tn), lambda j: (0, j)), compiler_params=_cparams(1), name="ada_mod",
    )(cc, ada_w, ada_b.reshape(1, n))


def _head_rms(t, bd):
    hi, lo = _split(t * t)
    return t * lax.rsqrt(_mm(hi, bd) + _mm(lo, bd) + EPS)


def _rope(t, cos, sin):
    n = t.shape[1]
    quarter = HEAD_DIM // 4
    lane = lax.broadcasted_iota(I32, t.shape, 1)
    up = pltpu.roll(t, n - quarter, axis=1)
    dn = pltpu.roll(t, quarter, axis=1)
    partner = jnp.where((lane & (2 * quarter - 1)) < quarter, up, dn)
    return t * cos + partner * sin


def _inproj_body(x_ref, sh_ref, sc_ref, nw_ref, w_ref, qnw_ref, knw_ref, cos_ref, sin_ref, bdq_ref, bdk_ref,
                 q_ref, k_ref, v_ref, u_ref, *, att_w, kv_w):
    h = _rms(x_ref[0]) * nw_ref[...]
    h = h * (1.0 + sc_ref[0]) + sh_ref[0]
    proj = _mm(h.astype(BF16), w_ref[...])
    cos, sin = cos_ref[...], sin_ref[...]
    reps = att_w // LANES
    cos_q = jnp.concatenate([cos] * reps, axis=1)
    sin_q = jnp.concatenate([sin] * reps, axis=1)
    q = _head_rms(proj[:, :att_w], bdq_ref[...]) * qnw_ref[...]
    q_ref[0] = (_rope(q, cos_q, sin_q) * (HEAD_DIM ** -0.5 * LOG2E)).astype(BF16)
    k = _head_rms(proj[:, att_w:att_w + kv_w], bdk_ref[...]) * knw_ref[...]
    k_ref[0] = _rope(k, cos, sin).astype(BF16)
    v_ref[0] = proj[:, att_w + kv_w:att_w + 2 * kv_w].astype(BF16)
    hy_off = att_w + 2 * kv_w
    for j in range(u_ref.shape[1]):
        u_ref[0, j] = proj[:, hy_off + LANES * j:hy_off + LANES * (j + 1)]


def _inproj(x, sh, sc, norm_w, w_bf, qnw, knw, att_w, kv_w):
    bsz, seq, d = x.shape
    tm = ROW_TILE
    hy_tiles = (w_bf.shape[1] - att_w - 2 * kv_w) // LANES
    cos, sin = _rope_tables(seq)
    row = lambda b, i: (b, i, 0)
    fixed2 = lambda b, i: (0, 0)
    per_b = lambda b, i: (b, 0, 0)
    return pl.pallas_call(
        functools.partial(_inproj_body, att_w=att_w, kv_w=kv_w),
        out_shape=(SDS((bsz, seq, att_w), BF16), SDS((bsz, seq, kv_w), BF16), SDS((bsz, seq, kv_w), BF16),
                   SDS((bsz, hy_tiles, seq, LANES), F32)),
        grid=(bsz, seq // tm),
        in_specs=[BS((1, tm, d), row), BS((1, 1, d), per_b), BS((1, 1, d), per_b), BS((1, d), fixed2),
                  BS(w_bf.shape, fixed2), BS((1, att_w), fixed2), BS((1, kv_w), fixed2),
                  BS((tm, LANES), lambda b, i: (i, 0)), BS((tm, LANES), lambda b, i: (i, 0)),
                  BS((att_w, att_w), fixed2), BS((kv_w, kv_w), fixed2)],
        out_specs=(BS((1, tm, att_w), row), BS((1, tm, kv_w), row), BS((1, tm, kv_w), row),
                   BS((1, hy_tiles, tm, LANES), lambda b, i: (b, 0, i, 0))),
        compiler_params=_cparams(2), name="in_proj",
    )(x, sh, sc, norm_w.reshape(1, d), w_bf, qnw, knw, jnp.asarray(cos), jnp.asarray(sin),
      jnp.asarray(_block_diag_mean(att_w)), jnp.asarray(_block_diag_mean(kv_w)))


def _ctxproj_body(x_ref, sh_ref, sc_ref, nw_ref, w_ref, knw_ref, bdk_ref, k_ref, v_ref, *, kv_w):
    h = _rms(x_ref[0]) * nw_ref[...]
    h = h * (1.0 + sc_ref[...]) + sh_ref[...]
    proj = _mm(h.astype(BF16), w_ref[...])
    k_ref[0] = (_head_rms(proj[:, :kv_w], bdk_ref[...]) * knw_ref[...]).astype(BF16)
    v_ref[0] = proj[:, kv_w:].astype(BF16)


def _ctxproj(ctx, sh, sc, norm_w, w_kv_bf, knw, kv_w):
    bsz, n_ctx, d = ctx.shape
    fixed2 = lambda b: (0, 0)
    row = lambda b: (b, 0, 0)
    return pl.pallas_call(
        functools.partial(_ctxproj_body, kv_w=kv_w),
        out_shape=(SDS((bsz, n_ctx, kv_w), BF16), SDS((bsz, n_ctx, kv_w), BF16)), grid=(bsz,),
        in_specs=[BS((1, n_ctx, d), row), BS((1, d), fixed2), BS((1, d), fixed2), BS((1, d), fixed2),
                  BS(w_kv_bf.shape, fixed2), BS((1, kv_w), fixed2), BS((kv_w, kv_w), fixed2)],
        out_specs=(BS((1, n_ctx, kv_w), row), BS((1, n_ctx, kv_w), row)),
        compiler_params=_cparams(1), name="ctx_proj",
    )(ctx, sh, sc, norm_w.reshape(1, d), w_kv_bf, knw, jnp.asarray(_block_diag_mean(kv_w)))


def _attn_body(sink_ref, q_ref, kp_ref, kc_ref, kn_ref, vp_ref, vc_ref, vn_ref, kx_ref, vx_ref, gw_ref, o_ref):
    i = pl.program_id(1)
    last = pl.num_programs(1) - 1
    n_ctx = kx_ref.shape[1]
    n_heads = q_ref.shape[2] // HEAD_DIM
    group = n_heads // N_KV_HEADS
    rows = group * Q_BLOCK
    n_band = 3 * Q_BLOCK
    r = lax.broadcasted_iota(I32, (rows, n_band), 0) & (Q_BLOCK - 1)
    j = lax.broadcasted_iota(I32, (rows, n_band), 1)
    in_band = (j >= r) & (j <= r + 2 * WINDOW)
    in_seq = jnp.logical_or(j >= Q_BLOCK, i > 0) & jnp.logical_or(j < 2 * Q_BLOCK, i < last)
    valid = in_band & in_seq
    head_of_row = lax.broadcasted_iota(I32, (rows, 1), 0) // Q_BLOCK
    outs = []
    for g in range(N_KV_HEADS):
        sl = slice(g * HEAD_DIM, (g + 1) * HEAD_DIM)
        kb = jnp.concatenate([kx_ref[0][:, sl], kp_ref[0][:, sl], kc_ref[0][:, sl], kn_ref[0][:, sl]], axis=0)
        vb = jnp.concatenate([vx_ref[0][:, sl], vp_ref[0][:, sl], vc_ref[0][:, sl], vn_ref[0][:, sl]], axis=0)
        heads = range(g * group, (g + 1) * group)
        q4 = jnp.concatenate([q_ref[0][:, h * HEAD_DIM:(h + 1) * HEAD_DIM] for h in heads], axis=0)
        sink = jnp.zeros((rows, 1), F32)
        for hh, h in enumerate(heads):
            sink = jnp.where(head_of_row == hh, sink_ref[h] * LOG2E, sink)
        s = _mm_nt(q4, kb)
        s_ctx = s[:, :n_ctx]
        s_band = jnp.where(valid, s[:, n_ctx:], NEG_INF)
        m = jnp.maximum(jnp.maximum(jnp.max(s_ctx, axis=-1, keepdims=True), jnp.max(s_band, axis=-1, keepdims=True)),
                        sink)
        p_ctx = jnp.exp2((s_ctx - m).astype(BF16))
        p_band = jnp.exp2((s_band - m).astype(BF16))
        denom = (jnp.sum(p_ctx.astype(F32), axis=-1, keepdims=True) + jnp.sum(p_band.astype(F32), axis=-1, keepdims=True)
                 + jnp.exp2(sink - m))
        o4 = (_mm(p_ctx, vb[:n_ctx]) + _mm(p_band, vb[n_ctx:])) / denom
        for hh in range(group):
            outs.append(o4[hh * Q_BLOCK:(hh + 1) * Q_BLOCK])
    att = jnp.concatenate(outs, axis=1)
    o_ref[0] = (_rms(att) * gw_ref[...]).astype(BF16)


def _attention(sink, q, k, v, kx, vx, gw):
    bsz, seq, att_w = q.shape
    kv_w = k.shape[2]
    n_ctx = kx.shape[1]
    nb = seq // Q_BLOCK
    cur = lambda b, i: (b, i, 0)
    prev = lambda b, i: (b, jnp.maximum(i - 1, 0), 0)
    nxt = lambda b, i: (b, jnp.minimum(i + 1, nb - 1), 0)
    per_b = lambda b, i: (b, 0, 0)
    kvb = (1, Q_BLOCK, kv_w)
    return pl.pallas_call(
        _attn_body, out_shape=SDS((bsz, seq, att_w), BF16), grid=(bsz, nb),
        in_specs=[BS(memory_space=pltpu.SMEM), BS((1, Q_BLOCK, att_w), cur),
                  BS(kvb, prev), BS(kvb, cur), BS(kvb, nxt), BS(kvb, prev), BS(kvb, cur), BS(kvb, nxt),
                  BS((1, n_ctx, kv_w), per_b), BS((1, n_ctx, kv_w), per_b), BS((1, att_w), lambda b, i: (0, 0))],
        out_specs=BS((1, Q_BLOCK, att_w), cur), compiler_params=_cparams(2), name="window_attn",
    )(sink, q, k, k, k, v, v, v, kx, vx, gw)


def _filter_body(f_ref, w1h, w1l, b1, w2h, w2l, b2, w3h, w3l, sf_ref, dl_ref, o_ref, *, hy_w):
    def mm_w(a, wh, wl):
        a_hi, a_lo = _split(a)
        return _mm(a_hi, wh[...]) + _mm(a_hi, wl[...]) + _mm(a_lo, wh[...])

    f = f_ref[...]
    h = jnp.sin(sf_ref[0:1, :] * (mm_w(f, w1h, w1l) + b1[...]))
    h = jnp.sin(sf_ref[1:2, :] * (mm_w(h, w2h, w2l) + b2[...]))
    h = mm_w(h, w3h, w3l)
    h = h * jnp.exp(-f[:, 0:1] * dl_ref[...])
    tl = f.shape[0]
    row = lax.broadcasted_iota(I32, h.shape, 0) + pl.program_id(0) * tl
    col = lax.broadcasted_iota(I32, h.shape, 1)
    is_bwd = ((col // hy_w) & 1) == 1
    h = jnp.where((row == 0) & is_bwd, 0.0, h)
    for j in range(o_ref.shape[0]):
        o_ref[j] = h[:, LANES * j:LANES * (j + 1)]


def _filter_mlp(seq, w1, b1, w2, b2, w3, sin_freq, hy_w):
    hid = w1.shape[1]
    fw = 64
    feat = jnp.asarray(_filter_features(seq, fw))
    w1p = jnp.zeros((fw, hid), F32).at[:w1.shape[0]].set(w1)
    n_out = w3.shape[1]
    delta = jnp.asarray(np.tile(_decay_rates(hy_w), n_out // hy_w)).reshape(1, n_out)
    tl = FILTER_TILE
    fixed = lambda i: (0, 0)
    ops = []
    for w in (w1p, w2, w3):
        ops.extend(_split(w))
    return pl.pallas_call(
        functools.partial(_filter_body, hy_w=hy_w),
        out_shape=SDS((n_out // LANES, seq, LANES), F32), grid=(seq // tl,),
        in_specs=[BS((tl, fw), lambda i: (i, 0)), BS((fw, hid), fixed), BS((fw, hid), fixed), BS((1, hid), fixed),
                  BS((hid, hid), fixed), BS((hid, hid), fixed), BS((1, hid), fixed),
                  BS((hid, n_out), fixed), BS((hid, n_out), fixed), BS((2, hid), fixed), BS((1, n_out), fixed)],
        out_specs=BS((n_out // LANES, tl, LANES), lambda i: (0, i, 0)),
        compiler_params=_cparams(1), name="hyena_filter_mlp",
    )(feat, ops[0], ops[1], b1.reshape(1, hid), ops[2], ops[3], b2.reshape(1, hid), ops[4], ops[5], sin_freq, delta)


C_TILES = 2
C_GROUP = C_TILES * LANES


def _load_cat(ref, row0, n_rows):
    return jnp.concatenate([ref[c, pl.ds(row0, n_rows), :] for c in range(C_TILES)], axis=1)


def _store_cat(ref, row0, n_rows, val):
    for c in range(C_TILES):
        ref[c, pl.ds(row0, n_rows), :] = val[:, c * LANES:(c + 1) * LANES]


def _fwd_major(src, p_ref, ak, passes):
    half = FFT_MAJOR // 2

    def body(g, carry):
        r0 = pl.multiple_of(g * SUBLANES, SUBLANES)
        st = jnp.concatenate([_load_cat(src, n2 * FFT_MINOR + r0, SUBLANES) for n2 in range(half)], axis=0)
        out = _mm_split(ak[0][...], ak[1][...] if passes == 3 else None, st, passes)
        for p in range(FFT_MAJOR):
            _store_cat(p_ref, p * FFT_MINOR + r0, SUBLANES, out[p * SUBLANES:(p + 1) * SUBLANES])
        return carry

    lax.fori_loop(0, FFT_MINOR // SUBLANES, body, 0, unroll=4)


def _inv_major(p_ref, dst, dk, passes):
    half = FFT_MAJOR // 2

    def body(g, carry):
        r0 = pl.multiple_of(g * SUBLANES, SUBLANES)
        st = jnp.concatenate([_load_cat(p_ref, p * FFT_MINOR + r0, SUBLANES) for p in range(FFT_MAJOR)], axis=0)
        out = _mm_split(dk[0][...], dk[1][...] if passes == 3 else None, st, passes)
        for n2 in range(half):
            _store_cat(dst, n2 * FFT_MINOR + r0, SUBLANES, out[n2 * SUBLANES:(n2 + 1) * SUBLANES])
        return carry

    lax.fori_loop(0, FFT_MINOR // SUBLANES, body, 0, unroll=4)


def _slab_spectrum(p_ref, mf, s, passes, real_slot=None):
    m = FFT_MINOR
    if real_slot is None:
        st = jnp.concatenate([_load_cat(p_ref, s * m, m), _load_cat(p_ref, (FFT_MAJOR // 2 + s) * m, m)], axis=0)
        x = _mm_split(mf[0][s], mf[1][s] if passes == 3 else None, st, passes)
    else:
        st = _load_cat(p_ref, real_slot * m, m)
        x = _mm_split(mf[0][s, :, 0:m], mf[1][s, :, 0:m] if passes == 3 else None, st, passes)
    return x[:m], x[m:]


def _spectrum_body(f_ref, b_ref, akh, akl, mfh, mfl, o_ref, p_ref):
    half = FFT_MAJOR // 2
    ak, mf = (akh, akl), (mfh, mfl)
    for is_bwd, src in ((False, f_ref), (True, b_ref)):
        _fwd_major(src, p_ref, ak, 3)

        def put(s, re, im, is_bwd=is_bwd):
            if is_bwd:
                o_ref[0, 0, s, 0] = o_ref[0, 0, s, 0] + re
                o_ref[0, 0, s, 1] = o_ref[0, 0, s, 1] - im
            else:
                o_ref[0, 0, s, 0] = re
                o_ref[0, 0, s, 1] = im

        put(0, *_slab_spectrum(p_ref, mf, 0, 3, real_slot=0))
        put(half, *_slab_spectrum(p_ref, mf, half, 3, real_slot=half))

        def body(s, carry):
            put(s, *_slab_spectrum(p_ref, mf, s, 3))
            return carry

        lax.fori_loop(1, half, body, 0, unroll=2)


def _spectrum(hf, seq, hy_w):
    ak, mf = _dft_hi_lo("ak"), _dft_hi_lo("mf")
    n_grp = hy_w // C_GROUP
    fixed2 = lambda o, c: (0, 0)
    fixed3 = lambda o, c: (0, 0, 0)
    one = pl.Buffered(1)
    return pl.pallas_call(
        _spectrum_body,
        out_shape=SDS((HYENA_ORDER, n_grp, N_SLABS, 2, FFT_MINOR, C_GROUP), F32), grid=(HYENA_ORDER, n_grp),
        in_specs=[BS((C_TILES, seq, LANES), lambda o, c: (o * 2 * n_grp + c, 0, 0), pipeline_mode=one),
                  BS((C_TILES, seq, LANES), lambda o, c: (o * 2 * n_grp + n_grp + c, 0, 0), pipeline_mode=one),
                  BS(ak[0].shape, fixed2, pipeline_mode=one), BS(ak[0].shape, fixed2, pipeline_mode=one),
                  BS(mf[0].shape, fixed3, pipeline_mode=one), BS(mf[0].shape, fixed3, pipeline_mode=one)],
        out_specs=BS((1, 1, N_SLABS, 2, FFT_MINOR, C_GROUP), lambda o, c: (o, c, 0, 0, 0, 0)),
        scratch_shapes=[pltpu.VMEM((C_TILES, FFT_MAJOR * FFT_MINOR, LANES), F32)],
        compiler_params=_cparams(2), name="hyena_spectrum",
    )(hf, hf, ak[0], ak[1], mf[0], mf[1])


CONV_CHUNK = 128
CONV_PASSES = 1


def _sconv_chunk(ref, c, r0, n_rows, seq, prm_ref, base):
    cl = slice(c * LANES, (c + 1) * LANES)
    cur = ref[c, pl.ds(r0, n_rows), :]
    row = lax.broadcasted_iota(I32, cur.shape, 0)
    before = ref[c, pl.ds(jnp.maximum(r0 - 1, 0), 1), :] * (r0 > 0).astype(F32)
    after = ref[c, pl.ds(jnp.minimum(r0 + n_rows, seq - 1), 1), :] * (r0 + n_rows < seq).astype(F32)
    prev = jnp.where(row == 0, before, pltpu.roll(cur, 1, axis=0))
    nxt = jnp.where(row == n_rows - 1, after, pltpu.roll(cur, n_rows - 1, axis=0))
    w = lambda k: prm_ref[0, 0, base + k:base + k + 1, cl]
    return w(3) + w(0) * prev + w(1) * cur + w(2) * nxt


def _conv_body(z_ref, x_ref, prm_ref, akh, dkh, mfh, mih, h_ref, o_ref, zs_ref, p_ref, *, conv_z):
    seq = z_ref.shape[2]
    half = FFT_MAJOR // 2
    m = FFT_MINOR
    zv, xv = z_ref.at[0], x_ref.at[0]
    ak, dk, mf, mi = (akh, None), (dkh, None), (mfh, None), (mih, None)
    n_chunks = seq // CONV_CHUNK

    if conv_z:
        def prep(i, carry):
            r0 = pl.multiple_of(i * CONV_CHUNK, CONV_CHUNK)
            for c in range(C_TILES):
                zs_ref[c, pl.ds(r0, CONV_CHUNK), :] = _sconv_chunk(zv, c, r0, CONV_CHUNK, seq, prm_ref, 0)
            return carry

        lax.fori_loop(0, n_chunks, prep, 0)
        src = zs_ref
    else:
        src = zv
    _fwd_major(src, p_ref, ak, CONV_PASSES)

    def slab(s, real_slot=None):
        xr, xi = _slab_spectrum(p_ref, mf, s, CONV_PASSES, real_slot)
        hr, hi = h_ref[0, 0, s, 0], h_ref[0, 0, s, 1]
        y = jnp.concatenate([xr * hr - xi * hi, xr * hi + xi * hr], axis=0)
        if real_slot is None:
            out = _mm_split(mi[0][s], None, y, CONV_PASSES)
            _store_cat(p_ref, s * m, m, out[:m])
            _store_cat(p_ref, (half + s) * m, m, out[m:])
        else:
            out = _mm_split(mi[0][s, 0:m, :], None, y, CONV_PASSES)
            _store_cat(p_ref, real_slot * m, m, out)

    slab(0, real_slot=0)
    slab(half, real_slot=half)

    def slab_loop(s, carry):
        slab(s)
        return carry

    lax.fori_loop(1, half, slab_loop, 0, unroll=4)
    _inv_major(p_ref, zs_ref, dk, CONV_PASSES)

    def fin(i, carry):
        r0 = pl.multiple_of(i * CONV_CHUNK, CONV_CHUNK)
        for c in range(C_TILES):
            cl = slice(c * LANES, (c + 1) * LANES)
            gate = _sconv_chunk(xv, c, r0, CONV_CHUNK, seq, prm_ref, 4)
            if conv_z:
                zval = _sconv_chunk(zv, c, r0, CONV_CHUNK, seq, prm_ref, 0)
            else:
                zval = zv[c, pl.ds(r0, CONV_CHUNK), :]
            skip = prm_ref[0, 0, 8:9, cl]
            o_ref[0, c, pl.ds(r0, CONV_CHUNK), :] = gate * (zs_ref[c, pl.ds(r0, CONV_CHUNK), :] + skip * zval)
        return carry

    lax.fori_loop(0, n_chunks, fin, 0, unroll=2)


def _hyena_order(z, z_grp0, x, x_grp0, prm, order, spec, conv_z, n_grp):
    bsz, _, seq, _ = z.shape
    ak, dk, mf, mi = (_dft_hi_lo(n)[0] for n in ("ak", "dk", "mf", "mi"))
    one = pl.Buffered(1)
    fixed2 = lambda c, b: (0, 0)
    fixed3 = lambda c, b: (0, 0, 0)
    blk = (1, C_TILES, seq, LANES)
    return pl.pallas_call(
        functools.partial(_conv_body, conv_z=conv_z),
        out_shape=SDS((bsz, n_grp * C_TILES, seq, LANES), F32), grid=(n_grp, bsz),
        in_specs=[BS(blk, lambda c, b: (b, z_grp0 + c, 0, 0)),
                  BS(blk, lambda c, b: (b, x_grp0 + c, 0, 0)),
                  BS((1, 1, 16, C_GROUP), lambda c, b: (order, c, 0, 0)),
                  BS(ak.shape, fixed2, pipeline_mode=one), BS(dk.shape, fixed2, pipeline_mode=one),
                  BS(mf.shape, fixed3, pipeline_mode=one), BS(mi.shape, fixed3, pipeline_mode=one),
                  BS((1, 1, N_SLABS, 2, FFT_MINOR, C_GROUP), lambda c, b: (order, c, 0, 0, 0, 0), pipeline_mode=one)],
        out_specs=BS(blk, lambda c, b: (b, c, 0, 0)),
        scratch_shapes=[pltpu.VMEM((C_TILES, seq, LANES), F32),
                        pltpu.VMEM((C_TILES, FFT_MAJOR * FFT_MINOR, LANES), F32)],
        compiler_params=_cparams(2), name=f"hyena_conv{order}",
    )(z, x, prm, ak, dk, mf, mi, spec)


ROUTER_ROWS = 48
EXPERT_ROW0 = 8


def _outproj_body(att_ref, hy_ref, gwh_ref, wo_ref, x_ref, g1_ref, sh2_ref, sc2_ref, n2w_ref, wrh_ref, wrl_ref,
                  br_ref, tri_ref, x1_ref, h2_ref, ids_ref, gates_ref, cnt_ref, carry_ref):
    @pl.when((pl.program_id(0) == 0) & (pl.program_id(1) == 0))
    def _():
        carry_ref[...] = jnp.zeros_like(carry_ref)

    hy = jnp.concatenate([hy_ref[0, j] for j in range(hy_ref.shape[1])], axis=1)
    hyn = _rms(hy) * gwh_ref[...]
    mix = _mm(jnp.concatenate([att_ref[0], hyn.astype(BF16)], axis=1), wo_ref[...])
    x1 = x_ref[0] + g1_ref[0] * mix
    x1_ref[0] = x1
    h2 = _rms(x1) * n2w_ref[...]
    h2 = h2 * (1.0 + sc2_ref[0]) + sh2_ref[0]
    h2_ref[...] = h2

    h_hi, h_lo = _split(h2)
    lg = _mm_nt(wrh_ref[...], h_hi) + _mm_nt(wrh_ref[...], h_lo) + _mm_nt(wrl_ref[...], h_hi) + br_ref[...]
    tm = lg.shape[1]
    gl = lg[0:N_GROUPS]
    el = lg[EXPERT_ROW0:EXPERT_ROW0 + N_EXPERTS]
    gmax = jnp.max(gl, axis=0, keepdims=True)
    rg = lax.broadcasted_iota(I32, gl.shape, 0).astype(F32)
    g_idx = jnp.min(jnp.where(gl == gmax, rg, float(N_GROUPS)), axis=0, keepdims=True)
    g_val = 1.0 / jnp.sum(jnp.exp(gl - gmax), axis=0, keepdims=True)
    re_i = lax.broadcasted_iota(I32, el.shape, 0)
    re = re_i.astype(F32)
    in_group = (re_i // EXPERTS_PER_GROUP).astype(F32) == g_idx
    elm = jnp.where(in_group, el, NEG_INF)
    m1 = jnp.max(elm, axis=0, keepdims=True)
    i1 = jnp.min(jnp.where(elm == m1, re, float(N_EXPERTS)), axis=0, keepdims=True)
    elm2 = jnp.where(re == i1, NEG_INF, elm)
    m2 = jnp.max(elm2, axis=0, keepdims=True)
    i2 = jnp.min(jnp.where(elm2 == m2, re, float(N_EXPERTS)), axis=0, keepdims=True)
    e2 = jnp.exp(m2 - m1)
    inv = g_val / (1.0 + e2)

    oh1 = (re == i1).astype(F32)
    oh2 = (re == i2).astype(F32)
    oh = oh1 + oh2
    base = _mm(oh.astype(BF16), tri_ref[...]) + carry_ref[:, 0:1]
    rank1 = jnp.sum(oh1 * base, axis=0, keepdims=True)
    rank2 = jnp.sum(oh2 * base, axis=0, keepdims=True)
    carry_ref[...] = carry_ref[...] + jnp.sum(oh, axis=1, keepdims=True)
    cnt_ref[...] = carry_ref[...]

    zi = jnp.zeros((1, tm), I32)
    for k, val in enumerate((i1.astype(I32), i2.astype(I32), rank1.astype(I32), rank2.astype(I32), zi, zi, zi, zi)):
        ids_ref[k:k + 1, :] = val
    zf = jnp.zeros((1, tm), F32)
    for k, val in enumerate((inv, inv * e2, zf, zf, zf, zf, zf, zf)):
        gates_ref[k:k + 1, :] = val


def _outproj_route(att_n, hy, gw_hy, wo_bf, x, g1, sh2, sc2, norm2_w, wr_hi, wr_lo, br):
    bsz, seq, d = x.shape
    tm = ROW_TILE
    n_tok = bsz * seq
    att_w = att_n.shape[2]
    hy_tiles = hy.shape[1]
    tri = jnp.asarray(np.triu(np.ones((tm, tm), np.float32), k=1).astype(ml_dtypes.bfloat16))
    row = lambda b, i: (b, i, 0)
    per_b = lambda b, i: (b, 0, 0)
    fixed2 = lambda b, i: (0, 0)
    tok = lambda b, i: (0, b * (seq // tm) + i)
    return pl.pallas_call(
        _outproj_body,
        out_shape=(SDS((bsz, seq, d), F32), SDS((n_tok, d), F32), SDS((SUBLANES, n_tok), I32),
                   SDS((SUBLANES, n_tok), F32), SDS((N_EXPERTS, LANES), F32)),
        grid=(bsz, seq // tm),
        in_specs=[BS((1, tm, att_w), row), BS((1, hy_tiles, tm, LANES), lambda b, i: (b, 0, i, 0)),
                  BS((1, hy_tiles * LANES), fixed2), BS(wo_bf.shape, fixed2), BS((1, tm, d), row),
                  BS((1, 1, d), per_b), BS((1, 1, d), per_b), BS((1, 1, d), per_b), BS((1, d), fixed2),
                  BS((ROUTER_ROWS, d), fixed2), BS((ROUTER_ROWS, d), fixed2), BS((ROUTER_ROWS, 1), fixed2),
                  BS((tm, tm), fixed2)],
        out_specs=(BS((1, tm, d), row), BS((tm, d), lambda b, i: (b * (seq // tm) + i, 0)),
                   BS((SUBLANES, tm), tok), BS((SUBLANES, tm), tok), BS((N_EXPERTS, LANES), fixed2)),
        scratch_shapes=[pltpu.VMEM((N_EXPERTS, LANES), F32)],
        compiler_params=_cparams(2), name="out_proj_router",
    )(att_n, hy, gw_hy, wo_bf, x, g1, sh2, sc2, norm2_w.reshape(1, d), wr_hi, wr_lo, br, tri)


ISSUE_UNROLL = 8


def _row_copy(src, src_row, dst, dst_row, sem):
    return pltpu.make_async_copy(src.at[pl.ds(src_row, 1), :], dst.at[pl.ds(dst_row, 1), :], sem)


def _dispatch_body(pend_ref, d_ref, h_ref, buf_ref, zero_ref, sem):
    n = h_ref.shape[0]

    @pl.when(pl.program_id(0) == 0)
    def _():
        zero_ref[...] = jnp.zeros_like(zero_ref)

        def tail_copy(e):
            row0 = pl.multiple_of(pend_ref[e] - MOE_BLOCK, MOE_BLOCK)
            return pltpu.make_async_copy(zero_ref, buf_ref.at[pl.ds(row0, MOE_BLOCK), :], sem)

        def non_empty(e):
            return pend_ref[e] > (pend_ref[e - 1] if e > 0 else 0)

        def spare_copy(i):
            row0 = pl.multiple_of(i * MOE_BLOCK, MOE_BLOCK)
            return pltpu.make_async_copy(zero_ref, buf_ref.at[pl.ds(row0, MOE_BLOCK), :], sem)

        first_spare = pend_ref[N_EXPERTS - 1] // MOE_BLOCK
        n_blocks = buf_ref.shape[0] // MOE_BLOCK
        for e in range(N_EXPERTS):
            @pl.when(non_empty(e))
            def _(e=e):
                tail_copy(e).start()
        lax.fori_loop(first_spare, n_blocks, lambda i, c: (spare_copy(i).start(), c)[1], 0)
        for e in range(N_EXPERTS):
            @pl.when(non_empty(e))
            def _(e=e):
                tail_copy(e).wait()
        lax.fori_loop(first_spare, n_blocks, lambda i, c: (spare_copy(i).wait(), c)[1], 0)

    def issue(t, carry):
        for j in range(TOP_K):
            _row_copy(h_ref, t, buf_ref, d_ref[j, t], sem).start()
        return carry

    lax.fori_loop(0, n, issue, 0, unroll=ISSUE_UNROLL)
    for j in range(TOP_K):
        pltpu.make_async_copy(h_ref, buf_ref.at[pl.ds(0, n), :], sem).wait()


def _dispatch(pend, dest, h2, n_rows):
    n_tok, d = h2.shape
    tm = GATHER_TILE
    grid_spec = pltpu.PrefetchScalarGridSpec(
        num_scalar_prefetch=1, grid=(n_tok // tm,),
        in_specs=[BS((TOP_K, tm), lambda i, pend: (0, i), memory_space=pltpu.SMEM),
                  BS((tm, d), lambda i, pend: (i, 0))],
        out_specs=BS(memory_space=pl.ANY),
        scratch_shapes=[pltpu.VMEM((MOE_BLOCK, d), F32), pltpu.SemaphoreType.DMA(())])
    return pl.pallas_call(_dispatch_body, out_shape=SDS((n_rows, d), F32), grid_spec=grid_spec,
                          compiler_params=_cparams(1), name="moe_dispatch")(pend, dest, h2)


def _expert_body(be_ref, nu_ref, x_ref, w13_ref, w2_ref, y_ref):
    del be_ref
    i = pl.program_id(0)
    hidden = w2_ref.shape[1]

    @pl.when(i < nu_ref[0])
    def _():
        h = _mm(x_ref[...].astype(BF16), w13_ref[0])
        act = _silu(h[:, :hidden]) * h[:, hidden:]
        y_ref[...] = _mm(act.astype(BF16), w2_ref[0])

    @pl.when(i >= nu_ref[0])
    def _():
        y_ref[...] = jnp.zeros_like(y_ref)


def _experts(block_expert, n_used, buf, w13_bf, w2_bf):
    n_rows, d = buf.shape
    nb = n_rows // MOE_BLOCK
    grid_spec = pltpu.PrefetchScalarGridSpec(
        num_scalar_prefetch=2, grid=(nb,),
        in_specs=[BS((MOE_BLOCK, d), lambda i, be, nu: (jnp.minimum(i, nu[0] - 1), 0)),
                  BS((1,) + w13_bf.shape[1:], lambda i, be, nu: (be[i], 0, 0)),
                  BS((1,) + w2_bf.shape[1:], lambda i, be, nu: (be[i], 0, 0))],
        out_specs=BS((MOE_BLOCK, d), lambda i, be, nu: (i, 0)))
    return pl.pallas_call(_expert_body, out_shape=SDS((n_rows, d), F32), grid_spec=grid_spec,
                          compiler_params=_cparams(1), name="moe_experts")(block_expert, n_used, buf, w13_bf, w2_bf)


def _combine_body(d_ref, y_ref, x1_ref, g2_ref, gt_ref, o_ref, rows_ref, sem):
    n = x1_ref.shape[0]

    def issue(t, carry):
        for j in range(TOP_K):
            _row_copy(y_ref, d_ref[j, t], rows_ref.at[j], t, sem).start()
        return carry

    lax.fori_loop(0, n, issue, 0, unroll=ISSUE_UNROLL)
    for j in range(TOP_K):
        pltpu.make_async_copy(y_ref.at[pl.ds(0, n), :], rows_ref.at[j], sem).wait()
    gt = gt_ref[...]
    moe = gt[:, 0:1] * rows_ref[0] + gt[:, 1:2] * rows_ref[1]
    o_ref[...] = x1_ref[...] + g2_ref[0] * moe


def _combine(dest, y, x1, g2, gates_t, seq):
    n_tok, d = x1.shape
    tm = GATHER_TILE
    return pl.pallas_call(
        _combine_body, out_shape=SDS((n_tok, d), F32), grid=(n_tok // tm,),
        in_specs=[BS((TOP_K, tm), lambda i: (0, i), memory_space=pltpu.SMEM), BS(memory_space=pl.ANY),
                  BS((tm, d), lambda i: (i, 0)), BS((1, 1, d), lambda i: (i // (seq // tm), 0, 0)),
                  BS((tm, TOP_K), lambda i: (i, 0))],
        out_specs=BS((tm, d), lambda i: (i, 0)),
        scratch_shapes=[pltpu.VMEM((TOP_K, tm, d), F32), pltpu.SemaphoreType.DMA(())],
        compiler_params=_cparams(1), name="moe_combine",
    )(dest, y, x1, g2, gates_t)


def kernel(x, c, ctx, c_ctx, ada_w, ada_b, norm1_w, w_in, q_norm_w, k_norm_w, attn_sink, hy_conv_w, hy_conv_b, hy_w1,
           hy_b1, hy_w2, hy_b2, hy_w3, hy_sin_freq, hy_skip, group_norm_w, w_out, norm2_w, router_g_w, router_g_b,
           router_e_w, router_e_b, exp_w1, exp_w3, exp_w2):
    bsz, seq, d = x.shape
    assert ada_w.shape[0] == 1, "single-layer block"
    att_w = d // 2
    hy_w = d - att_w
    kv_w = N_KV_HEADS * HEAD_DIM
    n_heads = att_w // HEAD_DIM
    n_grp = hy_w // C_GROUP
    assert 2 * seq == FFT_MAJOR * FFT_MINOR and seq % ROW_TILE == 0 and hy_w % C_GROUP == 0
    assert w_in.shape[2] == att_w + 2 * kv_w + (HYENA_ORDER + 1) * hy_w

    pad = (-(bsz + 1)) % SUBLANES
    cc = jnp.concatenate([c, c_ctx[None, :], jnp.zeros((pad, d), F32)], axis=0)
    mod = _ada(cc, ada_w[0], ada_b[0])
    lat = [mod[:bsz, k * d:(k + 1) * d].reshape(bsz, 1, d) for k in range(6)]
    sh1, sc1, g1, sh2, sc2, g2 = lat
    csh1 = mod[bsz:bsz + 1, 0:d]
    csc1 = mod[bsz:bsz + 1, d:2 * d]

    w_in_bf = w_in[0].astype(BF16)
    qnw = jnp.tile(q_norm_w[0], n_heads).reshape(1, att_w)
    knw = jnp.tile(k_norm_w[0], N_KV_HEADS).reshape(1, kv_w)
    q, k, v, u = _inproj(x, sh1, sc1, norm1_w[0], w_in_bf, qnw, knw, att_w, kv_w)
    kx, vx = _ctxproj(ctx, csh1, csc1, norm1_w[0], w_in_bf[:, att_w:att_w + 2 * kv_w], knw, kv_w)

    gw = group_norm_w[0]
    att_n = _attention(attn_sink[0], q, k, v, kx, vx, gw[:att_w].reshape(1, att_w))

    hf = _filter_mlp(seq, hy_w1[0], hy_b1[0], hy_w2[0], hy_b2[0], hy_w3[0], hy_sin_freq[0], hy_w)
    spec = _spectrum(hf, seq, hy_w)
    cw = hy_conv_w[0].reshape(3, HYENA_ORDER + 1, n_grp, C_GROUP)
    cb = hy_conv_b[0].reshape(1, HYENA_ORDER + 1, n_grp, C_GROUP)
    taps = jnp.concatenate([cw, cb], axis=0)
    prm = jnp.zeros((HYENA_ORDER, n_grp, 16, C_GROUP), F32)
    for o in range(HYENA_ORDER):
        prm = prm.at[o, :, 4:8].set(jnp.transpose(taps[:, o + 1], (1, 0, 2)))
        prm = prm.at[o, :, 8].set(hy_skip[0, o].reshape(n_grp, C_GROUP))
    prm = prm.at[0, :, 0:4].set(jnp.transpose(taps[:, 0], (1, 0, 2)))
    z = _hyena_order(u, 0, u, n_grp, prm, 0, spec, True, n_grp)
    hy = _hyena_order(z, 0, u, 2 * n_grp, prm, 1, spec, False, n_grp)

    wr = jnp.zeros((ROUTER_ROWS, d), F32)
    wr = wr.at[0:N_GROUPS].set(router_g_w[0].T).at[EXPERT_ROW0:EXPERT_ROW0 + N_EXPERTS].set(router_e_w[0].T)
    br = jnp.zeros((ROUTER_ROWS, 1), F32)
    br = br.at[0:N_GROUPS, 0].set(router_g_b[0]).at[EXPERT_ROW0:EXPERT_ROW0 + N_EXPERTS, 0].set(router_e_b[0])
    wr_hi, wr_lo = _split(wr)
    x1, h2, ids, gates, counts = _outproj_route(att_n, hy, gw[att_w:].reshape(1, hy_w), w_out[0].astype(BF16), x,
                                                g1, sh2, sc2, norm2_w[0], wr_hi, wr_lo, br)

    n_tok = bsz * seq
    cnt = counts[:, 0].astype(I32)
    padded = (cnt + MOE_BLOCK - 1) // MOE_BLOCK * MOE_BLOCK
    pend = jnp.cumsum(padded)
    pstart = pend - padded
    e_iota = jnp.arange(N_EXPERTS, dtype=I32)
    is_e = ids[0:TOP_K, :, None] == e_iota
    dest = jnp.sum(jnp.where(is_e, pstart, 0), axis=-1) + ids[TOP_K:2 * TOP_K]
    n_blocks = (n_tok * TOP_K) // MOE_BLOCK + N_EXPERTS
    block_row0 = jnp.arange(n_blocks, dtype=I32) * MOE_BLOCK
    block_expert = jnp.minimum(jnp.sum((block_row0[:, None] >= pend[None, :]).astype(I32), axis=-1), N_EXPERTS - 1)
    n_used = (pend[-1:] // MOE_BLOCK).astype(I32)

    buf = _dispatch(pend.astype(I32), dest, h2, n_blocks * MOE_BLOCK)
    w13 = jnp.concatenate([exp_w1[0], exp_w3[0]], axis=2).astype(BF16)
    y = _experts(block_expert, n_used, buf, w13, exp_w2[0].astype(BF16))
    out = _combine(dest, y, x1.reshape(n_tok, d), g2, gates[0:TOP_K].T, seq)
    return out.reshape(bsz, seq, d)
```

```python
import functools
import math

import ml_dtypes
import numpy as np

import jax
import jax.numpy as jnp
from jax import lax
from jax.experimental import pallas as pl
from jax.experimental.pallas import tpu as pltpu

F32, BF16, I32 = jnp.float32, jnp.bfloat16, jnp.int32
SDS = jax.ShapeDtypeStruct
BS = pl.BlockSpec

HEAD_DIM = 64
N_KV_HEADS = 2
WINDOW = 128
Q_BLOCK = 128
GRID_W = 64
ROPE_THETA = 10000.0
EPS = 1e-6
NEG_INF = -1e30
LOG2E = math.log2(math.e)
HYENA_BANDS = 16
HYENA_ORDER = 2
DECAY_TARGET = 1e-2
FAST_DECAY_PCT = 0.3
SLOW_DECAY_PCT = 1.5
N_GROUPS = 4
EXPERTS_PER_GROUP = 8
N_EXPERTS = N_GROUPS * EXPERTS_PER_GROUP
TOP_K = 2

LANES = 128
SUBLANES = 8
VMEM_LIMIT = 60 * 1024 * 1024

ROW_TILE = 512
MOE_BLOCK = 512
GATHER_TILE = 1024
FILTER_TILE = 512

FFT_MINOR = 128
FFT_MAJOR = 64
N_SLABS = FFT_MAJOR // 2 + 1


def _cparams(n_grid):
    return pltpu.CompilerParams(dimension_semantics=("arbitrary",) * n_grid, vmem_limit_bytes=VMEM_LIMIT)


def _mm(a, b):
    return jnp.dot(a, b, preferred_element_type=F32)


def _mm_nt(a, b):
    return lax.dot_general(a, b, (((1,), (1,)), ((), ())), preferred_element_type=F32)


def _split(x):
    hi = x.astype(BF16)
    lo = (x - hi.astype(F32)).astype(BF16)
    return hi, lo


def _mm_split(w_hi, w_lo, x, passes):
    if passes == 1:
        return _mm(w_hi, x.astype(BF16))
    x_hi, x_lo = _split(x)
    out = _mm(w_hi, x_hi) + _mm(w_hi, x_lo)
    if passes == 3:
        out = out + _mm(w_lo, x_hi)
    return out


def _mm3(a, b):
    a_hi, a_lo = _split(a)
    b_hi, b_lo = _split(b)
    return _mm(a_hi, b_hi) + _mm(a_hi, b_lo) + _mm(a_lo, b_hi)


def _silu(x):
    return x * (1.0 / (1.0 + jnp.exp(-x)))


def _rms(x):
    return x * lax.rsqrt(jnp.mean(x * x, axis=-1, keepdims=True) + EPS)


@functools.lru_cache(maxsize=None)
def _rope_tables(seq_len):
    pos = np.arange(seq_len)
    rows = (pos // GRID_W).astype(np.float64)
    cols = (pos % GRID_W).astype(np.float64)
    pairs = HEAD_DIM // 4
    inv_freq = ROPE_THETA ** (-np.arange(pairs, dtype=np.float64) / pairs)
    ang_r = rows[:, None] * inv_freq[None, :]
    ang_c = cols[:, None] * inv_freq[None, :]
    ang = np.concatenate([ang_r, ang_r, ang_c, ang_c], axis=1)
    sign = np.tile(np.concatenate([-np.ones(pairs), np.ones(pairs)]), 2)
    cos = np.cos(ang)
    sin = np.sin(ang) * sign[None, :]
    reps = LANES // HEAD_DIM
    return np.tile(cos, (1, reps)).astype(np.float32), np.tile(sin, (1, reps)).astype(np.float32)


@functools.lru_cache(maxsize=None)
def _block_diag_mean(width):
    h = np.arange(width) // HEAD_DIM
    return ((h[:, None] == h[None, :]).astype(np.float32) / HEAD_DIM).astype(ml_dtypes.bfloat16)


@functools.lru_cache(maxsize=None)
def _filter_features(seq_len, width):
    pos = np.arange(seq_len, dtype=np.float64)
    t = pos / seq_len
    bands = np.linspace(1e-4, HYENA_BANDS - 1, HYENA_BANDS)
    ang = (2.0 * math.pi / seq_len) * pos[:, None] * bands[None, :]
    feat = np.concatenate([t[:, None], np.cos(ang), -np.sin(ang)], axis=-1)
    out = np.zeros((seq_len, width), np.float32)
    out[:, :feat.shape[1]] = feat
    return out


@functools.lru_cache(maxsize=None)
def _decay_rates(width):
    max_decay = math.log(DECAY_TARGET) / FAST_DECAY_PCT
    min_decay = math.log(DECAY_TARGET) / SLOW_DECAY_PCT
    return np.abs(np.linspace(min_decay, max_decay, width)).astype(np.float32)


@functools.lru_cache(maxsize=None)
def _dft_tables():
    n = FFT_MAJOR * FFT_MINOR
    half = FFT_MAJOR // 2
    n2 = np.arange(half)
    a = np.zeros((FFT_MAJOR, half))
    a[:half] = np.cos(2 * np.pi * np.arange(half)[:, None] * n2[None, :] / FFT_MAJOR)
    a[half] = (-1.0) ** n2
    k2 = np.arange(1, half)
    a[half + 1:] = -np.sin(2 * np.pi * k2[:, None] * n2[None, :] / FFT_MAJOR)
    d = np.zeros((half, FFT_MAJOR))
    d[:, 0] = 1.0 / n
    d[:, 1:half] = 2 * np.cos(2 * np.pi * n2[:, None] * k2[None, :] / FFT_MAJOR) / n
    d[:, half] = (-1.0) ** n2 / n
    d[:, half + 1:] = -2 * np.sin(2 * np.pi * n2[:, None] * k2[None, :] / FFT_MAJOR) / n
    eye = np.eye(SUBLANES)
    ak = np.kron(a, eye)
    dk = np.kron(d, eye)
    s = np.arange(N_SLABS)[:, None, None]
    k1 = np.arange(FFT_MINOR)[None, :, None]
    n1 = np.arange(FFT_MINOR)[None, None, :]
    g = np.exp(-2j * np.pi * n1 * (FFT_MAJOR * k1 + s) / n)
    mf = np.block([[g.real, -g.imag], [g.imag, g.real]])
    gi = np.conj(g).transpose(0, 2, 1)
    mi = np.block([[gi.real, -gi.imag], [gi.imag, gi.real]])
    return {name: m.astype(np.float32) for name, m in (("ak", ak), ("dk", dk), ("mf", mf), ("mi", mi))}


def _dft_hi_lo(name):
    return _split(jnp.asarray(_dft_tables()[name]))


def _ada_body(c_ref, w_ref, b_ref, o_ref):
    o_ref[...] = _mm3(_silu(c_ref[...]), w_ref[...]) + b_ref[...]


def _ada(cc, ada_w, ada_b):
    rows, d = cc.shape
    n = ada_w.shape[1]
    tn = 1536
    return pl.pallas_call(
        _ada_body, out_shape=SDS((rows, n), F32), grid=(n // tn,),
        in_specs=[BS((rows, d), lambda j: (0, 0)), BS((d, tn), lambda j: (0, j)), BS((1, tn), lambda j: (0, j))],
        out_specs=BS((rows, tn), lambda j: (0, j)), compiler_params=_cparams(1), name="ada_mod",
    )(cc, ada_w, ada_b.reshape(1, n))


def _head_rms(t, bd):
    hi, lo = _split(t * t)
    return t * lax.rsqrt(_mm(hi, bd) + _mm(lo, bd) + EPS)


def _rope(t, cos, sin):
    n = t.shape[1]
    quarter = HEAD_DIM // 4
    lane = lax.broadcasted_iota(I32, t.shape, 1)
    up = pltpu.roll(t, n - quarter, axis=1)
    dn = pltpu.roll(t, quarter, axis=1)
    partner = jnp.where((lane & (2 * quarter - 1)) < quarter, up, dn)
    return t * cos + partner * sin


def _inproj_body(x_ref, sh_ref, sc_ref, nw_ref, w_ref, qnw_ref, knw_ref, cos_ref, sin_ref, bdq_ref, bdk_ref,
                 q_ref, k_ref, v_ref, u_ref, *, att_w, kv_w):
    h = _rms(x_ref[0]) * nw_ref[...]
    h = h * (1.0 + sc_ref[0]) + sh_ref[0]
    proj = _mm(h.astype(BF16), w_ref[...])
    cos, sin = cos_ref[...], sin_ref[...]
    reps = att_w // LANES
    cos_q = jnp.concatenate([cos] * reps, axis=1)
    sin_q = jnp.concatenate([sin] * reps, axis=1)
    q = _head_rms(proj[:, :att_w], bdq_ref[...]) * qnw_ref[...]
    q_ref[0] = (_rope(q, cos_q, sin_q) * (HEAD_DIM ** -0.5 * LOG2E)).astype(BF16)
    k = _head_rms(proj[:, att_w:att_w + kv_w], bdk_ref[...]) * knw_ref[...]
    k_ref[0] = _rope(k, cos, sin).astype(BF16)
    v_ref[0] = proj[:, att_w + kv_w:att_w + 2 * kv_w].astype(BF16)
    hy_off = att_w + 2 * kv_w
    for j in range(u_ref.shape[1]):
        u_ref[0, j] = proj[:, hy_off + LANES * j:hy_off + LANES * (j + 1)]


def _inproj(x, sh, sc, norm_w, w_bf, qnw, knw, att_w, kv_w):
    bsz, seq, d = x.shape
    tm = ROW_TILE
    hy_tiles = (w_bf.shape[1] - att_w - 2 * kv_w) // LANES
    cos, sin = _rope_tables(seq)
    row = lambda b, i: (b, i, 0)
    fixed2 = lambda b, i: (0, 0)
    per_b = lambda b, i: (b, 0, 0)
    return pl.pallas_call(
        functools.partial(_inproj_body, att_w=att_w, kv_w=kv_w),
        out_shape=(SDS((bsz, seq, att_w), BF16), SDS((bsz, seq, kv_w), BF16), SDS((bsz, seq, kv_w), BF16),
                   SDS((bsz, hy_tiles, seq, LANES), F32)),
        grid=(bsz, seq // tm),
        in_specs=[BS((1, tm, d), row), BS((1, 1, d), per_b), BS((1, 1, d), per_b), BS((1, d), fixed2),
                  BS(w_bf.shape, fixed2), BS((1, att_w), fixed2), BS((1, kv_w), fixed2),
                  BS((tm, LANES), lambda b, i: (i, 0)), BS((tm, LANES), lambda b, i: (i, 0)),
                  BS((att_w, att_w), fixed2), BS((kv_w, kv_w), fixed2)],
        out_specs=(BS((1, tm, att_w), row), BS((1, tm, kv_w), row), BS((1, tm, kv_w), row),
                   BS((1, hy_tiles, tm, LANES), lambda b, i: (b, 0, i, 0))),
        compiler_params=_cparams(2), name="in_proj",
    )(x, sh, sc, norm_w.reshape(1, d), w_bf, qnw, knw, jnp.asarray(cos), jnp.asarray(sin),
      jnp.asarray(_block_diag_mean(att_w)), jnp.asarray(_block_diag_mean(kv_w)))


def _ctxproj_body(x_ref, sh_ref, sc_ref, nw_ref, w_ref, knw_ref, bdk_ref, k_ref, v_ref, *, kv_w):
    h = _rms(x_ref[0]) * nw_ref[...]
    h = h * (1.0 + sc_ref[...]) + sh_ref[...]
    proj = _mm(h.astype(BF16), w_ref[...])
    k_ref[0] = (_head_rms(proj[:, :kv_w], bdk_ref[...]) * knw_ref[...]).astype(BF16)
    v_ref[0] = proj[:, kv_w:].astype(BF16)


def _ctxproj(ctx, sh, sc, norm_w, w_kv_bf, knw, kv_w):
    bsz, n_ctx, d = ctx.shape
    fixed2 = lambda b: (0, 0)
    row = lambda b: (b, 0, 0)
    return pl.pallas_call(
        functools.partial(_ctxproj_body, kv_w=kv_w),
        out_shape=(SDS((bsz, n_ctx, kv_w), BF16), SDS((bsz, n_ctx, kv_w), BF16)), grid=(bsz,),
        in_specs=[BS((1, n_ctx, d), row), BS((1, d), fixed2), BS((1, d), fixed2), BS((1, d), fixed2),
                  BS(w_kv_bf.shape, fixed2), BS((1, kv_w), fixed2), BS((kv_w, kv_w), fixed2)],
        out_specs=(BS((1, n_ctx, kv_w), row), BS((1, n_ctx, kv_w), row)),
        compiler_params=_cparams(1), name="ctx_proj",
    )(ctx, sh, sc, norm_w.reshape(1, d), w_kv_bf, knw, jnp.asarray(_block_diag_mean(kv_w)))


def _attn_body(sink_ref, q_ref, kp_ref, kc_ref, kn_ref, vp_ref, vc_ref, vn_ref, kx_ref, vx_ref, gw_ref, o_ref):
    i = pl.program_id(1)
    last = pl.num_programs(1) - 1
    n_ctx = kx_ref.shape[1]
    n_heads = q_ref.shape[2] // HEAD_DIM
    group = n_heads // N_KV_HEADS
    rows = group * Q_BLOCK
    n_band = 3 * Q_BLOCK
    r = lax.broadcasted_iota(I32, (rows, n_band), 0) & (Q_BLOCK - 1)
    j = lax.broadcasted_iota(I32, (rows, n_band), 1)
    in_band = (j >= r) & (j <= r + 2 * WINDOW)
    in_seq = jnp.logical_or(j >= Q_BLOCK, i > 0) & jnp.logical_or(j < 2 * Q_BLOCK, i < last)
    valid = in_band & in_seq
    head_of_row = lax.broadcasted_iota(I32, (rows, 1), 0) // Q_BLOCK
    outs = []
    for g in range(N_KV_HEADS):
        sl = slice(g * HEAD_DIM, (g + 1) * HEAD_DIM)
        kb = jnp.concatenate([kx_ref[0][:, sl], kp_ref[0][:, sl], kc_ref[0][:, sl], kn_ref[0][:, sl]], axis=0)
        vb = jnp.concatenate([vx_ref[0][:, sl], vp_ref[0][:, sl], vc_ref[0][:, sl], vn_ref[0][:, sl]], axis=0)
        heads = range(g * group, (g + 1) * group)
        q4 = jnp.concatenate([q_ref[0][:, h * HEAD_DIM:(h + 1) * HEAD_DIM] for h in heads], axis=0)
        sink = jnp.zeros((rows, 1), F32)
        for hh, h in enumerate(heads):
            sink = jnp.where(head_of_row == hh, sink_ref[h] * LOG2E, sink)
        s = _mm_nt(q4, kb)
        s_ctx = s[:, :n_ctx]
        s_band = jnp.where(valid, s[:, n_ctx:], NEG_INF)
        m = jnp.maximum(jnp.maximum(jnp.max(s_ctx, axis=-1, keepdims=True), jnp.max(s_band, axis=-1, keepdims=True)),
                        sink)
        p_ctx = jnp.exp2((s_ctx - m).astype(BF16))
        p_band = jnp.exp2((s_band - m).astype(BF16))
        denom = (jnp.sum(p_ctx.astype(F32), axis=-1, keepdims=True) + jnp.sum(p_band.astype(F32), axis=-1, keepdims=True)
                 + jnp.exp2(sink - m))
        o4 = (_mm(p_ctx, vb[:n_ctx]) + _mm(p_band, vb[n_ctx:])) / denom
        for hh in range(group):
            outs.append(o4[hh * Q_BLOCK:(hh + 1) * Q_BLOCK])
    att = jnp.concatenate(outs, axis=1)
    o_ref[0] = (_rms(att) * gw_ref[...]).astype(BF16)


def _attention(sink, q, k, v, kx, vx, gw):
    bsz, seq, att_w = q.shape
    kv_w = k.shape[2]
    n_ctx = kx.shape[1]
    nb = seq // Q_BLOCK
    cur = lambda b, i: (b, i, 0)
    prev = lambda b, i: (b, jnp.maximum(i - 1, 0), 0)
    nxt = lambda b, i: (b, jnp.minimum(i + 1, nb - 1), 0)
    per_b = lambda b, i: (b, 0, 0)
    kvb = (1, Q_BLOCK, kv_w)
    return pl.pallas_call(
        _attn_body, out_shape=SDS((bsz, seq, att_w), BF16), grid=(bsz, nb),
        in_specs=[BS(memory_space=pltpu.SMEM), BS((1, Q_BLOCK, att_w), cur),
                  BS(kvb, prev), BS(kvb, cur), BS(kvb, nxt), BS(kvb, prev), BS(kvb, cur), BS(kvb, nxt),
                  BS((1, n_ctx, kv_w), per_b), BS((1, n_ctx, kv_w), per_b), BS((1, att_w), lambda b, i: (0, 0))],
        out_specs=BS((1, Q_BLOCK, att_w), cur), compiler_params=_cparams(2), name="window_attn",
    )(sink, q, k, k, k, v, v, v, kx, vx, gw)


def _filter_body(f_ref, w1h, w1l, b1, w2h, w2l, b2, w3h, w3l, sf_ref, dl_ref, o_ref, *, hy_w):
    def mm_w(a, wh, wl):
        a_hi, a_lo = _split(a)
        return _mm(a_hi, wh[...]) + _mm(a_hi, wl[...]) + _mm(a_lo, wh[...])

    f = f_ref[...]
    h = jnp.sin(sf_ref[0:1, :] * (mm_w(f, w1h, w1l) + b1[...]))
    h = jnp.sin(sf_ref[1:2, :] * (mm_w(h, w2h, w2l) + b2[...]))
    h = mm_w(h, w3h, w3l)
    h = h * jnp.exp(-f[:, 0:1] * dl_ref[...])
    tl = f.shape[0]
    row = lax.broadcasted_iota(I32, h.shape, 0) + pl.program_id(0) * tl
    col = lax.broadcasted_iota(I32, h.shape, 1)
    is_bwd = ((col // hy_w) & 1) == 1
    h = jnp.where((row == 0) & is_bwd, 0.0, h)
    for j in range(o_ref.shape[0]):
        o_ref[j] = h[:, LANES * j:LANES * (j + 1)]


def _filter_mlp(seq, w1, b1, w2, b2, w3, sin_freq, hy_w):
    hid = w1.shape[1]
    fw = 64
    feat = jnp.asarray(_filter_features(seq, fw))
    w1p = jnp.zeros((fw, hid), F32).at[:w1.shape[0]].set(w1)
    n_out = w3.shape[1]
    delta = jnp.asarray(np.tile(_decay_rates(hy_w), n_out // hy_w)).reshape(1, n_out)
    tl = FILTER_TILE
    fixed = lambda i: (0, 0)
    ops = []
    for w in (w1p, w2, w3):
        ops.extend(_split(w))
    return pl.pallas_call(
        functools.partial(_filter_body, hy_w=hy_w),
        out_shape=SDS((n_out // LANES, seq, LANES), F32), grid=(seq // tl,),
        in_specs=[BS((tl, fw), lambda i: (i, 0)), BS((fw, hid), fixed), BS((fw, hid), fixed), BS((1, hid), fixed),
                  BS((hid, hid), fixed), BS((hid, hid), fixed), BS((1, hid), fixed),
                  BS((hid, n_out), fixed), BS((hid, n_out), fixed), BS((2, hid), fixed), BS((1, n_out), fixed)],
        out_specs=BS((n_out // LANES, tl, LANES), lambda i: (0, i, 0)),
        compiler_params=_cparams(1), name="hyena_filter_mlp",
    )(feat, ops[0], ops[1], b1.reshape(1, hid), ops[2], ops[3], b2.reshape(1, hid), ops[4], ops[5], sin_freq, delta)


C_TILES = 2
C_GROUP = C_TILES * LANES


def _load_cat(ref, row0, n_rows):
    return jnp.concatenate([ref[c, pl.ds(row0, n_rows), :] for c in range(C_TILES)], axis=1)


def _store_cat(ref, row0, n_rows, val):
    for c in range(C_TILES):
        ref[c, pl.ds(row0, n_rows), :] = val[:, c * LANES:(c + 1) * LANES]


def _fwd_major(src, p_ref, ak, passes):
    half = FFT_MAJOR // 2

    def body(g, carry):
        r0 = pl.multiple_of(g * SUBLANES, SUBLANES)
        st = jnp.concatenate([_load_cat(src, n2 * FFT_MINOR + r0, SUBLANES) for n2 in range(half)], axis=0)
        out = _mm_split(ak[0][...], ak[1][...] if passes == 3 else None, st, passes)
        for p in range(FFT_MAJOR):
            _store_cat(p_ref, p * FFT_MINOR + r0, SUBLANES, out[p * SUBLANES:(p + 1) * SUBLANES])
        return carry

    lax.fori_loop(0, FFT_MINOR // SUBLANES, body, 0, unroll=4)


def _inv_major(p_ref, dst, dk, passes):
    half = FFT_MAJOR // 2

    def body(g, carry):
        r0 = pl.multiple_of(g * SUBLANES, SUBLANES)
        st = jnp.concatenate([_load_cat(p_ref, p * FFT_MINOR + r0, SUBLANES) for p in range(FFT_MAJOR)], axis=0)
        out = _mm_split(dk[0][...], dk[1][...] if passes == 3 else None, st, passes)
        for n2 in range(half):
            _store_cat(dst, n2 * FFT_MINOR + r0, SUBLANES, out[n2 * SUBLANES:(n2 + 1) * SUBLANES])
        return carry

    lax.fori_loop(0, FFT_MINOR // SUBLANES, body, 0, unroll=4)


def _slab_spectrum(p_ref, mf, s, passes, real_slot=None):
    m = FFT_MINOR
    if real_slot is None:
        st = jnp.concatenate([_load_cat(p_ref, s * m, m), _load_cat(p_ref, (FFT_MAJOR // 2 + s) * m, m)], axis=0)
        x = _mm_split(mf[0][s], mf[1][s] if passes == 3 else None, st, passes)
    else:
        st = _load_cat(p_ref, real_slot * m, m)
        x = _mm_split(mf[0][s, :, 0:m], mf[1][s, :, 0:m] if passes == 3 else None, st, passes)
    return x[:m], x[m:]


def _spectrum_body(f_ref, b_ref, akh, akl, mfh, mfl, o_ref, p_ref):
    half = FFT_MAJOR // 2
    ak, mf = (akh, akl), (mfh, mfl)
    for is_bwd, src in ((False, f_ref), (True, b_ref)):
        _fwd_major(src, p_ref, ak, 3)

        def put(s, re, im, is_bwd=is_bwd):
            if is_bwd:
                o_ref[0, 0, s, 0] = o_ref[0, 0, s, 0] + re
                o_ref[0, 0, s, 1] = o_ref[0, 0, s, 1] - im
            else:
                o_ref[0, 0, s, 0] = re
                o_ref[0, 0, s, 1] = im

        put(0, *_slab_spectrum(p_ref, mf, 0, 3, real_slot=0))
        put(half, *_slab_spectrum(p_ref, mf, half, 3, real_slot=half))

        def body(s, carry):
            put(s, *_slab_spectrum(p_ref, mf, s, 3))
            return carry

        lax.fori_loop(1, half, body, 0, unroll=2)


def _spectrum(hf, seq, hy_w):
    ak, mf = _dft_hi_lo("ak"), _dft_hi_lo("mf")
    n_grp = hy_w // C_GROUP
    fixed2 = lambda o, c: (0, 0)
    fixed3 = lambda o, c: (0, 0, 0)
    one = pl.Buffered(1)
    return pl.pallas_call(
        _spectrum_body,
        out_shape=SDS((HYENA_ORDER, n_grp, N_SLABS, 2, FFT_MINOR, C_GROUP), F32), grid=(HYENA_ORDER, n_grp),
        in_specs=[BS((C_TILES, seq, LANES), lambda o, c: (o * 2 * n_grp + c, 0, 0), pipeline_mode=one),
                  BS((C_TILES, seq, LANES), lambda o, c: (o * 2 * n_grp + n_grp + c, 0, 0), pipeline_mode=one),
                  BS(ak[0].shape, fixed2, pipeline_mode=one), BS(ak[0].shape, fixed2, pipeline_mode=one),
                  BS(mf[0].shape, fixed3, pipeline_mode=one), BS(mf[0].shape, fixed3, pipeline_mode=one)],
        out_specs=BS((1, 1, N_SLABS, 2, FFT_MINOR, C_GROUP), lambda o, c: (o, c, 0, 0, 0, 0)),
        scratch_shapes=[pltpu.VMEM((C_TILES, FFT_MAJOR * FFT_MINOR, LANES), F32)],
        compiler_params=_cparams(2), name="hyena_spectrum",
    )(hf, hf, ak[0], ak[1], mf[0], mf[1])


CONV_CHUNK = 128
CONV_PASSES = 1


def _sconv_chunk(ref, c, r0, n_rows, seq, prm_ref, base):
    cl = slice(c * LANES, (c + 1) * LANES)
    cur = ref[c, pl.ds(r0, n_rows), :]
    row = lax.broadcasted_iota(I32, cur.shape, 0)
    before = ref[c, pl.ds(jnp.maximum(r0 - 1, 0), 1), :] * (r0 > 0).astype(F32)
    after = ref[c, pl.ds(jnp.minimum(r0 + n_rows, seq - 1), 1), :] * (r0 + n_rows < seq).astype(F32)
    prev = jnp.where(row == 0, before, pltpu.roll(cur, 1, axis=0))
    nxt = jnp.where(row == n_rows - 1, after, pltpu.roll(cur, n_rows - 1, axis=0))
    w = lambda k: prm_ref[0, 0, base + k:base + k + 1, cl]
    return w(3) + w(0) * prev + w(1) * cur + w(2) * nxt


def _conv_body(z_ref, x_ref, prm_ref, akh, dkh, mfh, mih, h_ref, o_ref, zs_ref, p_ref, *, conv_z):
    seq = z_ref.shape[2]
    half = FFT_MAJOR // 2
    m = FFT_MINOR
    zv, xv = z_ref.at[0], x_ref.at[0]
    ak, dk, mf, mi = (akh, None), (dkh, None), (mfh, None), (mih, None)
    n_chunks = seq // CONV_CHUNK

    if conv_z:
        def prep(i, carry):
            r0 = pl.multiple_of(i * CONV_CHUNK, CONV_CHUNK)
            for c in range(C_TILES):
                zs_ref[c, pl.ds(r0, CONV_CHUNK), :] = _sconv_chunk(zv, c, r0, CONV_CHUNK, seq, prm_ref, 0)
            return carry

        lax.fori_loop(0, n_chunks, prep, 0)
        src = zs_ref
    else:
        src = zv
    _fwd_major(src, p_ref, ak, CONV_PASSES)

    def slab(s, real_slot=None):
        xr, xi = _slab_spectrum(p_ref, mf, s, CONV_PASSES, real_slot)
        hr, hi = h_ref[0, 0, s, 0], h_ref[0, 0, s, 1]
        y = jnp.concatenate([xr * hr - xi * hi, xr * hi + xi * hr], axis=0)
        if real_slot is None:
            out = _mm_split(mi[0][s], None, y, CONV_PASSES)
            _store_cat(p_ref, s * m, m, out[:m])
            _store_cat(p_ref, (half + s) * m, m, out[m:])
        else:
            out = _mm_split(mi[0][s, 0:m, :], None, y, CONV_PASSES)
            _store_cat(p_ref, real_slot * m, m, out)

    slab(0, real_slot=0)
    slab(half, real_slot=half)

    def slab_loop(s, carry):
        slab(s)
        return carry

    lax.fori_loop(1, half, slab_loop, 0, unroll=8)
    _inv_major(p_ref, zs_ref, dk, CONV_PASSES)

    def fin(i, carry):
        r0 = pl.multiple_of(i * CONV_CHUNK, CONV_CHUNK)
        for c in range(C_TILES):
            cl = slice(c * LANES, (c + 1) * LANES)
            gate = _sconv_chunk(xv, c, r0, CONV_CHUNK, seq, prm_ref, 4)
            if conv_z:
                zval = _sconv_chunk(zv, c, r0, CONV_CHUNK, seq, prm_ref, 0)
            else:
                zval = zv[c, pl.ds(r0, CONV_CHUNK), :]
            skip = prm_ref[0, 0, 8:9, cl]
            o_ref[0, c, pl.ds(r0, CONV_CHUNK), :] = gate * (zs_ref[c, pl.ds(r0, CONV_CHUNK), :] + skip * zval)
        return carry

    lax.fori_loop(0, n_chunks, fin, 0, unroll=2)


def _hyena_order(z, z_grp0, x, x_grp0, prm, order, spec, conv_z, n_grp):
    bsz, _, seq, _ = z.shape
    ak, dk, mf, mi = (_dft_hi_lo(n)[0] for n in ("ak", "dk", "mf", "mi"))
    one = pl.Buffered(1)
    fixed2 = lambda c, b: (0, 0)
    fixed3 = lambda c, b: (0, 0, 0)
    blk = (1, C_TILES, seq, LANES)
    return pl.pallas_call(
        functools.partial(_conv_body, conv_z=conv_z),
        out_shape=SDS((bsz, n_grp * C_TILES, seq, LANES), F32), grid=(n_grp, bsz),
        in_specs=[BS(blk, lambda c, b: (b, z_grp0 + c, 0, 0)),
                  BS(blk, lambda c, b: (b, x_grp0 + c, 0, 0)),
                  BS((1, 1, 16, C_GROUP), lambda c, b: (order, c, 0, 0)),
                  BS(ak.shape, fixed2, pipeline_mode=one), BS(dk.shape, fixed2, pipeline_mode=one),
                  BS(mf.shape, fixed3, pipeline_mode=one), BS(mi.shape, fixed3, pipeline_mode=one),
                  BS((1, 1, N_SLABS, 2, FFT_MINOR, C_GROUP), lambda c, b: (order, c, 0, 0, 0, 0), pipeline_mode=one)],
        out_specs=BS(blk, lambda c, b: (b, c, 0, 0)),
        scratch_shapes=[pltpu.VMEM((C_TILES, seq, LANES), F32),
                        pltpu.VMEM((C_TILES, FFT_MAJOR * FFT_MINOR, LANES), F32)],
        compiler_params=_cparams(2), name=f"hyena_conv{order}",
    )(z, x, prm, ak, dk, mf, mi, spec)


ROUTER_ROWS = 48
EXPERT_ROW0 = 8


def _outproj_body(att_ref, hy_ref, gwh_ref, wo_ref, x_ref, g1_ref, sh2_ref, sc2_ref, n2w_ref, wrh_ref, wrl_ref,
                  br_ref, tri_ref, x1_ref, h2_ref, ids_ref, gates_ref, cnt_ref, carry_ref):
    @pl.when((pl.program_id(0) == 0) & (pl.program_id(1) == 0))
    def _():
        carry_ref[...] = jnp.zeros_like(carry_ref)

    hy = jnp.concatenate([hy_ref[0, j] for j in range(hy_ref.shape[1])], axis=1)
    hyn = _rms(hy) * gwh_ref[...]
    mix = _mm(jnp.concatenate([att_ref[0], hyn.astype(BF16)], axis=1), wo_ref[...])
    x1 = x_ref[0] + g1_ref[0] * mix
    x1_ref[0] = x1
    h2 = _rms(x1) * n2w_ref[...]
    h2 = h2 * (1.0 + sc2_ref[0]) + sh2_ref[0]
    h2_ref[...] = h2

    h_hi, h_lo = _split(h2)
    lg = _mm_nt(wrh_ref[...], h_hi) + _mm_nt(wrh_ref[...], h_lo) + _mm_nt(wrl_ref[...], h_hi) + br_ref[...]
    tm = lg.shape[1]
    gl = lg[0:N_GROUPS]
    el = lg[EXPERT_ROW0:EXPERT_ROW0 + N_EXPERTS]
    gmax = jnp.max(gl, axis=0, keepdims=True)
    rg = lax.broadcasted_iota(I32, gl.shape, 0).astype(F32)
    g_idx = jnp.min(jnp.where(gl == gmax, rg, float(N_GROUPS)), axis=0, keepdims=True)
    g_val = 1.0 / jnp.sum(jnp.exp(gl - gmax), axis=0, keepdims=True)
    re_i = lax.broadcasted_iota(I32, el.shape, 0)
    re = re_i.astype(F32)
    in_group = (re_i // EXPERTS_PER_GROUP).astype(F32) == g_idx
    elm = jnp.where(in_group, el, NEG_INF)
    m1 = jnp.max(elm, axis=0, keepdims=True)
    i1 = jnp.min(jnp.where(elm == m1, re, float(N_EXPERTS)), axis=0, keepdims=True)
    elm2 = jnp.where(re == i1, NEG_INF, elm)
    m2 = jnp.max(elm2, axis=0, keepdims=True)
    i2 = jnp.min(jnp.where(elm2 == m2, re, float(N_EXPERTS)), axis=0, keepdims=True)
    e2 = jnp.exp(m2 - m1)
    inv = g_val / (1.0 + e2)

    oh1 = (re == i1).astype(F32)
    oh2 = (re == i2).astype(F32)
    oh = oh1 + oh2
    base = _mm(oh.astype(BF16), tri_ref[...]) + carry_ref[:, 0:1]
    rank1 = jnp.sum(oh1 * base, axis=0, keepdims=True)
    rank2 = jnp.sum(oh2 * base, axis=0, keepdims=True)
    carry_ref[...] = carry_ref[...] + jnp.sum(oh, axis=1, keepdims=True)
    cnt_ref[...] = carry_ref[...]

    zi = jnp.zeros((1, tm), I32)
    for k, val in enumerate((i1.astype(I32), i2.astype(I32), rank1.astype(I32), rank2.astype(I32), zi, zi, zi, zi)):
        ids_ref[k:k + 1, :] = val
    zf = jnp.zeros((1, tm), F32)
    for k, val in enumerate((inv, inv * e2, zf, zf, zf, zf, zf, zf)):
        gates_ref[k:k + 1, :] = val


def _outproj_route(att_n, hy, gw_hy, wo_bf, x, g1, sh2, sc2, norm2_w, wr_hi, wr_lo, br):
    bsz, seq, d = x.shape
    tm = ROW_TILE
    n_tok = bsz * seq
    att_w = att_n.shape[2]
    hy_tiles = hy.shape[1]
    tri = jnp.asarray(np.triu(np.ones((tm, tm), np.float32), k=1).astype(ml_dtypes.bfloat16))
    row = lambda b, i: (b, i, 0)
    per_b = lambda b, i: (b, 0, 0)
    fixed2 = lambda b, i: (0, 0)
    tok = lambda b, i: (0, b * (seq // tm) + i)
    return pl.pallas_call(
        _outproj_body,
        out_shape=(SDS((bsz, seq, d), F32), SDS((n_tok, d), F32), SDS((SUBLANES, n_tok), I32),
                   SDS((SUBLANES, n_tok), F32), SDS((N_EXPERTS, LANES), F32)),
        grid=(bsz, seq // tm),
        in_specs=[BS((1, tm, att_w), row), BS((1, hy_tiles, tm, LANES), lambda b, i: (b, 0, i, 0)),
                  BS((1, hy_tiles * LANES), fixed2), BS(wo_bf.shape, fixed2), BS((1, tm, d), row),
                  BS((1, 1, d), per_b), BS((1, 1, d), per_b), BS((1, 1, d), per_b), BS((1, d), fixed2),
                  BS((ROUTER_ROWS, d), fixed2), BS((ROUTER_ROWS, d), fixed2), BS((ROUTER_ROWS, 1), fixed2),
                  BS((tm, tm), fixed2)],
        out_specs=(BS((1, tm, d), row), BS((tm, d), lambda b, i: (b * (seq // tm) + i, 0)),
                   BS((SUBLANES, tm), tok), BS((SUBLANES, tm), tok), BS((N_EXPERTS, LANES), fixed2)),
        scratch_shapes=[pltpu.VMEM((N_EXPERTS, LANES), F32)],
        compiler_params=_cparams(2), name="out_proj_router",
    )(att_n, hy, gw_hy, wo_bf, x, g1, sh2, sc2, norm2_w.reshape(1, d), wr_hi, wr_lo, br, tri)


ISSUE_UNROLL = 8


def _row_copy(src, src_row, dst, dst_row, sem):
    return pltpu.make_async_copy(src.at[pl.ds(src_row, 1), :], dst.at[pl.ds(dst_row, 1), :], sem)


def _dispatch_body(pend_ref, d_ref, h_ref, buf_ref, zero_ref, sem):
    n = h_ref.shape[0]

    @pl.when(pl.program_id(0) == 0)
    def _():
        zero_ref[...] = jnp.zeros_like(zero_ref)

        def tail_copy(e):
            row0 = pl.multiple_of(pend_ref[e] - MOE_BLOCK, MOE_BLOCK)
            return pltpu.make_async_copy(zero_ref, buf_ref.at[pl.ds(row0, MOE_BLOCK), :], sem)

        def non_empty(e):
            return pend_ref[e] > (pend_ref[e - 1] if e > 0 else 0)

        def spare_copy(i):
            row0 = pl.multiple_of(i * MOE_BLOCK, MOE_BLOCK)
            return pltpu.make_async_copy(zero_ref, buf_ref.at[pl.ds(row0, MOE_BLOCK), :], sem)

        first_spare = pend_ref[N_EXPERTS - 1] // MOE_BLOCK
        n_blocks = buf_ref.shape[0] // MOE_BLOCK
        for e in range(N_EXPERTS):
            @pl.when(non_empty(e))
            def _(e=e):
                tail_copy(e).start()
        lax.fori_loop(first_spare, n_blocks, lambda i, c: (spare_copy(i).start(), c)[1], 0)
        for e in range(N_EXPERTS):
            @pl.when(non_empty(e))
            def _(e=e):
                tail_copy(e).wait()
        lax.fori_loop(first_spare, n_blocks, lambda i, c: (spare_copy(i).wait(), c)[1], 0)

    def issue(t, carry):
        for j in range(TOP_K):
            _row_copy(h_ref, t, buf_ref, d_ref[j, t], sem).start()
        return carry

    lax.fori_loop(0, n, issue, 0, unroll=ISSUE_UNROLL)
    for j in range(TOP_K):
        pltpu.make_async_copy(h_ref, buf_ref.at[pl.ds(0, n), :], sem).wait()


def _dispatch(pend, dest, h2, n_rows):
    n_tok, d = h2.shape
    tm = GATHER_TILE
    grid_spec = pltpu.PrefetchScalarGridSpec(
        num_scalar_prefetch=1, grid=(n_tok // tm,),
        in_specs=[BS((TOP_K, tm), lambda i, pend: (0, i), memory_space=pltpu.SMEM),
                  BS((tm, d), lambda i, pend: (i, 0))],
        out_specs=BS(memory_space=pl.ANY),
        scratch_shapes=[pltpu.VMEM((MOE_BLOCK, d), F32), pltpu.SemaphoreType.DMA(())])
    return pl.pallas_call(_dispatch_body, out_shape=SDS((n_rows, d), F32), grid_spec=grid_spec,
                          compiler_params=_cparams(1), name="moe_dispatch")(pend, dest, h2)


def _expert_body(be_ref, nu_ref, x_ref, w1_ref, w3_ref, w2_ref, y_ref, w13_s, w2_s):
    i = pl.program_id(0)
    hidden = w2_ref.shape[1]
    active = i < nu_ref[0]
    new_expert = jnp.logical_or(i == 0, be_ref[i] != be_ref[jnp.maximum(i - 1, 0)])

    @pl.when(jnp.logical_and(active, new_expert))
    def _():
        w13_s[:, :hidden] = w1_ref[0].astype(BF16)
        w13_s[:, hidden:] = w3_ref[0].astype(BF16)
        w2_s[...] = w2_ref[0].astype(BF16)

    @pl.when(active)
    def _():
        h = _mm(x_ref[...].astype(BF16), w13_s[...])
        act = _silu(h[:, :hidden]) * h[:, hidden:]
        y_ref[...] = _mm(act.astype(BF16), w2_s[...])

    @pl.when(jnp.logical_not(active))
    def _():
        y_ref[...] = jnp.zeros_like(y_ref)


def _experts(block_expert, n_used, buf, w1, w3, w2):
    n_rows, d = buf.shape
    hidden = w2.shape[1]
    nb = n_rows // MOE_BLOCK
    by_expert = lambda i, be, nu: (be[i], 0, 0)
    grid_spec = pltpu.PrefetchScalarGridSpec(
        num_scalar_prefetch=2, grid=(nb,),
        in_specs=[BS((MOE_BLOCK, d), lambda i, be, nu: (jnp.minimum(i, nu[0] - 1), 0)),
                  BS((1, d, hidden), by_expert), BS((1, d, hidden), by_expert), BS((1, hidden, d), by_expert)],
        out_specs=BS((MOE_BLOCK, d), lambda i, be, nu: (i, 0)),
        scratch_shapes=[pltpu.VMEM((d, 2 * hidden), BF16), pltpu.VMEM((hidden, d), BF16)])
    return pl.pallas_call(_expert_body, out_shape=SDS((n_rows, d), F32), grid_spec=grid_spec,
                          compiler_params=_cparams(1), name="moe_experts")(block_expert, n_used, buf, w1, w3, w2)


def _combine_body(d_ref, y_ref, x1_ref, g2_ref, gt_ref, o_ref, rows_ref, sem):
    n = x1_ref.shape[0]

    def issue(t, carry):
        for j in range(TOP_K):
            _row_copy(y_ref, d_ref[j, t], rows_ref.at[j], t, sem).start()
        return carry

    lax.fori_loop(0, n, issue, 0, unroll=ISSUE_UNROLL)
    for j in range(TOP_K):
        pltpu.make_async_copy(y_ref.at[pl.ds(0, n), :], rows_ref.at[j], sem).wait()
    gt = gt_ref[...]
    moe = gt[:, 0:1] * rows_ref[0] + gt[:, 1:2] * rows_ref[1]
    o_ref[...] = x1_ref[...] + g2_ref[0] * moe


def _combine(dest, y, x1, g2, gates_t, seq):
    n_tok, d = x1.shape
    tm = GATHER_TILE
    return pl.pallas_call(
        _combine_body, out_shape=SDS((n_tok, d), F32), grid=(n_tok // tm,),
        in_specs=[BS((TOP_K, tm), lambda i: (0, i), memory_space=pltpu.SMEM), BS(memory_space=pl.ANY),
                  BS((tm, d), lambda i: (i, 0)), BS((1, 1, d), lambda i: (i // (seq // tm), 0, 0)),
                  BS((tm, TOP_K), lambda i: (i, 0))],
        out_specs=BS((tm, d), lambda i: (i, 0)),
        scratch_shapes=[pltpu.VMEM((TOP_K, tm, d), F32), pltpu.SemaphoreType.DMA(())],
        compiler_params=_cparams(1), name="moe_combine",
    )(dest, y, x1, g2, gates_t)


def kernel(x, c, ctx, c_ctx, ada_w, ada_b, norm1_w, w_in, q_norm_w, k_norm_w, attn_sink, hy_conv_w, hy_conv_b, hy_w1,
           hy_b1, hy_w2, hy_b2, hy_w3, hy_sin_freq, hy_skip, group_norm_w, w_out, norm2_w, router_g_w, router_g_b,
           router_e_w, router_e_b, exp_w1, exp_w3, exp_w2):
    bsz, seq, d = x.shape
    assert ada_w.shape[0] == 1, "single-layer block"
    att_w = d // 2
    hy_w = d - att_w
    kv_w = N_KV_HEADS * HEAD_DIM
    n_heads = att_w // HEAD_DIM
    n_grp = hy_w // C_GROUP
    assert 2 * seq == FFT_MAJOR * FFT_MINOR and seq % ROW_TILE == 0 and hy_w % C_GROUP == 0
    assert w_in.shape[2] == att_w + 2 * kv_w + (HYENA_ORDER + 1) * hy_w

    pad = (-(bsz + 1)) % SUBLANES
    cc = jnp.concatenate([c, c_ctx[None, :], jnp.zeros((pad, d), F32)], axis=0)
    mod = _ada(cc, ada_w[0], ada_b[0])
    lat = [mod[:bsz, k * d:(k + 1) * d].reshape(bsz, 1, d) for k in range(6)]
    sh1, sc1, g1, sh2, sc2, g2 = lat
    csh1 = mod[bsz:bsz + 1, 0:d]
    csc1 = mod[bsz:bsz + 1, d:2 * d]

    w_in_bf = w_in[0].astype(BF16)
    qnw = jnp.tile(q_norm_w[0], n_heads).reshape(1, att_w)
    knw = jnp.tile(k_norm_w[0], N_KV_HEADS).reshape(1, kv_w)
    q, k, v, u = _inproj(x, sh1, sc1, norm1_w[0], w_in_bf, qnw, knw, att_w, kv_w)
    kx, vx = _ctxproj(ctx, csh1, csc1, norm1_w[0], w_in_bf[:, att_w:att_w + 2 * kv_w], knw, kv_w)

    gw = group_norm_w[0]
    att_n = _attention(attn_sink[0], q, k, v, kx, vx, gw[:att_w].reshape(1, att_w))

    hf = _filter_mlp(seq, hy_w1[0], hy_b1[0], hy_w2[0], hy_b2[0], hy_w3[0], hy_sin_freq[0], hy_w)
    spec = _spectrum(hf, seq, hy_w)
    cw = hy_conv_w[0].reshape(3, HYENA_ORDER + 1, n_grp, C_GROUP)
    cb = hy_conv_b[0].reshape(1, HYENA_ORDER + 1, n_grp, C_GROUP)
    taps = jnp.concatenate([cw, cb], axis=0)
    prm = jnp.zeros((HYENA_ORDER, n_grp, 16, C_GROUP), F32)
    for o in range(HYENA_ORDER):
        prm = prm.at[o, :, 4:8].set(jnp.transpose(taps[:, o + 1], (1, 0, 2)))
        prm = prm.at[o, :, 8].set(hy_skip[0, o].reshape(n_grp, C_GROUP))
    prm = prm.at[0, :, 0:4].set(jnp.transpose(taps[:, 0], (1, 0, 2)))
    z = _hyena_order(u, 0, u, n_grp, prm, 0, spec, True, n_grp)
    hy = _hyena_order(z, 0, u, 2 * n_grp, prm, 1, spec, False, n_grp)

    wr = jnp.zeros((ROUTER_ROWS, d), F32)
    wr = wr.at[0:N_GROUPS].set(router_g_w[0].T).at[EXPERT_ROW0:EXPERT_ROW0 + N_EXPERTS].set(router_e_w[0].T)
    br = jnp.zeros((ROUTER_ROWS, 1), F32)
    br = br.at[0:N_GROUPS, 0].set(router_g_b[0]).at[EXPERT_ROW0:EXPERT_ROW0 + N_EXPERTS, 0].set(router_e_b[0])
    wr_hi, wr_lo = _split(wr)
    x1, h2, ids, gates, counts = _outproj_route(att_n, hy, gw[att_w:].reshape(1, hy_w), w_out[0].astype(BF16), x,
                                                g1, sh2, sc2, norm2_w[0], wr_hi, wr_lo, br)

    n_tok = bsz * seq
    cnt = counts[:, 0].astype(I32)
    padded = (cnt + MOE_BLOCK - 1) // MOE_BLOCK * MOE_BLOCK
    pend = jnp.cumsum(padded)
    pstart = pend - padded
    e_iota = jnp.arange(N_EXPERTS, dtype=I32)
    is_e = ids[0:TOP_K, :, None] == e_iota
    dest = jnp.sum(jnp.where(is_e, pstart, 0), axis=-1) + ids[TOP_K:2 * TOP_K]
    n_blocks = (n_tok * TOP_K) // MOE_BLOCK + N_EXPERTS
    block_row0 = jnp.arange(n_blocks, dtype=I32) * MOE_BLOCK
    block_expert = jnp.minimum(jnp.sum((block_row0[:, None] >= pend[None, :]).astype(I32), axis=-1), N_EXPERTS - 1)
    n_used = (pend[-1:] // MOE_BLOCK).astype(I32)

    buf = _dispatch(pend.astype(I32), dest, h2, n_blocks * MOE_BLOCK)
    y = _experts(block_expert, n_used, buf, exp_w1[0], exp_w3[0], exp_w2[0])
    out = _combine(dest, y, x1.reshape(n_tok, d), g2, gates[0:TOP_K].T, seq)
    return out.reshape(bsz, seq, d)
```

```python
import functools
import math

import ml_dtypes
import numpy as np

import jax
import jax.numpy as jnp
from jax import lax
from jax.experimental import pallas as pl
from jax.experimental.pallas import tpu as pltpu

F32, BF16, I32 = jnp.float32, jnp.bfloat16, jnp.int32
SDS = jax.ShapeDtypeStruct
BS = pl.BlockSpec

HEAD_DIM = 64
N_KV_HEADS = 2
WINDOW = 128
Q_BLOCK = 128
GRID_W = 64
ROPE_THETA = 10000.0
EPS = 1e-6
NEG_INF = -1e30
LOG2E = math.log2(math.e)
HYENA_BANDS = 16
HYENA_ORDER = 2
DECAY_TARGET = 1e-2
FAST_DECAY_PCT = 0.3
SLOW_DECAY_PCT = 1.5
N_GROUPS = 4
EXPERTS_PER_GROUP = 8
N_EXPERTS = N_GROUPS * EXPERTS_PER_GROUP
TOP_K = 2

LANES = 128
SUBLANES = 8
VMEM_LIMIT = 60 * 1024 * 1024

ROW_TILE = 512
MOE_BLOCK = 512
GATHER_TILE = 1024
FILTER_TILE = 512

FFT_MINOR = 128
FFT_MAJOR = 64
N_SLABS = FFT_MAJOR // 2 + 1


def _cparams(n_grid):
    return pltpu.CompilerParams(dimension_semantics=("arbitrary",) * n_grid, vmem_limit_bytes=VMEM_LIMIT)


def _mm(a, b):
    return jnp.dot(a, b, preferred_element_type=F32)


def _mm_nt(a, b):
    return lax.dot_general(a, b, (((1,), (1,)), ((), ())), preferred_element_type=F32)


def _split(x):
    hi = x.astype(BF16)
    lo = (x - hi.astype(F32)).astype(BF16)
    return hi, lo


def _mm_split(w_hi, w_lo, x, passes):
    if passes == 1:
        return _mm(w_hi, x.astype(BF16))
    x_hi, x_lo = _split(x)
    out = _mm(w_hi, x_hi) + _mm(w_hi, x_lo)
    if passes == 3:
        out = out + _mm(w_lo, x_hi)
    return out


def _mm3(a, b):
    a_hi, a_lo = _split(a)
    b_hi, b_lo = _split(b)
    return _mm(a_hi, b_hi) + _mm(a_hi, b_lo) + _mm(a_lo, b_hi)


def _silu(x):
    return x * (1.0 / (1.0 + jnp.exp(-x)))


def _rms(x):
    return x * lax.rsqrt(jnp.mean(x * x, axis=-1, keepdims=True) + EPS)


@functools.lru_cache(maxsize=None)
def _rope_tables(seq_len):
    pos = np.arange(seq_len)
    rows = (pos // GRID_W).astype(np.float64)
    cols = (pos % GRID_W).astype(np.float64)
    pairs = HEAD_DIM // 4
    inv_freq = ROPE_THETA ** (-np.arange(pairs, dtype=np.float64) / pairs)
    ang_r = rows[:, None] * inv_freq[None, :]
    ang_c = cols[:, None] * inv_freq[None, :]
    ang = np.concatenate([ang_r, ang_r, ang_c, ang_c], axis=1)
    sign = np.tile(np.concatenate([-np.ones(pairs), np.ones(pairs)]), 2)
    cos = np.cos(ang)
    sin = np.sin(ang) * sign[None, :]
    reps = LANES // HEAD_DIM
    return np.tile(cos, (1, reps)).astype(np.float32), np.tile(sin, (1, reps)).astype(np.float32)


@functools.lru_cache(maxsize=None)
def _block_diag_mean(width):
    h = np.arange(width) // HEAD_DIM
    return ((h[:, None] == h[None, :]).astype(np.float32) / HEAD_DIM).astype(ml_dtypes.bfloat16)


@functools.lru_cache(maxsize=None)
def _filter_features(seq_len, width):
    pos = np.arange(seq_len, dtype=np.float64)
    t = pos / seq_len
    bands = np.linspace(1e-4, HYENA_BANDS - 1, HYENA_BANDS)
    ang = (2.0 * math.pi / seq_len) * pos[:, None] * bands[None, :]
    feat = np.concatenate([t[:, None], np.cos(ang), -np.sin(ang)], axis=-1)
    out = np.zeros((seq_len, width), np.float32)
    out[:, :feat.shape[1]] = feat
    return out


@functools.lru_cache(maxsize=None)
def _decay_rates(width):
    max_decay = math.log(DECAY_TARGET) / FAST_DECAY_PCT
    min_decay = math.log(DECAY_TARGET) / SLOW_DECAY_PCT
    return np.abs(np.linspace(min_decay, max_decay, width)).astype(np.float32)


@functools.lru_cache(maxsize=None)
def _dft_tables():
    n = FFT_MAJOR * FFT_MINOR
    half = FFT_MAJOR // 2
    n2 = np.arange(half)
    a = np.zeros((FFT_MAJOR, half))
    a[:half] = np.cos(2 * np.pi * np.arange(half)[:, None] * n2[None, :] / FFT_MAJOR)
    a[half] = (-1.0) ** n2
    k2 = np.arange(1, half)
    a[half + 1:] = -np.sin(2 * np.pi * k2[:, None] * n2[None, :] / FFT_MAJOR)
    d = np.zeros((half, FFT_MAJOR))
    d[:, 0] = 1.0 / n
    d[:, 1:half] = 2 * np.cos(2 * np.pi * n2[:, None] * k2[None, :] / FFT_MAJOR) / n
    d[:, half] = (-1.0) ** n2 / n
    d[:, half + 1:] = -2 * np.sin(2 * np.pi * n2[:, None] * k2[None, :] / FFT_MAJOR) / n
    eye = np.eye(SUBLANES)
    ak = np.kron(a, eye)
    dk = np.kron(d, eye)
    s = np.arange(N_SLABS)[:, None, None]
    k1 = np.arange(FFT_MINOR)[None, :, None]
    n1 = np.arange(FFT_MINOR)[None, None, :]
    g = np.exp(-2j * np.pi * n1 * (FFT_MAJOR * k1 + s) / n)
    mf = np.block([[g.real, -g.imag], [g.imag, g.real]])
    gi = np.conj(g).transpose(0, 2, 1)
    mi = np.block([[gi.real, -gi.imag], [gi.imag, gi.real]])
    return {name: m.astype(np.float32) for name, m in (("ak", ak), ("dk", dk), ("mf", mf), ("mi", mi))}


def _dft_hi_lo(name):
    return _split(jnp.asarray(_dft_tables()[name]))


def _ada_body(c_ref, w_ref, b_ref, o_ref):
    o_ref[...] = _mm3(_silu(c_ref[...]), w_ref[...]) + b_ref[...]


def _ada(cc, ada_w, ada_b):
    rows, d = cc.shape
    n = ada_w.shape[1]
    tn = 1536
    return pl.pallas_call(
        _ada_body, out_shape=SDS((rows, n), F32), grid=(n // tn,),
        in_specs=[BS((rows, d), lambda j: (0, 0)), BS((d, tn), lambda j: (0, j)), BS((1, tn), lambda j: (0, j))],
        out_specs=BS((rows, tn), lambda j: (0, j)), compiler_params=_cparams(1), name="ada_mod",
    )(cc, ada_w, ada_b.reshape(1, n))


def _head_rms(t, bd):
    hi, lo = _split(t * t)
    return t * lax.rsqrt(_mm(hi, bd) + _mm(lo, bd) + EPS)


def _rope(t, cos, sin):
    n = t.shape[1]
    quarter = HEAD_DIM // 4
    lane = lax.broadcasted_iota(I32, t.shape, 1)
    up = pltpu.roll(t, n - quarter, axis=1)
    dn = pltpu.roll(t, quarter, axis=1)
    partner = jnp.where((lane & (2 * quarter - 1)) < quarter, up, dn)
    return t * cos + partner * sin


def _inproj_body(x_ref, sh_ref, sc_ref, nw_ref, w_ref, qnw_ref, knw_ref, cos_ref, sin_ref, bdq_ref, bdk_ref,
                 q_ref, k_ref, v_ref, u_ref, *, att_w, kv_w):
    h = _rms(x_ref[0]) * nw_ref[...]
    h = h * (1.0 + sc_ref[0]) + sh_ref[0]
    proj = _mm(h.astype(BF16), w_ref[...])
    cos, sin = cos_ref[...], sin_ref[...]
    reps = att_w // LANES
    cos_q = jnp.concatenate([cos] * reps, axis=1)
    sin_q = jnp.concatenate([sin] * reps, axis=1)
    q = _head_rms(proj[:, :att_w], bdq_ref[...]) * qnw_ref[...]
    q_ref[0] = (_rope(q, cos_q, sin_q) * (HEAD_DIM ** -0.5 * LOG2E)).astype(BF16)
    k = _head_rms(proj[:, att_w:att_w + kv_w], bdk_ref[...]) * knw_ref[...]
    k_ref[0] = _rope(k, cos, sin).astype(BF16)
    v_ref[0] = proj[:, att_w + kv_w:att_w + 2 * kv_w].astype(BF16)
    hy_off = att_w + 2 * kv_w
    for j in range(u_ref.shape[1]):
        u_ref[0, j] = proj[:, hy_off + LANES * j:hy_off + LANES * (j + 1)]


def _inproj(x, sh, sc, norm_w, w_bf, qnw, knw, att_w, kv_w):
    bsz, seq, d = x.shape
    tm = ROW_TILE
    hy_tiles = (w_bf.shape[1] - att_w - 2 * kv_w) // LANES
    cos, sin = _rope_tables(seq)
    row = lambda b, i: (b, i, 0)
    fixed2 = lambda b, i: (0, 0)
    per_b = lambda b, i: (b, 0, 0)
    return pl.pallas_call(
        functools.partial(_inproj_body, att_w=att_w, kv_w=kv_w),
        out_shape=(SDS((bsz, seq, att_w), BF16), SDS((bsz, seq, kv_w), BF16), SDS((bsz, seq, kv_w), BF16),
                   SDS((bsz, hy_tiles, seq, LANES), F32)),
        grid=(bsz, seq // tm),
        in_specs=[BS((1, tm, d), row), BS((1, 1, d), per_b), BS((1, 1, d), per_b), BS((1, d), fixed2),
                  BS(w_bf.shape, fixed2), BS((1, att_w), fixed2), BS((1, kv_w), fixed2),
                  BS((tm, LANES), lambda b, i: (i, 0)), BS((tm, LANES), lambda b, i: (i, 0)),
                  BS((att_w, att_w), fixed2), BS((kv_w, kv_w), fixed2)],
        out_specs=(BS((1, tm, att_w), row), BS((1, tm, kv_w), row), BS((1, tm, kv_w), row),
                   BS((1, hy_tiles, tm, LANES), lambda b, i: (b, 0, i, 0))),
        compiler_params=_cparams(2), name="in_proj",
    )(x, sh, sc, norm_w.reshape(1, d), w_bf, qnw, knw, jnp.asarray(cos), jnp.asarray(sin),
      jnp.asarray(_block_diag_mean(att_w)), jnp.asarray(_block_diag_mean(kv_w)))


def _ctxproj_body(x_ref, sh_ref, sc_ref, nw_ref, w_ref, knw_ref, bdk_ref, k_ref, v_ref, *, kv_w):
    h = _rms(x_ref[0]) * nw_ref[...]
    h = h * (1.0 + sc_ref[...]) + sh_ref[...]
    proj = _mm(h.astype(BF16), w_ref[...])
    k_ref[0] = (_head_rms(proj[:, :kv_w], bdk_ref[...]) * knw_ref[...]).astype(BF16)
    v_ref[0] = proj[:, kv_w:].astype(BF16)


def _ctxproj(ctx, sh, sc, norm_w, w_kv_bf, knw, kv_w):
    bsz, n_ctx, d = ctx.shape
    fixed2 = lambda b: (0, 0)
    row = lambda b: (b, 0, 0)
    return pl.pallas_call(
        functools.partial(_ctxproj_body, kv_w=kv_w),
        out_shape=(SDS((bsz, n_ctx, kv_w), BF16), SDS((bsz, n_ctx, kv_w), BF16)), grid=(bsz,),
        in_specs=[BS((1, n_ctx, d), row), BS((1, d), fixed2), BS((1, d), fixed2), BS((1, d), fixed2),
                  BS(w_kv_bf.shape, fixed2), BS((1, kv_w), fixed2), BS((kv_w, kv_w), fixed2)],
        out_specs=(BS((1, n_ctx, kv_w), row), BS((1, n_ctx, kv_w), row)),
        compiler_params=_cparams(1), name="ctx_proj",
    )(ctx, sh, sc, norm_w.reshape(1, d), w_kv_bf, knw, jnp.asarray(_block_diag_mean(kv_w)))


def _attn_body(sink_ref, q_ref, kp_ref, kc_ref, kn_ref, vp_ref, vc_ref, vn_ref, kx_ref, vx_ref, gw_ref, bias_ref,
               o_ref):
    n_ctx = kx_ref.shape[1]
    n_heads = q_ref.shape[2] // HEAD_DIM
    group = n_heads // N_KV_HEADS
    rows = group * Q_BLOCK
    bias = bias_ref[0]
    head_of_row = lax.broadcasted_iota(I32, (rows, 1), 0) // Q_BLOCK
    outs = []
    for g in range(N_KV_HEADS):
        sl = slice(g * HEAD_DIM, (g + 1) * HEAD_DIM)
        kb = jnp.concatenate([kx_ref[0][:, sl], kp_ref[0][:, sl], kc_ref[0][:, sl], kn_ref[0][:, sl]], axis=0)
        vb = jnp.concatenate([vx_ref[0][:, sl], vp_ref[0][:, sl], vc_ref[0][:, sl], vn_ref[0][:, sl]], axis=0)
        heads = range(g * group, (g + 1) * group)
        q4 = jnp.concatenate([q_ref[0][:, h * HEAD_DIM:(h + 1) * HEAD_DIM] for h in heads], axis=0)
        sink = jnp.zeros((rows, 1), F32)
        for hh, h in enumerate(heads):
            sink = jnp.where(head_of_row == hh, sink_ref[h] * LOG2E, sink)
        s = _mm_nt(q4, kb)
        s_ctx = s[:, :n_ctx]
        s_band = s[:, n_ctx:] + bias
        m = jnp.maximum(jnp.maximum(jnp.max(s_ctx, axis=-1, keepdims=True), jnp.max(s_band, axis=-1, keepdims=True)),
                        sink)
        p_ctx = jnp.exp2((s_ctx - m).astype(BF16))
        p_band = jnp.exp2((s_band - m).astype(BF16))
        denom = (jnp.sum(p_ctx.astype(F32), axis=-1, keepdims=True) + jnp.sum(p_band.astype(F32), axis=-1, keepdims=True)
                 + jnp.exp2(sink - m))
        o4 = (_mm(p_ctx, vb[:n_ctx]) + _mm(p_band, vb[n_ctx:])) / denom
        for hh in range(group):
            outs.append(o4[hh * Q_BLOCK:(hh + 1) * Q_BLOCK])
    att = jnp.concatenate(outs, axis=1)
    o_ref[0] = (_rms(att) * gw_ref[...]).astype(BF16)


@functools.lru_cache(maxsize=None)
def _band_bias(rows):
    r = (np.arange(rows) % Q_BLOCK)[:, None]
    j = np.arange(3 * Q_BLOCK)[None, :]
    in_band = (j >= r) & (j <= r + 2 * WINDOW)
    cases = (in_band, in_band & (j >= Q_BLOCK), in_band & (j < 2 * Q_BLOCK))
    return np.stack([np.where(c, 0.0, NEG_INF) for c in cases]).astype(np.float32)


def _attention(sink, q, k, v, kx, vx, gw):
    bsz, seq, att_w = q.shape
    kv_w = k.shape[2]
    n_ctx = kx.shape[1]
    nb = seq // Q_BLOCK
    cur = lambda b, i: (b, i, 0)
    prev = lambda b, i: (b, jnp.maximum(i - 1, 0), 0)
    nxt = lambda b, i: (b, jnp.minimum(i + 1, nb - 1), 0)
    per_b = lambda b, i: (b, 0, 0)
    kvb = (1, Q_BLOCK, kv_w)
    rows = (att_w // HEAD_DIM // N_KV_HEADS) * Q_BLOCK
    assert nb >= 2
    edge_case = lambda b, i: (jnp.where(i == 0, 1, jnp.where(i == nb - 1, 2, 0)), 0, 0)
    return pl.pallas_call(
        _attn_body, out_shape=SDS((bsz, seq, att_w), BF16), grid=(bsz, nb),
        in_specs=[BS(memory_space=pltpu.SMEM), BS((1, Q_BLOCK, att_w), cur),
                  BS(kvb, prev), BS(kvb, cur), BS(kvb, nxt), BS(kvb, prev), BS(kvb, cur), BS(kvb, nxt),
                  BS((1, n_ctx, kv_w), per_b), BS((1, n_ctx, kv_w), per_b), BS((1, att_w), lambda b, i: (0, 0)),
                  BS((1, rows, 3 * Q_BLOCK), edge_case)],
        out_specs=BS((1, Q_BLOCK, att_w), cur), compiler_params=_cparams(2), name="window_attn",
    )(sink, q, k, k, k, v, v, v, kx, vx, gw, jnp.asarray(_band_bias(rows)))


def _filter_body(f_ref, w1h, w1l, b1, w2h, w2l, b2, w3h, w3l, sf_ref, dl_ref, o_ref, *, hy_w):
    def mm_w(a, wh, wl):
        a_hi, a_lo = _split(a)
        return _mm(a_hi, wh[...]) + _mm(a_hi, wl[...]) + _mm(a_lo, wh[...])

    f = f_ref[...]
    h = jnp.sin(sf_ref[0:1, :] * (mm_w(f, w1h, w1l) + b1[...]))
    h = jnp.sin(sf_ref[1:2, :] * (mm_w(h, w2h, w2l) + b2[...]))
    h = mm_w(h, w3h, w3l)
    h = h * jnp.exp(-f[:, 0:1] * dl_ref[...])
    tl = f.shape[0]
    row = lax.broadcasted_iota(I32, h.shape, 0) + pl.program_id(0) * tl
    col = lax.broadcasted_iota(I32, h.shape, 1)
    is_bwd = ((col // hy_w) & 1) == 1
    h = jnp.where((row == 0) & is_bwd, 0.0, h)
    for j in range(o_ref.shape[0]):
        o_ref[j] = h[:, LANES * j:LANES * (j + 1)]


def _filter_mlp(seq, w1, b1, w2, b2, w3, sin_freq, hy_w):
    hid = w1.shape[1]
    fw = 64
    feat = jnp.asarray(_filter_features(seq, fw))
    w1p = jnp.zeros((fw, hid), F32).at[:w1.shape[0]].set(w1)
    n_out = w3.shape[1]
    delta = jnp.asarray(np.tile(_decay_rates(hy_w), n_out // hy_w)).reshape(1, n_out)
    tl = FILTER_TILE
    fixed = lambda i: (0, 0)
    ops = []
    for w in (w1p, w2, w3):
        ops.extend(_split(w))
    return pl.pallas_call(
        functools.partial(_filter_body, hy_w=hy_w),
        out_shape=SDS((n_out // LANES, seq, LANES), F32), grid=(seq // tl,),
        in_specs=[BS((tl, fw), lambda i: (i, 0)), BS((fw, hid), fixed), BS((fw, hid), fixed), BS((1, hid), fixed),
                  BS((hid, hid), fixed), BS((hid, hid), fixed), BS((1, hid), fixed),
                  BS((hid, n_out), fixed), BS((hid, n_out), fixed), BS((2, hid), fixed), BS((1, n_out), fixed)],
        out_specs=BS((n_out // LANES, tl, LANES), lambda i: (0, i, 0)),
        compiler_params=_cparams(1), name="hyena_filter_mlp",
    )(feat, ops[0], ops[1], b1.reshape(1, hid), ops[2], ops[3], b2.reshape(1, hid), ops[4], ops[5], sin_freq, delta)


C_TILES = 2
C_GROUP = C_TILES * LANES


def _load_cat(ref, row0, n_rows):
    return jnp.concatenate([ref[c, pl.ds(row0, n_rows), :] for c in range(C_TILES)], axis=1)


def _store_cat(ref, row0, n_rows, val):
    for c in range(C_TILES):
        ref[c, pl.ds(row0, n_rows), :] = val[:, c * LANES:(c + 1) * LANES]


def _fwd_major(src, p_ref, ak, passes):
    half = FFT_MAJOR // 2

    def body(g, carry):
        r0 = pl.multiple_of(g * SUBLANES, SUBLANES)
        st = jnp.concatenate([_load_cat(src, n2 * FFT_MINOR + r0, SUBLANES) for n2 in range(half)], axis=0)
        out = _mm_split(ak[0][...], ak[1][...] if passes == 3 else None, st, passes)
        for p in range(FFT_MAJOR):
            _store_cat(p_ref, p * FFT_MINOR + r0, SUBLANES, out[p * SUBLANES:(p + 1) * SUBLANES])
        return carry

    lax.fori_loop(0, FFT_MINOR // SUBLANES, body, 0, unroll=4)


def _inv_major(p_ref, dst, dk, passes):
    half = FFT_MAJOR // 2

    def body(g, carry):
        r0 = pl.multiple_of(g * SUBLANES, SUBLANES)
        st = jnp.concatenate([_load_cat(p_ref, p * FFT_MINOR + r0, SUBLANES) for p in range(FFT_MAJOR)], axis=0)
        out = _mm_split(dk[0][...], dk[1][...] if passes == 3 else None, st, passes)
        for n2 in range(half):
            _store_cat(dst, n2 * FFT_MINOR + r0, SUBLANES, out[n2 * SUBLANES:(n2 + 1) * SUBLANES])
        return carry

    lax.fori_loop(0, FFT_MINOR // SUBLANES, body, 0, unroll=4)


def _slab_spectrum(p_ref, mf, s, passes, real_slot=None):
    m = FFT_MINOR
    if real_slot is None:
        st = jnp.concatenate([_load_cat(p_ref, s * m, m), _load_cat(p_ref, (FFT_MAJOR // 2 + s) * m, m)], axis=0)
        x = _mm_split(mf[0][s], mf[1][s] if passes == 3 else None, st, passes)
    else:
        st = _load_cat(p_ref, real_slot * m, m)
        x = _mm_split(mf[0][s, :, 0:m], mf[1][s, :, 0:m] if passes == 3 else None, st, passes)
    return x[:m], x[m:]


def _spectrum_body(f_ref, b_ref, akh, akl, mfh, mfl, o_ref, p_ref):
    half = FFT_MAJOR // 2
    ak, mf = (akh, akl), (mfh, mfl)
    for is_bwd, src in ((False, f_ref), (True, b_ref)):
        _fwd_major(src, p_ref, ak, 3)

        def put(s, re, im, is_bwd=is_bwd):
            if is_bwd:
                o_ref[0, 0, s, 0] = o_ref[0, 0, s, 0] + re
                o_ref[0, 0, s, 1] = o_ref[0, 0, s, 1] - im
            else:
                o_ref[0, 0, s, 0] = re
                o_ref[0, 0, s, 1] = im

        put(0, *_slab_spectrum(p_ref, mf, 0, 3, real_slot=0))
        put(half, *_slab_spectrum(p_ref, mf, half, 3, real_slot=half))

        def body(s, carry):
            put(s, *_slab_spectrum(p_ref, mf, s, 3))
            return carry

        lax.fori_loop(1, half, body, 0, unroll=2)


def _spectrum(hf, seq, hy_w):
    ak, mf = _dft_hi_lo("ak"), _dft_hi_lo("mf")
    n_grp = hy_w // C_GROUP
    fixed2 = lambda o, c: (0, 0)
    fixed3 = lambda o, c: (0, 0, 0)
    one = pl.Buffered(1)
    return pl.pallas_call(
        _spectrum_body,
        out_shape=SDS((HYENA_ORDER, n_grp, N_SLABS, 2, FFT_MINOR, C_GROUP), F32), grid=(HYENA_ORDER, n_grp),
        in_specs=[BS((C_TILES, seq, LANES), lambda o, c: (o * 2 * n_grp + c, 0, 0), pipeline_mode=one),
                  BS((C_TILES, seq, LANES), lambda o, c: (o * 2 * n_grp + n_grp + c, 0, 0), pipeline_mode=one),
                  BS(ak[0].shape, fixed2, pipeline_mode=one), BS(ak[0].shape, fixed2, pipeline_mode=one),
                  BS(mf[0].shape, fixed3, pipeline_mode=one), BS(mf[0].shape, fixed3, pipeline_mode=one)],
        out_specs=BS((1, 1, N_SLABS, 2, FFT_MINOR, C_GROUP), lambda o, c: (o, c, 0, 0, 0, 0)),
        scratch_shapes=[pltpu.VMEM((C_TILES, FFT_MAJOR * FFT_MINOR, LANES), F32)],
        compiler_params=_cparams(2), name="hyena_spectrum",
    )(hf, hf, ak[0], ak[1], mf[0], mf[1])


CONV_CHUNK = 128
CONV_PASSES = 1


def _sconv_chunk(ref, c, r0, n_rows, seq, prm_ref, base):
    cl = slice(c * LANES, (c + 1) * LANES)
    cur = ref[c, pl.ds(r0, n_rows), :]
    row = lax.broadcasted_iota(I32, cur.shape, 0)
    before = ref[c, pl.ds(jnp.maximum(r0 - 1, 0), 1), :] * (r0 > 0).astype(F32)
    after = ref[c, pl.ds(jnp.minimum(r0 + n_rows, seq - 1), 1), :] * (r0 + n_rows < seq).astype(F32)
    prev = jnp.where(row == 0, before, pltpu.roll(cur, 1, axis=0))
    nxt = jnp.where(row == n_rows - 1, after, pltpu.roll(cur, n_rows - 1, axis=0))
    w = lambda k: prm_ref[0, 0, base + k:base + k + 1, cl]
    return w(3) + w(0) * prev + w(1) * cur + w(2) * nxt


def _conv_body(z_ref, x_ref, prm_ref, akh, dkh, mfh, mih, h_ref, o_ref, zs_ref, p_ref, *, conv_z):
    seq = z_ref.shape[2]
    half = FFT_MAJOR // 2
    m = FFT_MINOR
    zv, xv = z_ref.at[0], x_ref.at[0]
    ak, dk, mf, mi = (akh, None), (dkh, None), (mfh, None), (mih, None)
    n_chunks = seq // CONV_CHUNK

    if conv_z:
        def prep(i, carry):
            r0 = pl.multiple_of(i * CONV_CHUNK, CONV_CHUNK)
            for c in range(C_TILES):
                zs_ref[c, pl.ds(r0, CONV_CHUNK), :] = _sconv_chunk(zv, c, r0, CONV_CHUNK, seq, prm_ref, 0)
            return carry

        lax.fori_loop(0, n_chunks, prep, 0)
        src = zs_ref
    else:
        src = zv
    _fwd_major(src, p_ref, ak, CONV_PASSES)

    def slab(s, real_slot=None):
        xr, xi = _slab_spectrum(p_ref, mf, s, CONV_PASSES, real_slot)
        hr, hi = h_ref[0, 0, s, 0], h_ref[0, 0, s, 1]
        y = jnp.concatenate([xr * hr - xi * hi, xr * hi + xi * hr], axis=0)
        if real_slot is None:
            out = _mm_split(mi[0][s], None, y, CONV_PASSES)
            _store_cat(p_ref, s * m, m, out[:m])
            _store_cat(p_ref, (half + s) * m, m, out[m:])
        else:
            out = _mm_split(mi[0][s, 0:m, :], None, y, CONV_PASSES)
            _store_cat(p_ref, real_slot * m, m, out)

    slab(0, real_slot=0)
    slab(half, real_slot=half)

    def slab_loop(s, carry):
        slab(s)
        return carry

    lax.fori_loop(1, half, slab_loop, 0, unroll=8)
    _inv_major(p_ref, zs_ref, dk, CONV_PASSES)

    def fin(i, carry):
        r0 = pl.multiple_of(i * CONV_CHUNK, CONV_CHUNK)
        for c in range(C_TILES):
            cl = slice(c * LANES, (c + 1) * LANES)
            gate = _sconv_chunk(xv, c, r0, CONV_CHUNK, seq, prm_ref, 4)
            if conv_z:
                zval = _sconv_chunk(zv, c, r0, CONV_CHUNK, seq, prm_ref, 0)
            else:
                zval = zv[c, pl.ds(r0, CONV_CHUNK), :]
            skip = prm_ref[0, 0, 8:9, cl]
            o_ref[0, c, pl.ds(r0, CONV_CHUNK), :] = gate * (zs_ref[c, pl.ds(r0, CONV_CHUNK), :] + skip * zval)
        return carry

    lax.fori_loop(0, n_chunks, fin, 0, unroll=2)


def _hyena_order(z, z_grp0, x, x_grp0, prm, order, spec, conv_z, n_grp):
    bsz, _, seq, _ = z.shape
    ak, dk, mf, mi = (_dft_hi_lo(n)[0] for n in ("ak", "dk", "mf", "mi"))
    one = pl.Buffered(1)
    fixed2 = lambda c, b: (0, 0)
    fixed3 = lambda c, b: (0, 0, 0)
    blk = (1, C_TILES, seq, LANES)
    return pl.pallas_call(
        functools.partial(_conv_body, conv_z=conv_z),
        out_shape=SDS((bsz, n_grp * C_TILES, seq, LANES), F32), grid=(n_grp, bsz),
        in_specs=[BS(blk, lambda c, b: (b, z_grp0 + c, 0, 0)),
                  BS(blk, lambda c, b: (b, x_grp0 + c, 0, 0)),
                  BS((1, 1, 16, C_GROUP), lambda c, b: (order, c, 0, 0)),
                  BS(ak.shape, fixed2, pipeline_mode=one), BS(dk.shape, fixed2, pipeline_mode=one),
                  BS(mf.shape, fixed3, pipeline_mode=one), BS(mi.shape, fixed3, pipeline_mode=one),
                  BS((1, 1, N_SLABS, 2, FFT_MINOR, C_GROUP), lambda c, b: (order, c, 0, 0, 0, 0), pipeline_mode=one)],
        out_specs=BS(blk, lambda c, b: (b, c, 0, 0)),
        scratch_shapes=[pltpu.VMEM((C_TILES, seq, LANES), F32),
                        pltpu.VMEM((C_TILES, FFT_MAJOR * FFT_MINOR, LANES), F32)],
        compiler_params=_cparams(2), name=f"hyena_conv{order}",
    )(z, x, prm, ak, dk, mf, mi, spec)


ROUTER_ROWS = 48
EXPERT_ROW0 = 8


def _outproj_body(att_ref, hy_ref, gwh_ref, wo_ref, x_ref, g1_ref, sh2_ref, sc2_ref, n2w_ref, wrh_ref, wrl_ref,
                  br_ref, tri_ref, x1_ref, h2_ref, ids_ref, gates_ref, cnt_ref, carry_ref):
    @pl.when((pl.program_id(0) == 0) & (pl.program_id(1) == 0))
    def _():
        carry_ref[...] = jnp.zeros_like(carry_ref)

    hy = jnp.concatenate([hy_ref[0, j] for j in range(hy_ref.shape[1])], axis=1)
    hyn = _rms(hy) * gwh_ref[...]
    mix = _mm(jnp.concatenate([att_ref[0], hyn.astype(BF16)], axis=1), wo_ref[...])
    x1 = x_ref[0] + g1_ref[0] * mix
    x1_ref[0] = x1
    h2 = _rms(x1) * n2w_ref[...]
    h2 = h2 * (1.0 + sc2_ref[0]) + sh2_ref[0]
    h2_ref[...] = h2

    h_hi, h_lo = _split(h2)
    lg = _mm_nt(wrh_ref[...], h_hi) + _mm_nt(wrh_ref[...], h_lo) + _mm_nt(wrl_ref[...], h_hi) + br_ref[...]
    tm = lg.shape[1]
    gl = lg[0:N_GROUPS]
    el = lg[EXPERT_ROW0:EXPERT_ROW0 + N_EXPERTS]
    gmax = jnp.max(gl, axis=0, keepdims=True)
    rg = lax.broadcasted_iota(I32, gl.shape, 0).astype(F32)
    g_idx = jnp.min(jnp.where(gl == gmax, rg, float(N_GROUPS)), axis=0, keepdims=True)
    g_val = 1.0 / jnp.sum(jnp.exp(gl - gmax), axis=0, keepdims=True)
    re_i = lax.broadcasted_iota(I32, el.shape, 0)
    re = re_i.astype(F32)
    in_group = (re_i // EXPERTS_PER_GROUP).astype(F32) == g_idx
    elm = jnp.where(in_group, el, NEG_INF)
    m1 = jnp.max(elm, axis=0, keepdims=True)
    i1 = jnp.min(jnp.where(elm == m1, re, float(N_EXPERTS)), axis=0, keepdims=True)
    elm2 = jnp.where(re == i1, NEG_INF, elm)
    m2 = jnp.max(elm2, axis=0, keepdims=True)
    i2 = jnp.min(jnp.where(elm2 == m2, re, float(N_EXPERTS)), axis=0, keepdims=True)
    e2 = jnp.exp(m2 - m1)
    inv = g_val / (1.0 + e2)

    oh1 = (re == i1).astype(F32)
    oh2 = (re == i2).astype(F32)
    oh = oh1 + oh2
    base = _mm(oh.astype(BF16), tri_ref[...]) + carry_ref[:, 0:1]
    rank1 = jnp.sum(oh1 * base, axis=0, keepdims=True)
    rank2 = jnp.sum(oh2 * base, axis=0, keepdims=True)
    carry_ref[...] = carry_ref[...] + jnp.sum(oh, axis=1, keepdims=True)
    cnt_ref[...] = carry_ref[...]

    zi = jnp.zeros((1, tm), I32)
    for k, val in enumerate((i1.astype(I32), i2.astype(I32), rank1.astype(I32), rank2.astype(I32), zi, zi, zi, zi)):
        ids_ref[k:k + 1, :] = val
    zf = jnp.zeros((1, tm), F32)
    for k, val in enumerate((inv, inv * e2, zf, zf, zf, zf, zf, zf)):
        gates_ref[k:k + 1, :] = val


def _outproj_route(att_n, hy, gw_hy, wo_bf, x, g1, sh2, sc2, norm2_w, wr_hi, wr_lo, br):
    bsz, seq, d = x.shape
    tm = ROW_TILE
    n_tok = bsz * seq
    att_w = att_n.shape[2]
    hy_tiles = hy.shape[1]
    tri = jnp.asarray(np.triu(np.ones((tm, tm), np.float32), k=1).astype(ml_dtypes.bfloat16))
    row = lambda b, i: (b, i, 0)
    per_b = lambda b, i: (b, 0, 0)
    fixed2 = lambda b, i: (0, 0)
    tok = lambda b, i: (0, b * (seq // tm) + i)
    return pl.pallas_call(
        _outproj_body,
        out_shape=(SDS((bsz, seq, d), F32), SDS((n_tok, d), F32), SDS((SUBLANES, n_tok), I32),
                   SDS((SUBLANES, n_tok), F32), SDS((N_EXPERTS, LANES), F32)),
        grid=(bsz, seq // tm),
        in_specs=[BS((1, tm, att_w), row), BS((1, hy_tiles, tm, LANES), lambda b, i: (b, 0, i, 0)),
                  BS((1, hy_tiles * LANES), fixed2), BS(wo_bf.shape, fixed2), BS((1, tm, d), row),
                  BS((1, 1, d), per_b), BS((1, 1, d), per_b), BS((1, 1, d), per_b), BS((1, d), fixed2),
                  BS((ROUTER_ROWS, d), fixed2), BS((ROUTER_ROWS, d), fixed2), BS((ROUTER_ROWS, 1), fixed2),
                  BS((tm, tm), fixed2)],
        out_specs=(BS((1, tm, d), row), BS((tm, d), lambda b, i: (b * (seq // tm) + i, 0)),
                   BS((SUBLANES, tm), tok), BS((SUBLANES, tm), tok), BS((N_EXPERTS, LANES), fixed2)),
        scratch_shapes=[pltpu.VMEM((N_EXPERTS, LANES), F32)],
        compiler_params=_cparams(2), name="out_proj_router",
    )(att_n, hy, gw_hy, wo_bf, x, g1, sh2, sc2, norm2_w.reshape(1, d), wr_hi, wr_lo, br, tri)


ISSUE_UNROLL = 8


def _row_copy(src, src_row, dst, dst_row, sem):
    return pltpu.make_async_copy(src.at[pl.ds(src_row, 1), :], dst.at[pl.ds(dst_row, 1), :], sem)


def _dispatch_body(pend_ref, d_ref, h_ref, buf_ref, zero_ref, sem):
    n = h_ref.shape[0]

    @pl.when(pl.program_id(0) == 0)
    def _():
        zero_ref[...] = jnp.zeros_like(zero_ref)

        def tail_copy(e):
            row0 = pl.multiple_of(pend_ref[e] - MOE_BLOCK, MOE_BLOCK)
            return pltpu.make_async_copy(zero_ref, buf_ref.at[pl.ds(row0, MOE_BLOCK), :], sem)

        def non_empty(e):
            return pend_ref[e] > (pend_ref[e - 1] if e > 0 else 0)

        def spare_copy(i):
            row0 = pl.multiple_of(i * MOE_BLOCK, MOE_BLOCK)
            return pltpu.make_async_copy(zero_ref, buf_ref.at[pl.ds(row0, MOE_BLOCK), :], sem)

        first_spare = pend_ref[N_EXPERTS - 1] // MOE_BLOCK
        n_blocks = buf_ref.shape[0] // MOE_BLOCK
        for e in range(N_EXPERTS):
            @pl.when(non_empty(e))
            def _(e=e):
                tail_copy(e).start()
        lax.fori_loop(first_spare, n_blocks, lambda i, c: (spare_copy(i).start(), c)[1], 0)
        for e in range(N_EXPERTS):
            @pl.when(non_empty(e))
            def _(e=e):
                tail_copy(e).wait()
        lax.fori_loop(first_spare, n_blocks, lambda i, c: (spare_copy(i).wait(), c)[1], 0)

    def issue(t, carry):
        for j in range(TOP_K):
            _row_copy(h_ref, t, buf_ref, d_ref[j, t], sem).start()
        return carry

    lax.fori_loop(0, n, issue, 0, unroll=ISSUE_UNROLL)
    for j in range(TOP_K):
        pltpu.make_async_copy(h_ref, buf_ref.at[pl.ds(0, n), :], sem).wait()


def _dispatch(pend, dest, h2, n_rows):
    n_tok, d = h2.shape
    tm = GATHER_TILE
    grid_spec = pltpu.PrefetchScalarGridSpec(
        num_scalar_prefetch=1, grid=(n_tok // tm,),
        in_specs=[BS((TOP_K, tm), lambda i, pend: (0, i), memory_space=pltpu.SMEM),
                  BS((tm, d), lambda i, pend: (i, 0))],
        out_specs=BS(memory_space=pl.ANY),
        scratch_shapes=[pltpu.VMEM((MOE_BLOCK, d), F32), pltpu.SemaphoreType.DMA(())])
    return pl.pallas_call(_dispatch_body, out_shape=SDS((n_rows, d), F32), grid_spec=grid_spec,
                          compiler_params=_cparams(1), name="moe_dispatch")(pend, dest, h2)


def _expert_body(be_ref, nu_ref, x_ref, w1_ref, w3_ref, w2_ref, y_ref, w13_s, w2_s):
    i = pl.program_id(0)
    hidden = w2_ref.shape[1]
    active = i < nu_ref[0]
    new_expert = jnp.logical_or(i == 0, be_ref[i] != be_ref[jnp.maximum(i - 1, 0)])

    @pl.when(jnp.logical_and(active, new_expert))
    def _():
        w13_s[:, :hidden] = w1_ref[0].astype(BF16)
        w13_s[:, hidden:] = w3_ref[0].astype(BF16)
        w2_s[...] = w2_ref[0].astype(BF16)

    @pl.when(active)
    def _():
        h = _mm(x_ref[...].astype(BF16), w13_s[...])
        act = _silu(h[:, :hidden]) * h[:, hidden:]
        y_ref[...] = _mm(act.astype(BF16), w2_s[...])

    @pl.when(jnp.logical_not(active))
    def _():
        y_ref[...] = jnp.zeros_like(y_ref)


def _experts(block_expert, n_used, buf, w1, w3, w2):
    n_rows, d = buf.shape
    hidden = w2.shape[1]
    nb = n_rows // MOE_BLOCK
    by_expert = lambda i, be, nu: (be[i], 0, 0)
    grid_spec = pltpu.PrefetchScalarGridSpec(
        num_scalar_prefetch=2, grid=(nb,),
        in_specs=[BS((MOE_BLOCK, d), lambda i, be, nu: (jnp.minimum(i, nu[0] - 1), 0)),
                  BS((1, d, hidden), by_expert), BS((1, d, hidden), by_expert), BS((1, hidden, d), by_expert)],
        out_specs=BS((MOE_BLOCK, d), lambda i, be, nu: (i, 0)),
        scratch_shapes=[pltpu.VMEM((d, 2 * hidden), BF16), pltpu.VMEM((hidden, d), BF16)])
    return pl.pallas_call(_expert_body, out_shape=SDS((n_rows, d), F32), grid_spec=grid_spec,
                          compiler_params=_cparams(1), name="moe_experts")(block_expert, n_used, buf, w1, w3, w2)


def _combine_body(d_ref, y_ref, x1_ref, g2_ref, gt_ref, o_ref, rows_ref, sem):
    n = x1_ref.shape[0]

    def issue(t, carry):
        for j in range(TOP_K):
            _row_copy(y_ref, d_ref[j, t], rows_ref.at[j], t, sem).start()
        return carry

    lax.fori_loop(0, n, issue, 0, unroll=ISSUE_UNROLL)
    for j in range(TOP_K):
        pltpu.make_async_copy(y_ref.at[pl.ds(0, n), :], rows_ref.at[j], sem).wait()
    gt = gt_ref[...]
    moe = gt[:, 0:1] * rows_ref[0] + gt[:, 1:2] * rows_ref[1]
    o_ref[...] = x1_ref[...] + g2_ref[0] * moe


def _combine(dest, y, x1, g2, gates_t, seq):
    n_tok, d = x1.shape
    tm = GATHER_TILE
    return pl.pallas_call(
        _combine_body, out_shape=SDS((n_tok, d), F32), grid=(n_tok // tm,),
        in_specs=[BS((TOP_K, tm), lambda i: (0, i), memory_space=pltpu.SMEM), BS(memory_space=pl.ANY),
                  BS((tm, d), lambda i: (i, 0)), BS((1, 1, d), lambda i: (i // (seq // tm), 0, 0)),
                  BS((tm, TOP_K), lambda i: (i, 0))],
        out_specs=BS((tm, d), lambda i: (i, 0)),
        scratch_shapes=[pltpu.VMEM((TOP_K, tm, d), F32), pltpu.SemaphoreType.DMA(())],
        compiler_params=_cparams(1), name="moe_combine",
    )(dest, y, x1, g2, gates_t)


def kernel(x, c, ctx, c_ctx, ada_w, ada_b, norm1_w, w_in, q_norm_w, k_norm_w, attn_sink, hy_conv_w, hy_conv_b, hy_w1,
           hy_b1, hy_w2, hy_b2, hy_w3, hy_sin_freq, hy_skip, group_norm_w, w_out, norm2_w, router_g_w, router_g_b,
           router_e_w, router_e_b, exp_w1, exp_w3, exp_w2):
    bsz, seq, d = x.shape
    assert ada_w.shape[0] == 1, "single-layer block"
    att_w = d // 2
    hy_w = d - att_w
    kv_w = N_KV_HEADS * HEAD_DIM
    n_heads = att_w // HEAD_DIM
    n_grp = hy_w // C_GROUP
    assert 2 * seq == FFT_MAJOR * FFT_MINOR and seq % ROW_TILE == 0 and hy_w % C_GROUP == 0
    assert w_in.shape[2] == att_w + 2 * kv_w + (HYENA_ORDER + 1) * hy_w

    pad = (-(bsz + 1)) % SUBLANES
    cc = jnp.concatenate([c, c_ctx[None, :], jnp.zeros((pad, d), F32)], axis=0)
    mod = _ada(cc, ada_w[0], ada_b[0])
    lat = [mod[:bsz, k * d:(k + 1) * d].reshape(bsz, 1, d) for k in range(6)]
    sh1, sc1, g1, sh2, sc2, g2 = lat
    csh1 = mod[bsz:bsz + 1, 0:d]
    csc1 = mod[bsz:bsz + 1, d:2 * d]

    w_in_bf = w_in[0].astype(BF16)
    qnw = jnp.tile(q_norm_w[0], n_heads).reshape(1, att_w)
    knw = jnp.tile(k_norm_w[0], N_KV_HEADS).reshape(1, kv_w)
    q, k, v, u = _inproj(x, sh1, sc1, norm1_w[0], w_in_bf, qnw, knw, att_w, kv_w)
    kx, vx = _ctxproj(ctx, csh1, csc1, norm1_w[0], w_in_bf[:, att_w:att_w + 2 * kv_w], knw, kv_w)

    gw = group_norm_w[0]
    att_n = _attention(attn_sink[0], q, k, v, kx, vx, gw[:att_w].reshape(1, att_w))

    hf = _filter_mlp(seq, hy_w1[0], hy_b1[0], hy_w2[0], hy_b2[0], hy_w3[0], hy_sin_freq[0], hy_w)
    spec = _spectrum(hf, seq, hy_w)
    cw = hy_conv_w[0].reshape(3, HYENA_ORDER + 1, n_grp, C_GROUP)
    cb = hy_conv_b[0].reshape(1, HYENA_ORDER + 1, n_grp, C_GROUP)
    taps = jnp.concatenate([cw, cb], axis=0)
    prm = jnp.zeros((HYENA_ORDER, n_grp, 16, C_GROUP), F32)
    for o in range(HYENA_ORDER):
        prm = prm.at[o, :, 4:8].set(jnp.transpose(taps[:, o + 1], (1, 0, 2)))
        prm = prm.at[o, :, 8].set(hy_skip[0, o].reshape(n_grp, C_GROUP))
    prm = prm.at[0, :, 0:4].set(jnp.transpose(taps[:, 0], (1, 0, 2)))
    z = _hyena_order(u, 0, u, n_grp, prm, 0, spec, True, n_grp)
    hy = _hyena_order(z, 0, u, 2 * n_grp, prm, 1, spec, False, n_grp)

    wr = jnp.zeros((ROUTER_ROWS, d), F32)
    wr = wr.at[0:N_GROUPS].set(router_g_w[0].T).at[EXPERT_ROW0:EXPERT_ROW0 + N_EXPERTS].set(router_e_w[0].T)
    br = jnp.zeros((ROUTER_ROWS, 1), F32)
    br = br.at[0:N_GROUPS, 0].set(router_g_b[0]).at[EXPERT_ROW0:EXPERT_ROW0 + N_EXPERTS, 0].set(router_e_b[0])
    wr_hi, wr_lo = _split(wr)
    x1, h2, ids, gates, counts = _outproj_route(att_n, hy, gw[att_w:].reshape(1, hy_w), w_out[0].astype(BF16), x,
                                                g1, sh2, sc2, norm2_w[0], wr_hi, wr_lo, br)

    n_tok = bsz * seq
    cnt = counts[:, 0].astype(I32)
    padded = (cnt + MOE_BLOCK - 1) // MOE_BLOCK * MOE_BLOCK
    pend = jnp.cumsum(padded)
    pstart = pend - padded
    e_iota = jnp.arange(N_EXPERTS, dtype=I32)
    is_e = ids[0:TOP_K, :, None] == e_iota
    dest = jnp.sum(jnp.where(is_e, pstart, 0), axis=-1) + ids[TOP_K:2 * TOP_K]
    n_blocks = (n_tok * TOP_K) // MOE_BLOCK + N_EXPERTS
    block_row0 = jnp.arange(n_blocks, dtype=I32) * MOE_BLOCK
    block_expert = jnp.minimum(jnp.sum((block_row0[:, None] >= pend[None, :]).astype(I32), axis=-1), N_EXPERTS - 1)
    n_used = (pend[-1:] // MOE_BLOCK).astype(I32)

    buf = _dispatch(pend.astype(I32), dest, h2, n_blocks * MOE_BLOCK)
    y = _experts(block_expert, n_used, buf, exp_w1[0], exp_w3[0], exp_w2[0])
    out = _combine(dest, y, x1.reshape(n_tok, d), g2, gates[0:TOP_K].T, seq)
    return out.reshape(bsz, seq, d)
```

```python
import functools
import math

import ml_dtypes
import numpy as np

import jax
import jax.numpy as jnp
from jax import lax
from jax.experimental import pallas as pl
from jax.experimental.pallas import tpu as pltpu

F32, BF16, I32 = jnp.float32, jnp.bfloat16, jnp.int32
SDS = jax.ShapeDtypeStruct
BS = pl.BlockSpec

HEAD_DIM = 64
N_KV_HEADS = 2
WINDOW = 128
Q_BLOCK = 128
GRID_W = 64
ROPE_THETA = 10000.0
EPS = 1e-6
NEG_INF = -1e30
LOG2E = math.log2(math.e)
HYENA_BANDS = 16
HYENA_ORDER = 2
DECAY_TARGET = 1e-2
FAST_DECAY_PCT = 0.3
SLOW_DECAY_PCT = 1.5
N_GROUPS = 4
EXPERTS_PER_GROUP = 8
N_EXPERTS = N_GROUPS * EXPERTS_PER_GROUP
TOP_K = 2

LANES = 128
SUBLANES = 8
VMEM_LIMIT = 60 * 1024 * 1024

ROW_TILE = 1024
MOE_BLOCK = 512
GATHER_TILE = 2048
FILTER_TILE = 512

FFT_MINOR = 128
FFT_MAJOR = 64
N_SLABS = FFT_MAJOR // 2 + 1


def _cparams(n_grid):
    return pltpu.CompilerParams(dimension_semantics=("arbitrary",) * n_grid, vmem_limit_bytes=VMEM_LIMIT)


def _mm(a, b):
    return jnp.dot(a, b, preferred_element_type=F32)


def _mm_nt(a, b):
    return lax.dot_general(a, b, (((1,), (1,)), ((), ())), preferred_element_type=F32)


def _split(x):
    hi = x.astype(BF16)
    lo = (x - hi.astype(F32)).astype(BF16)
    return hi, lo


def _mm_split(w_hi, w_lo, x, passes):
    if passes == 1:
        return _mm(w_hi, x.astype(BF16))
    x_hi, x_lo = _split(x)
    out = _mm(w_hi, x_hi) + _mm(w_hi, x_lo)
    if passes == 3:
        out = out + _mm(w_lo, x_hi)
    return out


def _mm3(a, b):
    a_hi, a_lo = _split(a)
    b_hi, b_lo = _split(b)
    return _mm(a_hi, b_hi) + _mm(a_hi, b_lo) + _mm(a_lo, b_hi)


def _silu(x):
    return x * (1.0 / (1.0 + jnp.exp(-x)))


def _rms(x):
    return x * lax.rsqrt(jnp.mean(x * x, axis=-1, keepdims=True) + EPS)


@functools.lru_cache(maxsize=None)
def _rope_tables(seq_len):
    pos = np.arange(seq_len)
    rows = (pos // GRID_W).astype(np.float64)
    cols = (pos % GRID_W).astype(np.float64)
    pairs = HEAD_DIM // 4
    inv_freq = ROPE_THETA ** (-np.arange(pairs, dtype=np.float64) / pairs)
    ang_r = rows[:, None] * inv_freq[None, :]
    ang_c = cols[:, None] * inv_freq[None, :]
    ang = np.concatenate([ang_r, ang_r, ang_c, ang_c], axis=1)
    sign = np.tile(np.concatenate([-np.ones(pairs), np.ones(pairs)]), 2)
    cos = np.cos(ang)
    sin = np.sin(ang) * sign[None, :]
    reps = LANES // HEAD_DIM
    return np.tile(cos, (1, reps)).astype(np.float32), np.tile(sin, (1, reps)).astype(np.float32)


@functools.lru_cache(maxsize=None)
def _block_diag_mean(width):
    h = np.arange(width) // HEAD_DIM
    return ((h[:, None] == h[None, :]).astype(np.float32) / HEAD_DIM).astype(ml_dtypes.bfloat16)


@functools.lru_cache(maxsize=None)
def _filter_features(seq_len, width):
    pos = np.arange(seq_len, dtype=np.float64)
    t = pos / seq_len
    bands = np.linspace(1e-4, HYENA_BANDS - 1, HYENA_BANDS)
    ang = (2.0 * math.pi / seq_len) * pos[:, None] * bands[None, :]
    feat = np.concatenate([t[:, None], np.cos(ang), -np.sin(ang)], axis=-1)
    out = np.zeros((seq_len, width), np.float32)
    out[:, :feat.shape[1]] = feat
    return out


@functools.lru_cache(maxsize=None)
def _decay_rates(width):
    max_decay = math.log(DECAY_TARGET) / FAST_DECAY_PCT
    min_decay = math.log(DECAY_TARGET) / SLOW_DECAY_PCT
    return np.abs(np.linspace(min_decay, max_decay, width)).astype(np.float32)


@functools.lru_cache(maxsize=None)
def _dft_tables():
    n = FFT_MAJOR * FFT_MINOR
    half = FFT_MAJOR // 2
    n2 = np.arange(half)
    a = np.zeros((FFT_MAJOR, half))
    a[:half] = np.cos(2 * np.pi * np.arange(half)[:, None] * n2[None, :] / FFT_MAJOR)
    a[half] = (-1.0) ** n2
    k2 = np.arange(1, half)
    a[half + 1:] = -np.sin(2 * np.pi * k2[:, None] * n2[None, :] / FFT_MAJOR)
    d = np.zeros((half, FFT_MAJOR))
    d[:, 0] = 1.0 / n
    d[:, 1:half] = 2 * np.cos(2 * np.pi * n2[:, None] * k2[None, :] / FFT_MAJOR) / n
    d[:, half] = (-1.0) ** n2 / n
    d[:, half + 1:] = -2 * np.sin(2 * np.pi * n2[:, None] * k2[None, :] / FFT_MAJOR) / n
    eye = np.eye(SUBLANES)
    ak = np.kron(a, eye)
    dk = np.kron(d, eye)
    s = np.arange(N_SLABS)[:, None, None]
    k1 = np.arange(FFT_MINOR)[None, :, None]
    n1 = np.arange(FFT_MINOR)[None, None, :]
    g = np.exp(-2j * np.pi * n1 * (FFT_MAJOR * k1 + s) / n)
    mf = np.block([[g.real, -g.imag], [g.imag, g.real]])
    gi = np.conj(g).transpose(0, 2, 1)
    mi = np.block([[gi.real, -gi.imag], [gi.imag, gi.real]])
    return {name: m.astype(np.float32) for name, m in (("ak", ak), ("dk", dk), ("mf", mf), ("mi", mi))}


def _dft_hi_lo(name):
    return _split(jnp.asarray(_dft_tables()[name]))


def _ada_body(c_ref, w_ref, b_ref, o_ref):
    o_ref[...] = _mm3(_silu(c_ref[...]), w_ref[...]) + b_ref[...]


def _ada(cc, ada_w, ada_b):
    rows, d = cc.shape
    n = ada_w.shape[1]
    tn = 1536
    return pl.pallas_call(
        _ada_body, out_shape=SDS((rows, n), F32), grid=(n // tn,),
        in_specs=[BS((rows, d), lambda j: (0, 0)), BS((d, tn), lambda j: (0, j)), BS((1, tn), lambda j: (0, j))],
        out_specs=BS((rows, tn), lambda j: (0, j)), compiler_params=_cparams(1), name="ada_mod",
    )(cc, ada_w, ada_b.reshape(1, n))


def _head_rms(t, bd):
    hi, lo = _split(t * t)
    return t * lax.rsqrt(_mm(hi, bd) + _mm(lo, bd) + EPS)


def _rope(t, cos, sin):
    n = t.shape[1]
    quarter = HEAD_DIM // 4
    lane = lax.broadcasted_iota(I32, t.shape, 1)
    up = pltpu.roll(t, n - quarter, axis=1)
    dn = pltpu.roll(t, quarter, axis=1)
    partner = jnp.where((lane & (2 * quarter - 1)) < quarter, up, dn)
    return t * cos + partner * sin


def _inproj_body(x_ref, sh_ref, sc_ref, nw_ref, w_ref, qnw_ref, knw_ref, cos_ref, sin_ref, bdq_ref, bdk_ref,
                 q_ref, k_ref, v_ref, u_ref, *, att_w, kv_w):
    h = _rms(x_ref[0]) * nw_ref[...]
    h = h * (1.0 + sc_ref[0]) + sh_ref[0]
    proj = _mm(h.astype(BF16), w_ref[...])
    cos, sin = cos_ref[...], sin_ref[...]
    reps = att_w // LANES
    cos_q = jnp.concatenate([cos] * reps, axis=1)
    sin_q = jnp.concatenate([sin] * reps, axis=1)
    q = _head_rms(proj[:, :att_w], bdq_ref[...]) * qnw_ref[...]
    q_ref[0] = (_rope(q, cos_q, sin_q) * (HEAD_DIM ** -0.5 * LOG2E)).astype(BF16)
    k = _head_rms(proj[:, att_w:att_w + kv_w], bdk_ref[...]) * knw_ref[...]
    k_ref[0] = _rope(k, cos, sin).astype(BF16)
    v_ref[0] = proj[:, att_w + kv_w:att_w + 2 * kv_w].astype(BF16)
    hy_off = att_w + 2 * kv_w
    for j in range(u_ref.shape[1]):
        u_ref[0, j] = proj[:, hy_off + LANES * j:hy_off + LANES * (j + 1)]


def _inproj(x, sh, sc, norm_w, w_bf, qnw, knw, att_w, kv_w):
    bsz, seq, d = x.shape
    tm = ROW_TILE
    hy_tiles = (w_bf.shape[1] - att_w - 2 * kv_w) // LANES
    cos, sin = _rope_tables(seq)
    row = lambda b, i: (b, i, 0)
    fixed2 = lambda b, i: (0, 0)
    per_b = lambda b, i: (b, 0, 0)
    return pl.pallas_call(
        functools.partial(_inproj_body, att_w=att_w, kv_w=kv_w),
        out_shape=(SDS((bsz, seq, att_w), BF16), SDS((bsz, seq, kv_w), BF16), SDS((bsz, seq, kv_w), BF16),
                   SDS((bsz, hy_tiles, seq, LANES), F32)),
        grid=(bsz, seq // tm),
        in_specs=[BS((1, tm, d), row), BS((1, 1, d), per_b), BS((1, 1, d), per_b), BS((1, d), fixed2),
                  BS(w_bf.shape, fixed2), BS((1, att_w), fixed2), BS((1, kv_w), fixed2),
                  BS((tm, LANES), lambda b, i: (i, 0)), BS((tm, LANES), lambda b, i: (i, 0)),
                  BS((att_w, att_w), fixed2), BS((kv_w, kv_w), fixed2)],
        out_specs=(BS((1, tm, att_w), row), BS((1, tm, kv_w), row), BS((1, tm, kv_w), row),
                   BS((1, hy_tiles, tm, LANES), lambda b, i: (b, 0, i, 0))),
        compiler_params=_cparams(2), name="in_proj",
    )(x, sh, sc, norm_w.reshape(1, d), w_bf, qnw, knw, jnp.asarray(cos), jnp.asarray(sin),
      jnp.asarray(_block_diag_mean(att_w)), jnp.asarray(_block_diag_mean(kv_w)))


def _ctxproj_body(x_ref, sh_ref, sc_ref, nw_ref, w_ref, knw_ref, bdk_ref, k_ref, v_ref, *, kv_w):
    h = _rms(x_ref[0]) * nw_ref[...]
    h = h * (1.0 + sc_ref[...]) + sh_ref[...]
    proj = _mm(h.astype(BF16), w_ref[...])
    k_ref[0] = (_head_rms(proj[:, :kv_w], bdk_ref[...]) * knw_ref[...]).astype(BF16)
    v_ref[0] = proj[:, kv_w:].astype(BF16)


def _ctxproj(ctx, sh, sc, norm_w, w_kv_bf, knw, kv_w):
    bsz, n_ctx, d = ctx.shape
    fixed2 = lambda b: (0, 0)
    row = lambda b: (b, 0, 0)
    return pl.pallas_call(
        functools.partial(_ctxproj_body, kv_w=kv_w),
        out_shape=(SDS((bsz, n_ctx, kv_w), BF16), SDS((bsz, n_ctx, kv_w), BF16)), grid=(bsz,),
        in_specs=[BS((1, n_ctx, d), row), BS((1, d), fixed2), BS((1, d), fixed2), BS((1, d), fixed2),
                  BS(w_kv_bf.shape, fixed2), BS((1, kv_w), fixed2), BS((kv_w, kv_w), fixed2)],
        out_specs=(BS((1, n_ctx, kv_w), row), BS((1, n_ctx, kv_w), row)),
        compiler_params=_cparams(1), name="ctx_proj",
    )(ctx, sh, sc, norm_w.reshape(1, d), w_kv_bf, knw, jnp.asarray(_block_diag_mean(kv_w)))


def _attn_body(sink_ref, q_ref, kp_ref, kc_ref, kn_ref, vp_ref, vc_ref, vn_ref, kx_ref, vx_ref, gw_ref, bias_ref,
               o_ref):
    n_ctx = kx_ref.shape[1]
    n_heads = q_ref.shape[2] // HEAD_DIM
    group = n_heads // N_KV_HEADS
    rows = group * Q_BLOCK
    bias = bias_ref[0]
    head_of_row = lax.broadcasted_iota(I32, (rows, 1), 0) // Q_BLOCK
    outs = []
    for g in range(N_KV_HEADS):
        sl = slice(g * HEAD_DIM, (g + 1) * HEAD_DIM)
        kb = jnp.concatenate([kx_ref[0][:, sl], kp_ref[0][:, sl], kc_ref[0][:, sl], kn_ref[0][:, sl]], axis=0)
        vb = jnp.concatenate([vx_ref[0][:, sl], vp_ref[0][:, sl], vc_ref[0][:, sl], vn_ref[0][:, sl]], axis=0)
        heads = range(g * group, (g + 1) * group)
        q4 = jnp.concatenate([q_ref[0][:, h * HEAD_DIM:(h + 1) * HEAD_DIM] for h in heads], axis=0)
        sink = jnp.zeros((rows, 1), F32)
        for hh, h in enumerate(heads):
            sink = jnp.where(head_of_row == hh, sink_ref[h] * LOG2E, sink)
        s = _mm_nt(q4, kb)
        s_ctx = s[:, :n_ctx]
        s_band = s[:, n_ctx:] + bias
        m = jnp.maximum(jnp.maximum(jnp.max(s_ctx, axis=-1, keepdims=True), jnp.max(s_band, axis=-1, keepdims=True)),
                        sink)
        p_ctx = jnp.exp2((s_ctx - m).astype(BF16))
        p_band = jnp.exp2((s_band - m).astype(BF16))
        denom = (jnp.sum(p_ctx.astype(F32), axis=-1, keepdims=True) + jnp.sum(p_band.astype(F32), axis=-1, keepdims=True)
                 + jnp.exp2(sink - m))
        o4 = (_mm(p_ctx, vb[:n_ctx]) + _mm(p_band, vb[n_ctx:])) / denom
        for hh in range(group):
            outs.append(o4[hh * Q_BLOCK:(hh + 1) * Q_BLOCK])
    att = jnp.concatenate(outs, axis=1)
    o_ref[0] = (_rms(att) * gw_ref[...]).astype(BF16)


@functools.lru_cache(maxsize=None)
def _band_bias(rows):
    r = (np.arange(rows) % Q_BLOCK)[:, None]
    j = np.arange(3 * Q_BLOCK)[None, :]
    in_band = (j >= r) & (j <= r + 2 * WINDOW)
    cases = (in_band, in_band & (j >= Q_BLOCK), in_band & (j < 2 * Q_BLOCK))
    return np.stack([np.where(c, 0.0, NEG_INF) for c in cases]).astype(np.float32)


def _attention(sink, q, k, v, kx, vx, gw):
    bsz, seq, att_w = q.shape
    kv_w = k.shape[2]
    n_ctx = kx.shape[1]
    nb = seq // Q_BLOCK
    cur = lambda b, i: (b, i, 0)
    prev = lambda b, i: (b, jnp.maximum(i - 1, 0), 0)
    nxt = lambda b, i: (b, jnp.minimum(i + 1, nb - 1), 0)
    per_b = lambda b, i: (b, 0, 0)
    kvb = (1, Q_BLOCK, kv_w)
    rows = (att_w // HEAD_DIM // N_KV_HEADS) * Q_BLOCK
    assert nb >= 2
    edge_case = lambda b, i: (jnp.where(i == 0, 1, jnp.where(i == nb - 1, 2, 0)), 0, 0)
    return pl.pallas_call(
        _attn_body, out_shape=SDS((bsz, seq, att_w), BF16), grid=(bsz, nb),
        in_specs=[BS(memory_space=pltpu.SMEM), BS((1, Q_BLOCK, att_w), cur),
                  BS(kvb, prev), BS(kvb, cur), BS(kvb, nxt), BS(kvb, prev), BS(kvb, cur), BS(kvb, nxt),
                  BS((1, n_ctx, kv_w), per_b), BS((1, n_ctx, kv_w), per_b), BS((1, att_w), lambda b, i: (0, 0)),
                  BS((1, rows, 3 * Q_BLOCK), edge_case)],
        out_specs=BS((1, Q_BLOCK, att_w), cur), compiler_params=_cparams(2), name="window_attn",
    )(sink, q, k, k, k, v, v, v, kx, vx, gw, jnp.asarray(_band_bias(rows)))


def _filter_body(f_ref, w1h, w1l, b1, w2h, w2l, b2, w3h, w3l, sf_ref, dl_ref, o_ref, *, hy_w):
    def mm_w(a, wh, wl):
        a_hi, a_lo = _split(a)
        return _mm(a_hi, wh[...]) + _mm(a_hi, wl[...]) + _mm(a_lo, wh[...])

    f = f_ref[...]
    h = jnp.sin(sf_ref[0:1, :] * (mm_w(f, w1h, w1l) + b1[...]))
    h = jnp.sin(sf_ref[1:2, :] * (mm_w(h, w2h, w2l) + b2[...]))
    h = mm_w(h, w3h, w3l)
    h = h * jnp.exp(-f[:, 0:1] * dl_ref[...])
    tl = f.shape[0]
    row = lax.broadcasted_iota(I32, h.shape, 0) + pl.program_id(0) * tl
    col = lax.broadcasted_iota(I32, h.shape, 1)
    is_bwd = ((col // hy_w) & 1) == 1
    h = jnp.where((row == 0) & is_bwd, 0.0, h)
    for j in range(o_ref.shape[0]):
        o_ref[j] = h[:, LANES * j:LANES * (j + 1)]


def _filter_mlp(seq, w1, b1, w2, b2, w3, sin_freq, hy_w):
    hid = w1.shape[1]
    fw = 64
    feat = jnp.asarray(_filter_features(seq, fw))
    w1p = jnp.zeros((fw, hid), F32).at[:w1.shape[0]].set(w1)
    n_out = w3.shape[1]
    delta = jnp.asarray(np.tile(_decay_rates(hy_w), n_out // hy_w)).reshape(1, n_out)
    tl = FILTER_TILE
    fixed = lambda i: (0, 0)
    ops = []
    for w in (w1p, w2, w3):
        ops.extend(_split(w))
    return pl.pallas_call(
        functools.partial(_filter_body, hy_w=hy_w),
        out_shape=SDS((n_out // LANES, seq, LANES), F32), grid=(seq // tl,),
        in_specs=[BS((tl, fw), lambda i: (i, 0)), BS((fw, hid), fixed), BS((fw, hid), fixed), BS((1, hid), fixed),
                  BS((hid, hid), fixed), BS((hid, hid), fixed), BS((1, hid), fixed),
                  BS((hid, n_out), fixed), BS((hid, n_out), fixed), BS((2, hid), fixed), BS((1, n_out), fixed)],
        out_specs=BS((n_out // LANES, tl, LANES), lambda i: (0, i, 0)),
        compiler_params=_cparams(1), name="hyena_filter_mlp",
    )(feat, ops[0], ops[1], b1.reshape(1, hid), ops[2], ops[3], b2.reshape(1, hid), ops[4], ops[5], sin_freq, delta)


C_TILES = 2
C_GROUP = C_TILES * LANES


def _load_cat(ref, row0, n_rows):
    return jnp.concatenate([ref[c, pl.ds(row0, n_rows), :] for c in range(C_TILES)], axis=1)


def _store_cat(ref, row0, n_rows, val):
    for c in range(C_TILES):
        ref[c, pl.ds(row0, n_rows), :] = val[:, c * LANES:(c + 1) * LANES]


def _fwd_major(src, p_ref, ak, passes):
    half = FFT_MAJOR // 2

    def body(g, carry):
        r0 = pl.multiple_of(g * SUBLANES, SUBLANES)
        st = jnp.concatenate([_load_cat(src, n2 * FFT_MINOR + r0, SUBLANES) for n2 in range(half)], axis=0)
        out = _mm_split(ak[0][...], ak[1][...] if passes == 3 else None, st, passes)
        for p in range(FFT_MAJOR):
            _store_cat(p_ref, p * FFT_MINOR + r0, SUBLANES, out[p * SUBLANES:(p + 1) * SUBLANES])
        return carry

    lax.fori_loop(0, FFT_MINOR // SUBLANES, body, 0, unroll=4)


def _inv_major(p_ref, dst, dk, passes):
    half = FFT_MAJOR // 2

    def body(g, carry):
        r0 = pl.multiple_of(g * SUBLANES, SUBLANES)
        st = jnp.concatenate([_load_cat(p_ref, p * FFT_MINOR + r0, SUBLANES) for p in range(FFT_MAJOR)], axis=0)
        out = _mm_split(dk[0][...], dk[1][...] if passes == 3 else None, st, passes)
        for n2 in range(half):
            _store_cat(dst, n2 * FFT_MINOR + r0, SUBLANES, out[n2 * SUBLANES:(n2 + 1) * SUBLANES])
        return carry

    lax.fori_loop(0, FFT_MINOR // SUBLANES, body, 0, unroll=4)


def _slab_spectrum(p_ref, mf, s, passes, real_slot=None):
    m = FFT_MINOR
    if real_slot is None:
        st = jnp.concatenate([_load_cat(p_ref, s * m, m), _load_cat(p_ref, (FFT_MAJOR // 2 + s) * m, m)], axis=0)
        x = _mm_split(mf[0][s], mf[1][s] if passes == 3 else None, st, passes)
    else:
        st = _load_cat(p_ref, real_slot * m, m)
        x = _mm_split(mf[0][s, :, 0:m], mf[1][s, :, 0:m] if passes == 3 else None, st, passes)
    return x[:m], x[m:]


def _spectrum_body(f_ref, b_ref, akh, akl, mfh, mfl, o_ref, p_ref):
    half = FFT_MAJOR // 2
    ak, mf = (akh, akl), (mfh, mfl)
    for is_bwd, src in ((False, f_ref), (True, b_ref)):
        _fwd_major(src, p_ref, ak, 3)

        def put(s, re, im, is_bwd=is_bwd):
            if is_bwd:
                o_ref[0, 0, s, 0] = o_ref[0, 0, s, 0] + re
                o_ref[0, 0, s, 1] = o_ref[0, 0, s, 1] - im
            else:
                o_ref[0, 0, s, 0] = re
                o_ref[0, 0, s, 1] = im

        put(0, *_slab_spectrum(p_ref, mf, 0, 3, real_slot=0))
        put(half, *_slab_spectrum(p_ref, mf, half, 3, real_slot=half))

        def body(s, carry):
            put(s, *_slab_spectrum(p_ref, mf, s, 3))
            return carry

        lax.fori_loop(1, half, body, 0, unroll=2)


def _spectrum(hf, seq, hy_w):
    ak, mf = _dft_hi_lo("ak"), _dft_hi_lo("mf")
    n_grp = hy_w // C_GROUP
    fixed2 = lambda o, c: (0, 0)
    fixed3 = lambda o, c: (0, 0, 0)
    one = pl.Buffered(1)
    return pl.pallas_call(
        _spectrum_body,
        out_shape=SDS((HYENA_ORDER, n_grp, N_SLABS, 2, FFT_MINOR, C_GROUP), F32), grid=(HYENA_ORDER, n_grp),
        in_specs=[BS((C_TILES, seq, LANES), lambda o, c: (o * 2 * n_grp + c, 0, 0), pipeline_mode=one),
                  BS((C_TILES, seq, LANES), lambda o, c: (o * 2 * n_grp + n_grp + c, 0, 0), pipeline_mode=one),
                  BS(ak[0].shape, fixed2, pipeline_mode=one), BS(ak[0].shape, fixed2, pipeline_mode=one),
                  BS(mf[0].shape, fixed3, pipeline_mode=one), BS(mf[0].shape, fixed3, pipeline_mode=one)],
        out_specs=BS((1, 1, N_SLABS, 2, FFT_MINOR, C_GROUP), lambda o, c: (o, c, 0, 0, 0, 0)),
        scratch_shapes=[pltpu.VMEM((C_TILES, FFT_MAJOR * FFT_MINOR, LANES), F32)],
        compiler_params=_cparams(2), name="hyena_spectrum",
    )(hf, hf, ak[0], ak[1], mf[0], mf[1])


CONV_CHUNK = 128
CONV_PASSES = 1


def _sconv_chunk(ref, c, r0, n_rows, seq, prm_ref, base):
    cl = slice(c * LANES, (c + 1) * LANES)
    cur = ref[c, pl.ds(r0, n_rows), :]
    row = lax.broadcasted_iota(I32, cur.shape, 0)
    before = ref[c, pl.ds(jnp.maximum(r0 - 1, 0), 1), :] * (r0 > 0).astype(F32)
    after = ref[c, pl.ds(jnp.minimum(r0 + n_rows, seq - 1), 1), :] * (r0 + n_rows < seq).astype(F32)
    prev = jnp.where(row == 0, before, pltpu.roll(cur, 1, axis=0))
    nxt = jnp.where(row == n_rows - 1, after, pltpu.roll(cur, n_rows - 1, axis=0))
    w = lambda k: prm_ref[0, 0, base + k:base + k + 1, cl]
    return w(3) + w(0) * prev + w(1) * cur + w(2) * nxt


def _conv_body(z_ref, x_ref, prm_ref, akh, dkh, mfh, mih, h_ref, o_ref, zs_ref, p_ref, *, conv_z):
    seq = z_ref.shape[2]
    half = FFT_MAJOR // 2
    m = FFT_MINOR
    zv, xv = z_ref.at[0], x_ref.at[0]
    ak, dk, mf, mi = (akh, None), (dkh, None), (mfh, None), (mih, None)
    n_chunks = seq // CONV_CHUNK

    if conv_z:
        def prep(i, carry):
            r0 = pl.multiple_of(i * CONV_CHUNK, CONV_CHUNK)
            for c in range(C_TILES):
                zs_ref[c, pl.ds(r0, CONV_CHUNK), :] = _sconv_chunk(zv, c, r0, CONV_CHUNK, seq, prm_ref, 0)
            return carry

        lax.fori_loop(0, n_chunks, prep, 0)
        src = zs_ref
    else:
        src = zv
    _fwd_major(src, p_ref, ak, CONV_PASSES)

    def slab(s, real_slot=None):
        xr, xi = _slab_spectrum(p_ref, mf, s, CONV_PASSES, real_slot)
        hr, hi = h_ref[0, 0, s, 0], h_ref[0, 0, s, 1]
        y = jnp.concatenate([xr * hr - xi * hi, xr * hi + xi * hr], axis=0)
        if real_slot is None:
            out = _mm_split(mi[0][s], None, y, CONV_PASSES)
            _store_cat(p_ref, s * m, m, out[:m])
            _store_cat(p_ref, (half + s) * m, m, out[m:])
        else:
            out = _mm_split(mi[0][s, 0:m, :], None, y, CONV_PASSES)
            _store_cat(p_ref, real_slot * m, m, out)

    slab(0, real_slot=0)
    slab(half, real_slot=half)

    def slab_loop(s, carry):
        slab(s)
        return carry

    lax.fori_loop(1, half, slab_loop, 0, unroll=8)
    _inv_major(p_ref, zs_ref, dk, CONV_PASSES)

    def fin(i, carry):
        r0 = pl.multiple_of(i * CONV_CHUNK, CONV_CHUNK)
        for c in range(C_TILES):
            cl = slice(c * LANES, (c + 1) * LANES)
            gate = _sconv_chunk(xv, c, r0, CONV_CHUNK, seq, prm_ref, 4)
            if conv_z:
                zval = _sconv_chunk(zv, c, r0, CONV_CHUNK, seq, prm_ref, 0)
            else:
                zval = zv[c, pl.ds(r0, CONV_CHUNK), :]
            skip = prm_ref[0, 0, 8:9, cl]
            o_ref[0, c, pl.ds(r0, CONV_CHUNK), :] = gate * (zs_ref[c, pl.ds(r0, CONV_CHUNK), :] + skip * zval)
        return carry

    lax.fori_loop(0, n_chunks, fin, 0, unroll=2)


def _hyena_order(z, z_grp0, x, x_grp0, prm, order, spec, conv_z, n_grp):
    bsz, _, seq, _ = z.shape
    ak, dk, mf, mi = (_dft_hi_lo(n)[0] for n in ("ak", "dk", "mf", "mi"))
    one = pl.Buffered(1)
    fixed2 = lambda c, b: (0, 0)
    fixed3 = lambda c, b: (0, 0, 0)
    blk = (1, C_TILES, seq, LANES)
    return pl.pallas_call(
        functools.partial(_conv_body, conv_z=conv_z),
        out_shape=SDS((bsz, n_grp * C_TILES, seq, LANES), F32), grid=(n_grp, bsz),
        in_specs=[BS(blk, lambda c, b: (b, z_grp0 + c, 0, 0)),
                  BS(blk, lambda c, b: (b, x_grp0 + c, 0, 0)),
                  BS((1, 1, 16, C_GROUP), lambda c, b: (order, c, 0, 0)),
                  BS(ak.shape, fixed2, pipeline_mode=one), BS(dk.shape, fixed2, pipeline_mode=one),
                  BS(mf.shape, fixed3, pipeline_mode=one), BS(mi.shape, fixed3, pipeline_mode=one),
                  BS((1, 1, N_SLABS, 2, FFT_MINOR, C_GROUP), lambda c, b: (order, c, 0, 0, 0, 0), pipeline_mode=one)],
        out_specs=BS(blk, lambda c, b: (b, c, 0, 0)),
        scratch_shapes=[pltpu.VMEM((C_TILES, seq, LANES), F32),
                        pltpu.VMEM((C_TILES, FFT_MAJOR * FFT_MINOR, LANES), F32)],
        compiler_params=_cparams(2), name=f"hyena_conv{order}",
    )(z, x, prm, ak, dk, mf, mi, spec)


ROUTER_ROWS = 48
EXPERT_ROW0 = 8


def _outproj_body(att_ref, hy_ref, gwh_ref, wo_ref, x_ref, g1_ref, sh2_ref, sc2_ref, n2w_ref, wrh_ref, wrl_ref,
                  br_ref, tri_ref, x1_ref, h2_ref, ids_ref, gates_ref, cnt_ref, carry_ref):
    @pl.when((pl.program_id(0) == 0) & (pl.program_id(1) == 0))
    def _():
        carry_ref[...] = jnp.zeros_like(carry_ref)

    hy = jnp.concatenate([hy_ref[0, j] for j in range(hy_ref.shape[1])], axis=1)
    hyn = _rms(hy) * gwh_ref[...]
    mix = _mm(jnp.concatenate([att_ref[0], hyn.astype(BF16)], axis=1), wo_ref[...])
    x1 = x_ref[0] + g1_ref[0] * mix
    x1_ref[0] = x1
    h2 = _rms(x1) * n2w_ref[...]
    h2 = h2 * (1.0 + sc2_ref[0]) + sh2_ref[0]
    h2_ref[...] = h2

    h_hi, h_lo = _split(h2)
    lg = _mm_nt(wrh_ref[...], h_hi) + _mm_nt(wrh_ref[...], h_lo) + _mm_nt(wrl_ref[...], h_hi) + br_ref[...]
    tm = lg.shape[1]
    gl = lg[0:N_GROUPS]
    el = lg[EXPERT_ROW0:EXPERT_ROW0 + N_EXPERTS]
    gmax = jnp.max(gl, axis=0, keepdims=True)
    rg = lax.broadcasted_iota(I32, gl.shape, 0).astype(F32)
    g_idx = jnp.min(jnp.where(gl == gmax, rg, float(N_GROUPS)), axis=0, keepdims=True)
    g_val = 1.0 / jnp.sum(jnp.exp(gl - gmax), axis=0, keepdims=True)
    re_i = lax.broadcasted_iota(I32, el.shape, 0)
    re = re_i.astype(F32)
    in_group = (re_i // EXPERTS_PER_GROUP).astype(F32) == g_idx
    elm = jnp.where(in_group, el, NEG_INF)
    m1 = jnp.max(elm, axis=0, keepdims=True)
    i1 = jnp.min(jnp.where(elm == m1, re, float(N_EXPERTS)), axis=0, keepdims=True)
    elm2 = jnp.where(re == i1, NEG_INF, elm)
    m2 = jnp.max(elm2, axis=0, keepdims=True)
    i2 = jnp.min(jnp.where(elm2 == m2, re, float(N_EXPERTS)), axis=0, keepdims=True)
    e2 = jnp.exp(m2 - m1)
    inv = g_val / (1.0 + e2)

    oh1 = (re == i1).astype(F32)
    oh2 = (re == i2).astype(F32)
    oh = oh1 + oh2
    base = _mm(oh.astype(BF16), tri_ref[...]) + carry_ref[:, 0:1]
    rank1 = jnp.sum(oh1 * base, axis=0, keepdims=True)
    rank2 = jnp.sum(oh2 * base, axis=0, keepdims=True)
    carry_ref[...] = carry_ref[...] + jnp.sum(oh, axis=1, keepdims=True)
    cnt_ref[...] = carry_ref[...]

    zi = jnp.zeros((1, tm), I32)
    for k, val in enumerate((i1.astype(I32), i2.astype(I32), rank1.astype(I32), rank2.astype(I32), zi, zi, zi, zi)):
        ids_ref[k:k + 1, :] = val
    zf = jnp.zeros((1, tm), F32)
    for k, val in enumerate((inv, inv * e2, zf, zf, zf, zf, zf, zf)):
        gates_ref[k:k + 1, :] = val


def _outproj_route(att_n, hy, gw_hy, wo_bf, x, g1, sh2, sc2, norm2_w, wr_hi, wr_lo, br):
    bsz, seq, d = x.shape
    tm = ROW_TILE
    n_tok = bsz * seq
    att_w = att_n.shape[2]
    hy_tiles = hy.shape[1]
    tri = jnp.asarray(np.triu(np.ones((tm, tm), np.float32), k=1).astype(ml_dtypes.bfloat16))
    row = lambda b, i: (b, i, 0)
    per_b = lambda b, i: (b, 0, 0)
    fixed2 = lambda b, i: (0, 0)
    tok = lambda b, i: (0, b * (seq // tm) + i)
    return pl.pallas_call(
        _outproj_body,
        out_shape=(SDS((bsz, seq, d), F32), SDS((n_tok, d), F32), SDS((SUBLANES, n_tok), I32),
                   SDS((SUBLANES, n_tok), F32), SDS((N_EXPERTS, LANES), F32)),
        grid=(bsz, seq // tm),
        in_specs=[BS((1, tm, att_w), row), BS((1, hy_tiles, tm, LANES), lambda b, i: (b, 0, i, 0)),
                  BS((1, hy_tiles * LANES), fixed2), BS(wo_bf.shape, fixed2), BS((1, tm, d), row),
                  BS((1, 1, d), per_b), BS((1, 1, d), per_b), BS((1, 1, d), per_b), BS((1, d), fixed2),
                  BS((ROUTER_ROWS, d), fixed2), BS((ROUTER_ROWS, d), fixed2), BS((ROUTER_ROWS, 1), fixed2),
                  BS((tm, tm), fixed2)],
        out_specs=(BS((1, tm, d), row), BS((tm, d), lambda b, i: (b * (seq // tm) + i, 0)),
                   BS((SUBLANES, tm), tok), BS((SUBLANES, tm), tok), BS((N_EXPERTS, LANES), fixed2)),
        scratch_shapes=[pltpu.VMEM((N_EXPERTS, LANES), F32)],
        compiler_params=_cparams(2), name="out_proj_router",
    )(att_n, hy, gw_hy, wo_bf, x, g1, sh2, sc2, norm2_w.reshape(1, d), wr_hi, wr_lo, br, tri)


ISSUE_UNROLL = 8


def _row_copy(src, src_row, dst, dst_row, sem):
    return pltpu.make_async_copy(src.at[pl.ds(src_row, 1), :], dst.at[pl.ds(dst_row, 1), :], sem)


def _dispatch_body(pend_ref, d_ref, h_ref, buf_ref, zero_ref, sem):
    n = h_ref.shape[0]

    @pl.when(pl.program_id(0) == 0)
    def _():
        zero_ref[...] = jnp.zeros_like(zero_ref)

        def tail_copy(e):
            row0 = pl.multiple_of(pend_ref[e] - MOE_BLOCK, MOE_BLOCK)
            return pltpu.make_async_copy(zero_ref, buf_ref.at[pl.ds(row0, MOE_BLOCK), :], sem)

        def non_empty(e):
            return pend_ref[e] > (pend_ref[e - 1] if e > 0 else 0)

        def spare_copy(i):
            row0 = pl.multiple_of(i * MOE_BLOCK, MOE_BLOCK)
            return pltpu.make_async_copy(zero_ref, buf_ref.at[pl.ds(row0, MOE_BLOCK), :], sem)

        first_spare = pend_ref[N_EXPERTS - 1] // MOE_BLOCK
        n_blocks = buf_ref.shape[0] // MOE_BLOCK
        for e in range(N_EXPERTS):
            @pl.when(non_empty(e))
            def _(e=e):
                tail_copy(e).start()
        lax.fori_loop(first_spare, n_blocks, lambda i, c: (spare_copy(i).start(), c)[1], 0)
        for e in range(N_EXPERTS):
            @pl.when(non_empty(e))
            def _(e=e):
                tail_copy(e).wait()
        lax.fori_loop(first_spare, n_blocks, lambda i, c: (spare_copy(i).wait(), c)[1], 0)

    def issue(t, carry):
        for j in range(TOP_K):
            _row_copy(h_ref, t, buf_ref, d_ref[j, t], sem).start()
        return carry

    lax.fori_loop(0, n, issue, 0, unroll=ISSUE_UNROLL)
    for j in range(TOP_K):
        pltpu.make_async_copy(h_ref, buf_ref.at[pl.ds(0, n), :], sem).wait()


def _dispatch(pend, dest, h2, n_rows):
    n_tok, d = h2.shape
    tm = GATHER_TILE
    grid_spec = pltpu.PrefetchScalarGridSpec(
        num_scalar_prefetch=1, grid=(n_tok // tm,),
        in_specs=[BS((TOP_K, tm), lambda i, pend: (0, i), memory_space=pltpu.SMEM),
                  BS((tm, d), lambda i, pend: (i, 0))],
        out_specs=BS(memory_space=pl.ANY),
        scratch_shapes=[pltpu.VMEM((MOE_BLOCK, d), F32), pltpu.SemaphoreType.DMA(())])
    return pl.pallas_call(_dispatch_body, out_shape=SDS((n_rows, d), F32), grid_spec=grid_spec,
                          compiler_params=_cparams(1), name="moe_dispatch")(pend, dest, h2)


def _expert_body(be_ref, nu_ref, x_ref, w1_ref, w3_ref, w2_ref, y_ref, w13_s, w2_s):
    i = pl.program_id(0)
    hidden = w2_ref.shape[1]
    active = i < nu_ref[0]
    new_expert = jnp.logical_or(i == 0, be_ref[i] != be_ref[jnp.maximum(i - 1, 0)])

    @pl.when(jnp.logical_and(active, new_expert))
    def _():
        w13_s[:, :hidden] = w1_ref[0].astype(BF16)
        w13_s[:, hidden:] = w3_ref[0].astype(BF16)
        w2_s[...] = w2_ref[0].astype(BF16)

    @pl.when(active)
    def _():
        h = _mm(x_ref[...].astype(BF16), w13_s[...])
        act = _silu(h[:, :hidden]) * h[:, hidden:]
        y_ref[...] = _mm(act.astype(BF16), w2_s[...])

    @pl.when(jnp.logical_not(active))
    def _():
        y_ref[...] = jnp.zeros_like(y_ref)


def _experts(block_expert, n_used, buf, w1, w3, w2):
    n_rows, d = buf.shape
    hidden = w2.shape[1]
    nb = n_rows // MOE_BLOCK
    by_expert = lambda i, be, nu: (be[i], 0, 0)
    grid_spec = pltpu.PrefetchScalarGridSpec(
        num_scalar_prefetch=2, grid=(nb,),
        in_specs=[BS((MOE_BLOCK, d), lambda i, be, nu: (jnp.minimum(i, nu[0] - 1), 0)),
                  BS((1, d, hidden), by_expert), BS((1, d, hidden), by_expert), BS((1, hidden, d), by_expert)],
        out_specs=BS((MOE_BLOCK, d), lambda i, be, nu: (i, 0)),
        scratch_shapes=[pltpu.VMEM((d, 2 * hidden), BF16), pltpu.VMEM((hidden, d), BF16)])
    return pl.pallas_call(_expert_body, out_shape=SDS((n_rows, d), F32), grid_spec=grid_spec,
                          compiler_params=_cparams(1), name="moe_experts")(block_expert, n_used, buf, w1, w3, w2)


def _combine_body(d_ref, y_ref, x1_ref, g2_ref, gt_ref, o_ref, rows_ref, sem):
    n = x1_ref.shape[0]

    def issue(t, carry):
        for j in range(TOP_K):
            _row_copy(y_ref, d_ref[j, t], rows_ref.at[j], t, sem).start()
        return carry

    lax.fori_loop(0, n, issue, 0, unroll=ISSUE_UNROLL)
    for j in range(TOP_K):
        pltpu.make_async_copy(y_ref.at[pl.ds(0, n), :], rows_ref.at[j], sem).wait()
    gt = gt_ref[...]
    moe = gt[:, 0:1] * rows_ref[0] + gt[:, 1:2] * rows_ref[1]
    o_ref[...] = x1_ref[...] + g2_ref[0] * moe


def _combine(dest, y, x1, g2, gates_t, seq):
    n_tok, d = x1.shape
    tm = GATHER_TILE
    return pl.pallas_call(
        _combine_body, out_shape=SDS((n_tok, d), F32), grid=(n_tok // tm,),
        in_specs=[BS((TOP_K, tm), lambda i: (0, i), memory_space=pltpu.SMEM), BS(memory_space=pl.ANY),
                  BS((tm, d), lambda i: (i, 0)), BS((1, 1, d), lambda i: (i // (seq // tm), 0, 0)),
                  BS((tm, TOP_K), lambda i: (i, 0))],
        out_specs=BS((tm, d), lambda i: (i, 0)),
        scratch_shapes=[pltpu.VMEM((TOP_K, tm, d), F32), pltpu.SemaphoreType.DMA(())],
        compiler_params=_cparams(1), name="moe_combine",
    )(dest, y, x1, g2, gates_t)


def kernel(x, c, ctx, c_ctx, ada_w, ada_b, norm1_w, w_in, q_norm_w, k_norm_w, attn_sink, hy_conv_w, hy_conv_b, hy_w1,
           hy_b1, hy_w2, hy_b2, hy_w3, hy_sin_freq, hy_skip, group_norm_w, w_out, norm2_w, router_g_w, router_g_b,
           router_e_w, router_e_b, exp_w1, exp_w3, exp_w2):
    bsz, seq, d = x.shape
    assert ada_w.shape[0] == 1, "single-layer block"
    att_w = d // 2
    hy_w = d - att_w
    kv_w = N_KV_HEADS * HEAD_DIM
    n_heads = att_w // HEAD_DIM
    n_grp = hy_w // C_GROUP
    assert 2 * seq == FFT_MAJOR * FFT_MINOR and seq % ROW_TILE == 0 and hy_w % C_GROUP == 0
    assert w_in.shape[2] == att_w + 2 * kv_w + (HYENA_ORDER + 1) * hy_w

    pad = (-(bsz + 1)) % SUBLANES
    cc = jnp.concatenate([c, c_ctx[None, :], jnp.zeros((pad, d), F32)], axis=0)
    mod = _ada(cc, ada_w[0], ada_b[0])
    lat = [mod[:bsz, k * d:(k + 1) * d].reshape(bsz, 1, d) for k in range(6)]
    sh1, sc1, g1, sh2, sc2, g2 = lat
    csh1 = mod[bsz:bsz + 1, 0:d]
    csc1 = mod[bsz:bsz + 1, d:2 * d]

    w_in_bf = w_in[0].astype(BF16)
    qnw = jnp.tile(q_norm_w[0], n_heads).reshape(1, att_w)
    knw = jnp.tile(k_norm_w[0], N_KV_HEADS).reshape(1, kv_w)
    q, k, v, u = _inproj(x, sh1, sc1, norm1_w[0], w_in_bf, qnw, knw, att_w, kv_w)
    kx, vx = _ctxproj(ctx, csh1, csc1, norm1_w[0], w_in_bf[:, att_w:att_w + 2 * kv_w], knw, kv_w)

    gw = group_norm_w[0]
    att_n = _attention(attn_sink[0], q, k, v, kx, vx, gw[:att_w].reshape(1, att_w))

    hf = _filter_mlp(seq, hy_w1[0], hy_b1[0], hy_w2[0], hy_b2[0], hy_w3[0], hy_sin_freq[0], hy_w)
    spec = _spectrum(hf, seq, hy_w)
    cw = hy_conv_w[0].reshape(3, HYENA_ORDER + 1, n_grp, C_GROUP)
    cb = hy_conv_b[0].reshape(1, HYENA_ORDER + 1, n_grp, C_GROUP)
    taps = jnp.concatenate([cw, cb], axis=0)
    prm = jnp.zeros((HYENA_ORDER, n_grp, 16, C_GROUP), F32)
    for o in range(HYENA_ORDER):
        prm = prm.at[o, :, 4:8].set(jnp.transpose(taps[:, o + 1], (1, 0, 2)))
        prm = prm.at[o, :, 8].set(hy_skip[0, o].reshape(n_grp, C_GROUP))
    prm = prm.at[0, :, 0:4].set(jnp.transpose(taps[:, 0], (1, 0, 2)))
    z = _hyena_order(u, 0, u, n_grp, prm, 0, spec, True, n_grp)
    hy = _hyena_order(z, 0, u, 2 * n_grp, prm, 1, spec, False, n_grp)

    wr = jnp.zeros((ROUTER_ROWS, d), F32)
    wr = wr.at[0:N_GROUPS].set(router_g_w[0].T).at[EXPERT_ROW0:EXPERT_ROW0 + N_EXPERTS].set(router_e_w[0].T)
    br = jnp.zeros((ROUTER_ROWS, 1), F32)
    br = br.at[0:N_GROUPS, 0].set(router_g_b[0]).at[EXPERT_ROW0:EXPERT_ROW0 + N_EXPERTS, 0].set(router_e_b[0])
    wr_hi, wr_lo = _split(wr)
    x1, h2, ids, gates, counts = _outproj_route(att_n, hy, gw[att_w:].reshape(1, hy_w), w_out[0].astype(BF16), x,
                                                g1, sh2, sc2, norm2_w[0], wr_hi, wr_lo, br)

    n_tok = bsz * seq
    cnt = counts[:, 0].astype(I32)
    padded = (cnt + MOE_BLOCK - 1) // MOE_BLOCK * MOE_BLOCK
    pend = jnp.cumsum(padded)
    pstart = pend - padded
    e_iota = jnp.arange(N_EXPERTS, dtype=I32)
    is_e = ids[0:TOP_K, :, None] == e_iota
    dest = jnp.sum(jnp.where(is_e, pstart, 0), axis=-1) + ids[TOP_K:2 * TOP_K]
    n_blocks = (n_tok * TOP_K) // MOE_BLOCK + N_EXPERTS
    block_row0 = jnp.arange(n_blocks, dtype=I32) * MOE_BLOCK
    block_expert = jnp.minimum(jnp.sum((block_row0[:, None] >= pend[None, :]).astype(I32), axis=-1), N_EXPERTS - 1)
    n_used = (pend[-1:] // MOE_BLOCK).astype(I32)

    buf = _dispatch(pend.astype(I32), dest, h2, n_blocks * MOE_BLOCK)
    y = _experts(block_expert, n_used, buf, exp_w1[0], exp_w3[0], exp_w2[0])
    out = _combine(dest, y, x1.reshape(n_tok, d), g2, gates[0:TOP_K].T, seq)
    return out.reshape(bsz, seq, d)
```

```python
import functools
import math

import ml_dtypes
import numpy as np

import jax
import jax.numpy as jnp
from jax import lax
from jax.experimental import pallas as pl
from jax.experimental.pallas import tpu as pltpu

F32, BF16, I32 = jnp.float32, jnp.bfloat16, jnp.int32
SDS = jax.ShapeDtypeStruct
BS = pl.BlockSpec

HEAD_DIM = 64
N_KV_HEADS = 2
WINDOW = 128
Q_BLOCK = 128
GRID_W = 64
ROPE_THETA = 10000.0
EPS = 1e-6
NEG_INF = -1e30
LOG2E = math.log2(math.e)
HYENA_BANDS = 16
HYENA_ORDER = 2
DECAY_TARGET = 1e-2
FAST_DECAY_PCT = 0.3
SLOW_DECAY_PCT = 1.5
N_GROUPS = 4
EXPERTS_PER_GROUP = 8
N_EXPERTS = N_GROUPS * EXPERTS_PER_GROUP
TOP_K = 2

LANES = 128
SUBLANES = 8
VMEM_LIMIT = 60 * 1024 * 1024

ROW_TILE = 1024
MOE_BLOCK = 512
DISPATCH_TILE = 2048
COMBINE_TILE = 1024
FILTER_TILE = 512

FFT_MINOR = 128
FFT_MAJOR = 64
N_SLABS = FFT_MAJOR // 2 + 1


def _cparams(n_grid):
    return pltpu.CompilerParams(dimension_semantics=("arbitrary",) * n_grid, vmem_limit_bytes=VMEM_LIMIT)


def _mm(a, b):
    return jnp.dot(a, b, preferred_element_type=F32)


def _mm_nt(a, b):
    return lax.dot_general(a, b, (((1,), (1,)), ((), ())), preferred_element_type=F32)


def _split(x):
    hi = x.astype(BF16)
    lo = (x - hi.astype(F32)).astype(BF16)
    return hi, lo


def _mm_split(w_hi, w_lo, x, passes):
    if passes == 1:
        return _mm(w_hi, x.astype(BF16))
    x_hi, x_lo = _split(x)
    out = _mm(w_hi, x_hi) + _mm(w_hi, x_lo)
    if passes == 3:
        out = out + _mm(w_lo, x_hi)
    return out


def _mm3(a, b):
    a_hi, a_lo = _split(a)
    b_hi, b_lo = _split(b)
    return _mm(a_hi, b_hi) + _mm(a_hi, b_lo) + _mm(a_lo, b_hi)


def _silu(x):
    return x * (1.0 / (1.0 + jnp.exp(-x)))


def _rms(x):
    return x * lax.rsqrt(jnp.mean(x * x, axis=-1, keepdims=True) + EPS)


@functools.lru_cache(maxsize=None)
def _rope_tables(seq_len):
    pos = np.arange(seq_len)
    rows = (pos // GRID_W).astype(np.float64)
    cols = (pos % GRID_W).astype(np.float64)
    pairs = HEAD_DIM // 4
    inv_freq = ROPE_THETA ** (-np.arange(pairs, dtype=np.float64) / pairs)
    ang_r = rows[:, None] * inv_freq[None, :]
    ang_c = cols[:, None] * inv_freq[None, :]
    ang = np.concatenate([ang_r, ang_r, ang_c, ang_c], axis=1)
    sign = np.tile(np.concatenate([-np.ones(pairs), np.ones(pairs)]), 2)
    cos = np.cos(ang)
    sin = np.sin(ang) * sign[None, :]
    reps = LANES // HEAD_DIM
    return np.tile(cos, (1, reps)).astype(np.float32), np.tile(sin, (1, reps)).astype(np.float32)


@functools.lru_cache(maxsize=None)
def _block_diag_mean(width):
    h = np.arange(width) // HEAD_DIM
    return ((h[:, None] == h[None, :]).astype(np.float32) / HEAD_DIM).astype(ml_dtypes.bfloat16)


@functools.lru_cache(maxsize=None)
def _filter_features(seq_len, width):
    pos = np.arange(seq_len, dtype=np.float64)
    t = pos / seq_len
    bands = np.linspace(1e-4, HYENA_BANDS - 1, HYENA_BANDS)
    ang = (2.0 * math.pi / seq_len) * pos[:, None] * bands[None, :]
    feat = np.concatenate([t[:, None], np.cos(ang), -np.sin(ang)], axis=-1)
    out = np.zeros((seq_len, width), np.float32)
    out[:, :feat.shape[1]] = feat
    return out


@functools.lru_cache(maxsize=None)
def _decay_rates(width):
    max_decay = math.log(DECAY_TARGET) / FAST_DECAY_PCT
    min_decay = math.log(DECAY_TARGET) / SLOW_DECAY_PCT
    return np.abs(np.linspace(min_decay, max_decay, width)).astype(np.float32)


@functools.lru_cache(maxsize=None)
def _dft_tables():
    n = FFT_MAJOR * FFT_MINOR
    half = FFT_MAJOR // 2
    n2 = np.arange(half)
    a = np.zeros((FFT_MAJOR, half))
    a[:half] = np.cos(2 * np.pi * np.arange(half)[:, None] * n2[None, :] / FFT_MAJOR)
    a[half] = (-1.0) ** n2
    k2 = np.arange(1, half)
    a[half + 1:] = -np.sin(2 * np.pi * k2[:, None] * n2[None, :] / FFT_MAJOR)
    d = np.zeros((half, FFT_MAJOR))
    d[:, 0] = 1.0 / n
    d[:, 1:half] = 2 * np.cos(2 * np.pi * n2[:, None] * k2[None, :] / FFT_MAJOR) / n
    d[:, half] = (-1.0) ** n2 / n
    d[:, half + 1:] = -2 * np.sin(2 * np.pi * n2[:, None] * k2[None, :] / FFT_MAJOR) / n
    eye = np.eye(SUBLANES)
    ak = np.kron(a, eye)
    dk = np.kron(d, eye)
    s = np.arange(N_SLABS)[:, None, None]
    k1 = np.arange(FFT_MINOR)[None, :, None]
    n1 = np.arange(FFT_MINOR)[None, None, :]
    g = np.exp(-2j * np.pi * n1 * (FFT_MAJOR * k1 + s) / n)
    mf = np.block([[g.real, -g.imag], [g.imag, g.real]])
    gi = np.conj(g).transpose(0, 2, 1)
    mi = np.block([[gi.real, -gi.imag], [gi.imag, gi.real]])
    return {name: m.astype(np.float32) for name, m in (("ak", ak), ("dk", dk), ("mf", mf), ("mi", mi))}


def _dft_hi_lo(name):
    return _split(jnp.asarray(_dft_tables()[name]))


def _ada_body(c_ref, w_ref, b_ref, o_ref):
    o_ref[...] = _mm3(_silu(c_ref[...]), w_ref[...]) + b_ref[...]


def _ada(cc, ada_w, ada_b):
    rows, d = cc.shape
    n = ada_w.shape[1]
    tn = 1536
    return pl.pallas_call(
        _ada_body, out_shape=SDS((rows, n), F32), grid=(n // tn,),
        in_specs=[BS((rows, d), lambda j: (0, 0)), BS((d, tn), lambda j: (0, j)), BS((1, tn), lambda j: (0, j))],
        out_specs=BS((rows, tn), lambda j: (0, j)), compiler_params=_cparams(1), name="ada_mod",
    )(cc, ada_w, ada_b.reshape(1, n))


def _head_rms(t, bd):
    hi, lo = _split(t * t)
    return t * lax.rsqrt(_mm(hi, bd) + _mm(lo, bd) + EPS)


def _rope(t, cos, sin):
    n = t.shape[1]
    quarter = HEAD_DIM // 4
    lane = lax.broadcasted_iota(I32, t.shape, 1)
    up = pltpu.roll(t, n - quarter, axis=1)
    dn = pltpu.roll(t, quarter, axis=1)
    partner = jnp.where((lane & (2 * quarter - 1)) < quarter, up, dn)
    return t * cos + partner * sin


def _inproj_body(x_ref, sh_ref, sc_ref, nw_ref, w_ref, qnw_ref, knw_ref, cos_ref, sin_ref, bdq_ref, bdk_ref,
                 q_ref, k_ref, v_ref, u_ref, *, att_w, kv_w):
    h = _rms(x_ref[0]) * nw_ref[...]
    h = h * (1.0 + sc_ref[0]) + sh_ref[0]
    proj = _mm(h.astype(BF16), w_ref[...])
    cos, sin = cos_ref[...], sin_ref[...]
    reps = att_w // LANES
    cos_q = jnp.concatenate([cos] * reps, axis=1)
    sin_q = jnp.concatenate([sin] * reps, axis=1)
    q = _head_rms(proj[:, :att_w], bdq_ref[...]) * qnw_ref[...]
    q_ref[0] = (_rope(q, cos_q, sin_q) * (HEAD_DIM ** -0.5 * LOG2E)).astype(BF16)
    k = _head_rms(proj[:, att_w:att_w + kv_w], bdk_ref[...]) * knw_ref[...]
    k_ref[0] = _rope(k, cos, sin).astype(BF16)
    v_ref[0] = proj[:, att_w + kv_w:att_w + 2 * kv_w].astype(BF16)
    hy_off = att_w + 2 * kv_w
    for j in range(u_ref.shape[1]):
        u_ref[0, j] = proj[:, hy_off + LANES * j:hy_off + LANES * (j + 1)]


def _inproj(x, sh, sc, norm_w, w_bf, qnw, knw, att_w, kv_w):
    bsz, seq, d = x.shape
    tm = ROW_TILE
    hy_tiles = (w_bf.shape[1] - att_w - 2 * kv_w) // LANES
    cos, sin = _rope_tables(seq)
    row = lambda b, i: (b, i, 0)
    fixed2 = lambda b, i: (0, 0)
    per_b = lambda b, i: (b, 0, 0)
    return pl.pallas_call(
        functools.partial(_inproj_body, att_w=att_w, kv_w=kv_w),
        out_shape=(SDS((bsz, seq, att_w), BF16), SDS((bsz, seq, kv_w), BF16), SDS((bsz, seq, kv_w), BF16),
                   SDS((bsz, hy_tiles, seq, LANES), F32)),
        grid=(bsz, seq // tm),
        in_specs=[BS((1, tm, d), row), BS((1, 1, d), per_b), BS((1, 1, d), per_b), BS((1, d), fixed2),
                  BS(w_bf.shape, fixed2), BS((1, att_w), fixed2), BS((1, kv_w), fixed2),
                  BS((tm, LANES), lambda b, i: (i, 0)), BS((tm, LANES), lambda b, i: (i, 0)),
                  BS((att_w, att_w), fixed2), BS((kv_w, kv_w), fixed2)],
        out_specs=(BS((1, tm, att_w), row), BS((1, tm, kv_w), row), BS((1, tm, kv_w), row),
                   BS((1, hy_tiles, tm, LANES), lambda b, i: (b, 0, i, 0))),
        compiler_params=_cparams(2), name="in_proj",
    )(x, sh, sc, norm_w.reshape(1, d), w_bf, qnw, knw, jnp.asarray(cos), jnp.asarray(sin),
      jnp.asarray(_block_diag_mean(att_w)), jnp.asarray(_block_diag_mean(kv_w)))


def _ctxproj_body(x_ref, sh_ref, sc_ref, nw_ref, w_ref, knw_ref, bdk_ref, k_ref, v_ref, *, kv_w):
    h = _rms(x_ref[0]) * nw_ref[...]
    h = h * (1.0 + sc_ref[...]) + sh_ref[...]
    proj = _mm(h.astype(BF16), w_ref[...])
    k_ref[0] = (_head_rms(proj[:, :kv_w], bdk_ref[...]) * knw_ref[...]).astype(BF16)
    v_ref[0] = proj[:, kv_w:].astype(BF16)


def _ctxproj(ctx, sh, sc, norm_w, w_kv_bf, knw, kv_w):
    bsz, n_ctx, d = ctx.shape
    fixed2 = lambda b: (0, 0)
    row = lambda b: (b, 0, 0)
    return pl.pallas_call(
        functools.partial(_ctxproj_body, kv_w=kv_w),
        out_shape=(SDS((bsz, n_ctx, kv_w), BF16), SDS((bsz, n_ctx, kv_w), BF16)), grid=(bsz,),
        in_specs=[BS((1, n_ctx, d), row), BS((1, d), fixed2), BS((1, d), fixed2), BS((1, d), fixed2),
                  BS(w_kv_bf.shape, fixed2), BS((1, kv_w), fixed2), BS((kv_w, kv_w), fixed2)],
        out_specs=(BS((1, n_ctx, kv_w), row), BS((1, n_ctx, kv_w), row)),
        compiler_params=_cparams(1), name="ctx_proj",
    )(ctx, sh, sc, norm_w.reshape(1, d), w_kv_bf, knw, jnp.asarray(_block_diag_mean(kv_w)))


def _attn_body(sink_ref, q_ref, kp_ref, kc_ref, kn_ref, vp_ref, vc_ref, vn_ref, kx_ref, vx_ref, gw_ref, bias_ref,
               o_ref):
    n_ctx = kx_ref.shape[1]
    n_heads = q_ref.shape[2] // HEAD_DIM
    group = n_heads // N_KV_HEADS
    rows = group * Q_BLOCK
    bias = bias_ref[0]
    head_of_row = lax.broadcasted_iota(I32, (rows, 1), 0) // Q_BLOCK
    outs = []
    for g in range(N_KV_HEADS):
        sl = slice(g * HEAD_DIM, (g + 1) * HEAD_DIM)
        kb = jnp.concatenate([kx_ref[0][:, sl], kp_ref[0][:, sl], kc_ref[0][:, sl], kn_ref[0][:, sl]], axis=0)
        vb = jnp.concatenate([vx_ref[0][:, sl], vp_ref[0][:, sl], vc_ref[0][:, sl], vn_ref[0][:, sl]], axis=0)
        heads = range(g * group, (g + 1) * group)
        q4 = jnp.concatenate([q_ref[0][:, h * HEAD_DIM:(h + 1) * HEAD_DIM] for h in heads], axis=0)
        sink = jnp.zeros((rows, 1), F32)
        for hh, h in enumerate(heads):
            sink = jnp.where(head_of_row == hh, sink_ref[h] * LOG2E, sink)
        s = _mm_nt(q4, kb)
        s_ctx = s[:, :n_ctx]
        s_band = s[:, n_ctx:] + bias
        m = jnp.maximum(jnp.maximum(jnp.max(s_ctx, axis=-1, keepdims=True), jnp.max(s_band, axis=-1, keepdims=True)),
                        sink)
        p_ctx = jnp.exp2((s_ctx - m).astype(BF16))
        p_band = jnp.exp2((s_band - m).astype(BF16))
        denom = (jnp.sum(p_ctx.astype(F32), axis=-1, keepdims=True) + jnp.sum(p_band.astype(F32), axis=-1, keepdims=True)
                 + jnp.exp2(sink - m))
        o4 = (_mm(p_ctx, vb[:n_ctx]) + _mm(p_band, vb[n_ctx:])) / denom
        for hh in range(group):
            outs.append(o4[hh * Q_BLOCK:(hh + 1) * Q_BLOCK])
    att = jnp.concatenate(outs, axis=1)
    o_ref[0] = (_rms(att) * gw_ref[...]).astype(BF16)


@functools.lru_cache(maxsize=None)
def _band_bias(rows):
    r = (np.arange(rows) % Q_BLOCK)[:, None]
    j = np.arange(3 * Q_BLOCK)[None, :]
    in_band = (j >= r) & (j <= r + 2 * WINDOW)
    cases = (in_band, in_band & (j >= Q_BLOCK), in_band & (j < 2 * Q_BLOCK))
    return np.stack([np.where(c, 0.0, NEG_INF) for c in cases]).astype(np.float32)


def _attention(sink, q, k, v, kx, vx, gw):
    bsz, seq, att_w = q.shape
    kv_w = k.shape[2]
    n_ctx = kx.shape[1]
    nb = seq // Q_BLOCK
    cur = lambda b, i: (b, i, 0)
    prev = lambda b, i: (b, jnp.maximum(i - 1, 0), 0)
    nxt = lambda b, i: (b, jnp.minimum(i + 1, nb - 1), 0)
    per_b = lambda b, i: (b, 0, 0)
    kvb = (1, Q_BLOCK, kv_w)
    rows = (att_w // HEAD_DIM // N_KV_HEADS) * Q_BLOCK
    assert nb >= 2
    edge_case = lambda b, i: (jnp.where(i == 0, 1, jnp.where(i == nb - 1, 2, 0)), 0, 0)
    return pl.pallas_call(
        _attn_body, out_shape=SDS((bsz, seq, att_w), BF16), grid=(bsz, nb),
        in_specs=[BS(memory_space=pltpu.SMEM), BS((1, Q_BLOCK, att_w), cur),
                  BS(kvb, prev), BS(kvb, cur), BS(kvb, nxt), BS(kvb, prev), BS(kvb, cur), BS(kvb, nxt),
                  BS((1, n_ctx, kv_w), per_b), BS((1, n_ctx, kv_w), per_b), BS((1, att_w), lambda b, i: (0, 0)),
                  BS((1, rows, 3 * Q_BLOCK), edge_case)],
        out_specs=BS((1, Q_BLOCK, att_w), cur), compiler_params=_cparams(2), name="window_attn",
    )(sink, q, k, k, k, v, v, v, kx, vx, gw, jnp.asarray(_band_bias(rows)))


def _filter_body(f_ref, w1h, w1l, b1, w2h, w2l, b2, w3h, w3l, sf_ref, dl_ref, o_ref, *, hy_w):
    def mm_w(a, wh, wl):
        a_hi, a_lo = _split(a)
        return _mm(a_hi, wh[...]) + _mm(a_hi, wl[...]) + _mm(a_lo, wh[...])

    f = f_ref[...]
    h = jnp.sin(sf_ref[0:1, :] * (mm_w(f, w1h, w1l) + b1[...]))
    h = jnp.sin(sf_ref[1:2, :] * (mm_w(h, w2h, w2l) + b2[...]))
    h = mm_w(h, w3h, w3l)
    h = h * jnp.exp(-f[:, 0:1] * dl_ref[...])
    tl = f.shape[0]
    row = lax.broadcasted_iota(I32, h.shape, 0) + pl.program_id(0) * tl
    col = lax.broadcasted_iota(I32, h.shape, 1)
    is_bwd = ((col // hy_w) & 1) == 1
    h = jnp.where((row == 0) & is_bwd, 0.0, h)
    for j in range(o_ref.shape[0]):
        o_ref[j] = h[:, LANES * j:LANES * (j + 1)]


def _filter_mlp(seq, w1, b1, w2, b2, w3, sin_freq, hy_w):
    hid = w1.shape[1]
    fw = 64
    feat = jnp.asarray(_filter_features(seq, fw))
    w1p = jnp.zeros((fw, hid), F32).at[:w1.shape[0]].set(w1)
    n_out = w3.shape[1]
    delta = jnp.asarray(np.tile(_decay_rates(hy_w), n_out // hy_w)).reshape(1, n_out)
    tl = FILTER_TILE
    fixed = lambda i: (0, 0)
    ops = []
    for w in (w1p, w2, w3):
        ops.extend(_split(w))
    return pl.pallas_call(
        functools.partial(_filter_body, hy_w=hy_w),
        out_shape=SDS((n_out // LANES, seq, LANES), F32), grid=(seq // tl,),
        in_specs=[BS((tl, fw), lambda i: (i, 0)), BS((fw, hid), fixed), BS((fw, hid), fixed), BS((1, hid), fixed),
                  BS((hid, hid), fixed), BS((hid, hid), fixed), BS((1, hid), fixed),
                  BS((hid, n_out), fixed), BS((hid, n_out), fixed), BS((2, hid), fixed), BS((1, n_out), fixed)],
        out_specs=BS((n_out // LANES, tl, LANES), lambda i: (0, i, 0)),
        compiler_params=_cparams(1), name="hyena_filter_mlp",
    )(feat, ops[0], ops[1], b1.reshape(1, hid), ops[2], ops[3], b2.reshape(1, hid), ops[4], ops[5], sin_freq, delta)


C_TILES = 2
C_GROUP = C_TILES * LANES


def _load_cat(ref, row0, n_rows):
    return jnp.concatenate([ref[c, pl.ds(row0, n_rows), :] for c in range(C_TILES)], axis=1)


def _store_cat(ref, row0, n_rows, val):
    for c in range(C_TILES):
        ref[c, pl.ds(row0, n_rows), :] = val[:, c * LANES:(c + 1) * LANES]


def _fwd_major(src, p_ref, ak, passes):
    half = FFT_MAJOR // 2

    def body(g, carry):
        r0 = pl.multiple_of(g * SUBLANES, SUBLANES)
        st = jnp.concatenate([_load_cat(src, n2 * FFT_MINOR + r0, SUBLANES) for n2 in range(half)], axis=0)
        out = _mm_split(ak[0][...], ak[1][...] if passes == 3 else None, st, passes)
        for p in range(FFT_MAJOR):
            _store_cat(p_ref, p * FFT_MINOR + r0, SUBLANES, out[p * SUBLANES:(p + 1) * SUBLANES])
        return carry

    lax.fori_loop(0, FFT_MINOR // SUBLANES, body, 0, unroll=4)


def _inv_major(p_ref, dst, dk, passes):
    half = FFT_MAJOR // 2

    def body(g, carry):
        r0 = pl.multiple_of(g * SUBLANES, SUBLANES)
        st = jnp.concatenate([_load_cat(p_ref, p * FFT_MINOR + r0, SUBLANES) for p in range(FFT_MAJOR)], axis=0)
        out = _mm_split(dk[0][...], dk[1][...] if passes == 3 else None, st, passes)
        for n2 in range(half):
            _store_cat(dst, n2 * FFT_MINOR + r0, SUBLANES, out[n2 * SUBLANES:(n2 + 1) * SUBLANES])
        return carry

    lax.fori_loop(0, FFT_MINOR // SUBLANES, body, 0, unroll=4)


def _slab_spectrum(p_ref, mf, s, passes, real_slot=None):
    m = FFT_MINOR
    if real_slot is None:
        st = jnp.concatenate([_load_cat(p_ref, s * m, m), _load_cat(p_ref, (FFT_MAJOR // 2 + s) * m, m)], axis=0)
        x = _mm_split(mf[0][s], mf[1][s] if passes == 3 else None, st, passes)
    else:
        st = _load_cat(p_ref, real_slot * m, m)
        x = _mm_split(mf[0][s, :, 0:m], mf[1][s, :, 0:m] if passes == 3 else None, st, passes)
    return x[:m], x[m:]


def _spectrum_body(f_ref, b_ref, akh, akl, mfh, mfl, o_ref, p_ref):
    half = FFT_MAJOR // 2
    ak, mf = (akh, akl), (mfh, mfl)
    for is_bwd, src in ((False, f_ref), (True, b_ref)):
        _fwd_major(src, p_ref, ak, 3)

        def put(s, re, im, is_bwd=is_bwd):
            if is_bwd:
                o_ref[0, 0, s, 0] = o_ref[0, 0, s, 0] + re
                o_ref[0, 0, s, 1] = o_ref[0, 0, s, 1] - im
            else:
                o_ref[0, 0, s, 0] = re
                o_ref[0, 0, s, 1] = im

        put(0, *_slab_spectrum(p_ref, mf, 0, 3, real_slot=0))
        put(half, *_slab_spectrum(p_ref, mf, half, 3, real_slot=half))

        def body(s, carry):
            put(s, *_slab_spectrum(p_ref, mf, s, 3))
            return carry

        lax.fori_loop(1, half, body, 0, unroll=8)


def _spectrum(hf, seq, hy_w):
    ak, mf = _dft_hi_lo("ak"), _dft_hi_lo("mf")
    n_grp = hy_w // C_GROUP
    fixed2 = lambda o, c: (0, 0)
    fixed3 = lambda o, c: (0, 0, 0)
    one = pl.Buffered(1)
    return pl.pallas_call(
        _spectrum_body,
        out_shape=SDS((HYENA_ORDER, n_grp, N_SLABS, 2, FFT_MINOR, C_GROUP), F32), grid=(HYENA_ORDER, n_grp),
        in_specs=[BS((C_TILES, seq, LANES), lambda o, c: (o * 2 * n_grp + c, 0, 0), pipeline_mode=one),
                  BS((C_TILES, seq, LANES), lambda o, c: (o * 2 * n_grp + n_grp + c, 0, 0), pipeline_mode=one),
                  BS(ak[0].shape, fixed2, pipeline_mode=one), BS(ak[0].shape, fixed2, pipeline_mode=one),
                  BS(mf[0].shape, fixed3, pipeline_mode=one), BS(mf[0].shape, fixed3, pipeline_mode=one)],
        out_specs=BS((1, 1, N_SLABS, 2, FFT_MINOR, C_GROUP), lambda o, c: (o, c, 0, 0, 0, 0)),
        scratch_shapes=[pltpu.VMEM((C_TILES, FFT_MAJOR * FFT_MINOR, LANES), F32)],
        compiler_params=_cparams(2), name="hyena_spectrum",
    )(hf, hf, ak[0], ak[1], mf[0], mf[1])


CONV_CHUNK = 128
CONV_PASSES = 1


def _sconv_chunk(ref, c, r0, n_rows, seq, prm_ref, base):
    cl = slice(c * LANES, (c + 1) * LANES)
    cur = ref[c, pl.ds(r0, n_rows), :]
    row = lax.broadcasted_iota(I32, cur.shape, 0)
    before = ref[c, pl.ds(jnp.maximum(r0 - 1, 0), 1), :] * (r0 > 0).astype(F32)
    after = ref[c, pl.ds(jnp.minimum(r0 + n_rows, seq - 1), 1), :] * (r0 + n_rows < seq).astype(F32)
    prev = jnp.where(row == 0, before, pltpu.roll(cur, 1, axis=0))
    nxt = jnp.where(row == n_rows - 1, after, pltpu.roll(cur, n_rows - 1, axis=0))
    w = lambda k: prm_ref[0, 0, base + k:base + k + 1, cl]
    return w(3) + w(0) * prev + w(1) * cur + w(2) * nxt


def _conv_body(z_ref, x_ref, prm_ref, akh, dkh, mfh, mih, h_ref, o_ref, zs_ref, p_ref, *, conv_z):
    seq = z_ref.shape[2]
    half = FFT_MAJOR // 2
    m = FFT_MINOR
    zv, xv = z_ref.at[0], x_ref.at[0]
    ak, dk, mf, mi = (akh, None), (dkh, None), (mfh, None), (mih, None)
    n_chunks = seq // CONV_CHUNK

    if conv_z:
        def prep(i, carry):
            r0 = pl.multiple_of(i * CONV_CHUNK, CONV_CHUNK)
            for c in range(C_TILES):
                zs_ref[c, pl.ds(r0, CONV_CHUNK), :] = _sconv_chunk(zv, c, r0, CONV_CHUNK, seq, prm_ref, 0)
            return carry

        lax.fori_loop(0, n_chunks, prep, 0)
        src = zs_ref
    else:
        src = zv
    _fwd_major(src, p_ref, ak, CONV_PASSES)

    def slab(s, real_slot=None):
        xr, xi = _slab_spectrum(p_ref, mf, s, CONV_PASSES, real_slot)
        hr, hi = h_ref[0, 0, s, 0], h_ref[0, 0, s, 1]
        y = jnp.concatenate([xr * hr - xi * hi, xr * hi + xi * hr], axis=0)
        if real_slot is None:
            out = _mm_split(mi[0][s], None, y, CONV_PASSES)
            _store_cat(p_ref, s * m, m, out[:m])
            _store_cat(p_ref, (half + s) * m, m, out[m:])
        else:
            out = _mm_split(mi[0][s, 0:m, :], None, y, CONV_PASSES)
            _store_cat(p_ref, real_slot * m, m, out)

    slab(0, real_slot=0)
    slab(half, real_slot=half)

    def slab_loop(s, carry):
        slab(s)
        return carry

    lax.fori_loop(1, half, slab_loop, 0, unroll=8)
    _inv_major(p_ref, zs_ref, dk, CONV_PASSES)

    def fin(i, carry):
        r0 = pl.multiple_of(i * CONV_CHUNK, CONV_CHUNK)
        for c in range(C_TILES):
            cl = slice(c * LANES, (c + 1) * LANES)
            gate = _sconv_chunk(xv, c, r0, CONV_CHUNK, seq, prm_ref, 4)
            if conv_z:
                zval = _sconv_chunk(zv, c, r0, CONV_CHUNK, seq, prm_ref, 0)
            else:
                zval = zv[c, pl.ds(r0, CONV_CHUNK), :]
            skip = prm_ref[0, 0, 8:9, cl]
            o_ref[0, c, pl.ds(r0, CONV_CHUNK), :] = gate * (zs_ref[c, pl.ds(r0, CONV_CHUNK), :] + skip * zval)
        return carry

    lax.fori_loop(0, n_chunks, fin, 0, unroll=4)


def _hyena_order(z, z_grp0, x, x_grp0, prm, order, spec, conv_z, n_grp):
    bsz, _, seq, _ = z.shape
    ak, dk, mf, mi = (_dft_hi_lo(n)[0] for n in ("ak", "dk", "mf", "mi"))
    one = pl.Buffered(1)
    fixed2 = lambda c, b: (0, 0)
    fixed3 = lambda c, b: (0, 0, 0)
    blk = (1, C_TILES, seq, LANES)
    return pl.pallas_call(
        functools.partial(_conv_body, conv_z=conv_z),
        out_shape=SDS((bsz, n_grp * C_TILES, seq, LANES), F32), grid=(n_grp, bsz),
        in_specs=[BS(blk, lambda c, b: (b, z_grp0 + c, 0, 0)),
                  BS(blk, lambda c, b: (b, x_grp0 + c, 0, 0)),
                  BS((1, 1, 16, C_GROUP), lambda c, b: (order, c, 0, 0)),
                  BS(ak.shape, fixed2, pipeline_mode=one), BS(dk.shape, fixed2, pipeline_mode=one),
                  BS(mf.shape, fixed3, pipeline_mode=one), BS(mi.shape, fixed3, pipeline_mode=one),
                  BS((1, 1, N_SLABS, 2, FFT_MINOR, C_GROUP), lambda c, b: (order, c, 0, 0, 0, 0), pipeline_mode=one)],
        out_specs=BS(blk, lambda c, b: (b, c, 0, 0)),
        scratch_shapes=[pltpu.VMEM((C_TILES, seq, LANES), F32),
                        pltpu.VMEM((C_TILES, FFT_MAJOR * FFT_MINOR, LANES), F32)],
        compiler_params=_cparams(2), name=f"hyena_conv{order}",
    )(z, x, prm, ak, dk, mf, mi, spec)


ROUTER_ROWS = 48
EXPERT_ROW0 = 8


def _outproj_body(att_ref, hy_ref, gwh_ref, wo_ref, x_ref, g1_ref, sh2_ref, sc2_ref, n2w_ref, wrh_ref, wrl_ref,
                  br_ref, tri_ref, x1_ref, h2_ref, ids_ref, gates_ref, cnt_ref, carry_ref):
    @pl.when((pl.program_id(0) == 0) & (pl.program_id(1) == 0))
    def _():
        carry_ref[...] = jnp.zeros_like(carry_ref)

    hy = jnp.concatenate([hy_ref[0, j] for j in range(hy_ref.shape[1])], axis=1)
    hyn = _rms(hy) * gwh_ref[...]
    mix = _mm(jnp.concatenate([att_ref[0], hyn.astype(BF16)], axis=1), wo_ref[...])
    x1 = x_ref[0] + g1_ref[0] * mix
    x1_ref[0] = x1
    h2 = _rms(x1) * n2w_ref[...]
    h2 = h2 * (1.0 + sc2_ref[0]) + sh2_ref[0]
    h2_ref[...] = h2

    h_hi, h_lo = _split(h2)
    lg = _mm_nt(wrh_ref[...], h_hi) + _mm_nt(wrh_ref[...], h_lo) + _mm_nt(wrl_ref[...], h_hi) + br_ref[...]
    tm = lg.shape[1]
    gl = lg[0:N_GROUPS]
    el = lg[EXPERT_ROW0:EXPERT_ROW0 + N_EXPERTS]
    gmax = jnp.max(gl, axis=0, keepdims=True)
    rg = lax.broadcasted_iota(I32, gl.shape, 0).astype(F32)
    g_idx = jnp.min(jnp.where(gl == gmax, rg, float(N_GROUPS)), axis=0, keepdims=True)
    g_val = 1.0 / jnp.sum(jnp.exp(gl - gmax), axis=0, keepdims=True)
    re_i = lax.broadcasted_iota(I32, el.shape, 0)
    re = re_i.astype(F32)
    in_group = (re_i // EXPERTS_PER_GROUP).astype(F32) == g_idx
    elm = jnp.where(in_group, el, NEG_INF)
    m1 = jnp.max(elm, axis=0, keepdims=True)
    i1 = jnp.min(jnp.where(elm == m1, re, float(N_EXPERTS)), axis=0, keepdims=True)
    elm2 = jnp.where(re == i1, NEG_INF, elm)
    m2 = jnp.max(elm2, axis=0, keepdims=True)
    i2 = jnp.min(jnp.where(elm2 == m2, re, float(N_EXPERTS)), axis=0, keepdims=True)
    e2 = jnp.exp(m2 - m1)
    inv = g_val / (1.0 + e2)

    oh1 = (re == i1).astype(F32)
    oh2 = (re == i2).astype(F32)
    oh = oh1 + oh2
    base = _mm(oh.astype(BF16), tri_ref[...]) + carry_ref[:, 0:1]
    rank1 = jnp.sum(oh1 * base, axis=0, keepdims=True)
    rank2 = jnp.sum(oh2 * base, axis=0, keepdims=True)
    carry_ref[...] = carry_ref[...] + jnp.sum(oh, axis=1, keepdims=True)
    cnt_ref[...] = carry_ref[...]

    zi = jnp.zeros((1, tm), I32)
    for k, val in enumerate((i1.astype(I32), i2.astype(I32), rank1.astype(I32), rank2.astype(I32), zi, zi, zi, zi)):
        ids_ref[k:k + 1, :] = val
    zf = jnp.zeros((1, tm), F32)
    for k, val in enumerate((inv, inv * e2, zf, zf, zf, zf, zf, zf)):
        gates_ref[k:k + 1, :] = val


def _outproj_route(att_n, hy, gw_hy, wo_bf, x, g1, sh2, sc2, norm2_w, wr_hi, wr_lo, br):
    bsz, seq, d = x.shape
    tm = ROW_TILE
    n_tok = bsz * seq
    att_w = att_n.shape[2]
    hy_tiles = hy.shape[1]
    tri = jnp.asarray(np.triu(np.ones((tm, tm), np.float32), k=1).astype(ml_dtypes.bfloat16))
    row = lambda b, i: (b, i, 0)
    per_b = lambda b, i: (b, 0, 0)
    fixed2 = lambda b, i: (0, 0)
    tok = lambda b, i: (0, b * (seq // tm) + i)
    return pl.pallas_call(
        _outproj_body,
        out_shape=(SDS((bsz, seq, d), F32), SDS((n_tok, d), F32), SDS((SUBLANES, n_tok), I32),
                   SDS((SUBLANES, n_tok), F32), SDS((N_EXPERTS, LANES), F32)),
        grid=(bsz, seq // tm),
        in_specs=[BS((1, tm, att_w), row), BS((1, hy_tiles, tm, LANES), lambda b, i: (b, 0, i, 0)),
                  BS((1, hy_tiles * LANES), fixed2), BS(wo_bf.shape, fixed2), BS((1, tm, d), row),
                  BS((1, 1, d), per_b), BS((1, 1, d), per_b), BS((1, 1, d), per_b), BS((1, d), fixed2),
                  BS((ROUTER_ROWS, d), fixed2), BS((ROUTER_ROWS, d), fixed2), BS((ROUTER_ROWS, 1), fixed2),
                  BS((tm, tm), fixed2)],
        out_specs=(BS((1, tm, d), row), BS((tm, d), lambda b, i: (b * (seq // tm) + i, 0)),
                   BS((SUBLANES, tm), tok), BS((SUBLANES, tm), tok), BS((N_EXPERTS, LANES), fixed2)),
        scratch_shapes=[pltpu.VMEM((N_EXPERTS, LANES), F32)],
        compiler_params=_cparams(2), name="out_proj_router",
    )(att_n, hy, gw_hy, wo_bf, x, g1, sh2, sc2, norm2_w.reshape(1, d), wr_hi, wr_lo, br, tri)


ISSUE_UNROLL = 8


def _row_copy(src, src_row, dst, dst_row, sem):
    return pltpu.make_async_copy(src.at[pl.ds(src_row, 1), :], dst.at[pl.ds(dst_row, 1), :], sem)


def _dispatch_body(pend_ref, d_ref, h_ref, buf_ref, zero_ref, sem):
    n = h_ref.shape[0]

    @pl.when(pl.program_id(0) == 0)
    def _():
        zero_ref[...] = jnp.zeros_like(zero_ref)

        def tail_copy(e):
            row0 = pl.multiple_of(pend_ref[e] - MOE_BLOCK, MOE_BLOCK)
            return pltpu.make_async_copy(zero_ref, buf_ref.at[pl.ds(row0, MOE_BLOCK), :], sem)

        def non_empty(e):
            return pend_ref[e] > (pend_ref[e - 1] if e > 0 else 0)

        def spare_copy(i):
            row0 = pl.multiple_of(i * MOE_BLOCK, MOE_BLOCK)
            return pltpu.make_async_copy(zero_ref, buf_ref.at[pl.ds(row0, MOE_BLOCK), :], sem)

        first_spare = pend_ref[N_EXPERTS - 1] // MOE_BLOCK
        n_blocks = buf_ref.shape[0] // MOE_BLOCK
        for e in range(N_EXPERTS):
            @pl.when(non_empty(e))
            def _(e=e):
                tail_copy(e).start()
        lax.fori_loop(first_spare, n_blocks, lambda i, c: (spare_copy(i).start(), c)[1], 0)
        for e in range(N_EXPERTS):
            @pl.when(non_empty(e))
            def _(e=e):
                tail_copy(e).wait()
        lax.fori_loop(first_spare, n_blocks, lambda i, c: (spare_copy(i).wait(), c)[1], 0)

    def issue(t, carry):
        for j in range(TOP_K):
            _row_copy(h_ref, t, buf_ref, d_ref[j, t], sem).start()
        return carry

    lax.fori_loop(0, n, issue, 0, unroll=ISSUE_UNROLL)
    for j in range(TOP_K):
        pltpu.make_async_copy(h_ref, buf_ref.at[pl.ds(0, n), :], sem).wait()


def _dispatch(pend, dest, h2, n_rows):
    n_tok, d = h2.shape
    tm = DISPATCH_TILE
    grid_spec = pltpu.PrefetchScalarGridSpec(
        num_scalar_prefetch=1, grid=(n_tok // tm,),
        in_specs=[BS((TOP_K, tm), lambda i, pend: (0, i), memory_space=pltpu.SMEM),
                  BS((tm, d), lambda i, pend: (i, 0))],
        out_specs=BS(memory_space=pl.ANY),
        scratch_shapes=[pltpu.VMEM((MOE_BLOCK, d), F32), pltpu.SemaphoreType.DMA(())])
    return pl.pallas_call(_dispatch_body, out_shape=SDS((n_rows, d), F32), grid_spec=grid_spec,
                          compiler_params=_cparams(1), name="moe_dispatch")(pend, dest, h2)


def _expert_body(be_ref, nu_ref, x_ref, w1_ref, w3_ref, w2_ref, y_ref, w13_s, w2_s):
    i = pl.program_id(0)
    hidden = w2_ref.shape[1]
    active = i < nu_ref[0]
    new_expert = jnp.logical_or(i == 0, be_ref[i] != be_ref[jnp.maximum(i - 1, 0)])

    @pl.when(jnp.logical_and(active, new_expert))
    def _():
        w13_s[:, :hidden] = w1_ref[0].astype(BF16)
        w13_s[:, hidden:] = w3_ref[0].astype(BF16)
        w2_s[...] = w2_ref[0].astype(BF16)

    @pl.when(active)
    def _():
        h = _mm(x_ref[...].astype(BF16), w13_s[...])
        act = _silu(h[:, :hidden]) * h[:, hidden:]
        y_ref[...] = _mm(act.astype(BF16), w2_s[...])

    @pl.when(jnp.logical_not(active))
    def _():
        y_ref[...] = jnp.zeros_like(y_ref)


def _experts(block_expert, n_used, buf, w1, w3, w2):
    n_rows, d = buf.shape
    hidden = w2.shape[1]
    nb = n_rows // MOE_BLOCK
    by_expert = lambda i, be, nu: (be[i], 0, 0)
    grid_spec = pltpu.PrefetchScalarGridSpec(
        num_scalar_prefetch=2, grid=(nb,),
        in_specs=[BS((MOE_BLOCK, d), lambda i, be, nu: (jnp.minimum(i, nu[0] - 1), 0)),
                  BS((1, d, hidden), by_expert), BS((1, d, hidden), by_expert), BS((1, hidden, d), by_expert)],
        out_specs=BS((MOE_BLOCK, d), lambda i, be, nu: (i, 0)),
        scratch_shapes=[pltpu.VMEM((d, 2 * hidden), BF16), pltpu.VMEM((hidden, d), BF16)])
    return pl.pallas_call(_expert_body, out_shape=SDS((n_rows, d), F32), grid_spec=grid_spec,
                          compiler_params=_cparams(1), name="moe_experts")(block_expert, n_used, buf, w1, w3, w2)


def _combine_body(d_ref, y_ref, x1_ref, g2_ref, gt_ref, o_ref, rows_ref, sem):
    n = x1_ref.shape[0]

    def issue(t, carry):
        for j in range(TOP_K):
            _row_copy(y_ref, d_ref[j, t], rows_ref.at[j], t, sem).start()
        return carry

    lax.fori_loop(0, n, issue, 0, unroll=ISSUE_UNROLL)
    for j in range(TOP_K):
        pltpu.make_async_copy(y_ref.at[pl.ds(0, n), :], rows_ref.at[j], sem).wait()
    gt = gt_ref[...]
    moe = gt[:, 0:1] * rows_ref[0] + gt[:, 1:2] * rows_ref[1]
    o_ref[...] = x1_ref[...] + g2_ref[0] * moe


def _combine(dest, y, x1, g2, gates_t, seq):
    n_tok, d = x1.shape
    tm = COMBINE_TILE
    return pl.pallas_call(
        _combine_body, out_shape=SDS((n_tok, d), F32), grid=(n_tok // tm,),
        in_specs=[BS((TOP_K, tm), lambda i: (0, i), memory_space=pltpu.SMEM), BS(memory_space=pl.ANY),
                  BS((tm, d), lambda i: (i, 0)), BS((1, 1, d), lambda i: (i // (seq // tm), 0, 0)),
                  BS((tm, TOP_K), lambda i: (i, 0))],
        out_specs=BS((tm, d), lambda i: (i, 0)),
        scratch_shapes=[pltpu.VMEM((TOP_K, tm, d), F32), pltpu.SemaphoreType.DMA(())],
        compiler_params=_cparams(1), name="moe_combine",
    )(dest, y, x1, g2, gates_t)


def kernel(x, c, ctx, c_ctx, ada_w, ada_b, norm1_w, w_in, q_norm_w, k_norm_w, attn_sink, hy_conv_w, hy_conv_b, hy_w1,
           hy_b1, hy_w2, hy_b2, hy_w3, hy_sin_freq, hy_skip, group_norm_w, w_out, norm2_w, router_g_w, router_g_b,
           router_e_w, router_e_b, exp_w1, exp_w3, exp_w2):
    bsz, seq, d = x.shape
    assert ada_w.shape[0] == 1, "single-layer block"
    att_w = d // 2
    hy_w = d - att_w
    kv_w = N_KV_HEADS * HEAD_DIM
    n_heads = att_w // HEAD_DIM
    n_grp = hy_w // C_GROUP
    assert 2 * seq == FFT_MAJOR * FFT_MINOR and seq % ROW_TILE == 0 and hy_w % C_GROUP == 0
    assert w_in.shape[2] == att_w + 2 * kv_w + (HYENA_ORDER + 1) * hy_w

    pad = (-(bsz + 1)) % SUBLANES
    cc = jnp.concatenate([c, c_ctx[None, :], jnp.zeros((pad, d), F32)], axis=0)
    mod = _ada(cc, ada_w[0], ada_b[0])
    lat = [mod[:bsz, k * d:(k + 1) * d].reshape(bsz, 1, d) for k in range(6)]
    sh1, sc1, g1, sh2, sc2, g2 = lat
    csh1 = mod[bsz:bsz + 1, 0:d]
    csc1 = mod[bsz:bsz + 1, d:2 * d]

    w_in_bf = w_in[0].astype(BF16)
    qnw = jnp.tile(q_norm_w[0], n_heads).reshape(1, att_w)
    knw = jnp.tile(k_norm_w[0], N_KV_HEADS).reshape(1, kv_w)
    q, k, v, u = _inproj(x, sh1, sc1, norm1_w[0], w_in_bf, qnw, knw, att_w, kv_w)
    kx, vx = _ctxproj(ctx, csh1, csc1, norm1_w[0], w_in_bf[:, att_w:att_w + 2 * kv_w], knw, kv_w)

    gw = group_norm_w[0]
    att_n = _attention(attn_sink[0], q, k, v, kx, vx, gw[:att_w].reshape(1, att_w))

    hf = _filter_mlp(seq, hy_w1[0], hy_b1[0], hy_w2[0], hy_b2[0], hy_w3[0], hy_sin_freq[0], hy_w)
    spec = _spectrum(hf, seq, hy_w)
    cw = hy_conv_w[0].reshape(3, HYENA_ORDER + 1, n_grp, C_GROUP)
    cb = hy_conv_b[0].reshape(1, HYENA_ORDER + 1, n_grp, C_GROUP)
    taps = jnp.concatenate([cw, cb], axis=0)
    prm = jnp.zeros((HYENA_ORDER, n_grp, 16, C_GROUP), F32)
    for o in range(HYENA_ORDER):
        prm = prm.at[o, :, 4:8].set(jnp.transpose(taps[:, o + 1], (1, 0, 2)))
        prm = prm.at[o, :, 8].set(hy_skip[0, o].reshape(n_grp, C_GROUP))
    prm = prm.at[0, :, 0:4].set(jnp.transpose(taps[:, 0], (1, 0, 2)))
    z = _hyena_order(u, 0, u, n_grp, prm, 0, spec, True, n_grp)
    hy = _hyena_order(z, 0, u, 2 * n_grp, prm, 1, spec, False, n_grp)

    wr = jnp.zeros((ROUTER_ROWS, d), F32)
    wr = wr.at[0:N_GROUPS].set(router_g_w[0].T).at[EXPERT_ROW0:EXPERT_ROW0 + N_EXPERTS].set(router_e_w[0].T)
    br = jnp.zeros((ROUTER_ROWS, 1), F32)
    br = br.at[0:N_GROUPS, 0].set(router_g_b[0]).at[EXPERT_ROW0:EXPERT_ROW0 + N_EXPERTS, 0].set(router_e_b[0])
    wr_hi, wr_lo = _split(wr)
    x1, h2, ids, gates, counts = _outproj_route(att_n, hy, gw[att_w:].reshape(1, hy_w), w_out[0].astype(BF16), x,
                                                g1, sh2, sc2, norm2_w[0], wr_hi, wr_lo, br)

    n_tok = bsz * seq
    cnt = counts[:, 0].astype(I32)
    padded = (cnt + MOE_BLOCK - 1) // MOE_BLOCK * MOE_BLOCK
    pend = jnp.cumsum(padded)
    pstart = pend - padded
    e_iota = jnp.arange(N_EXPERTS, dtype=I32)
    is_e = ids[0:TOP_K, :, None] == e_iota
    dest = jnp.sum(jnp.where(is_e, pstart, 0), axis=-1) + ids[TOP_K:2 * TOP_K]
    n_blocks = (n_tok * TOP_K) // MOE_BLOCK + N_EXPERTS
    block_row0 = jnp.arange(n_blocks, dtype=I32) * MOE_BLOCK
    block_expert = jnp.minimum(jnp.sum((block_row0[:, None] >= pend[None, :]).astype(I32), axis=-1), N_EXPERTS - 1)
    n_used = (pend[-1:] // MOE_BLOCK).astype(I32)

    buf = _dispatch(pend.astype(I32), dest, h2, n_blocks * MOE_BLOCK)
    y = _experts(block_expert, n_used, buf, exp_w1[0], exp_w3[0], exp_w2[0])
    out = _combine(dest, y, x1.reshape(n_tok, d), g2, gates[0:TOP_K].T, seq)
    return out.reshape(bsz, seq, d)
```

```python
import functools
import math

import ml_dtypes
import numpy as np

import jax
import jax.numpy as jnp
from jax import lax
from jax.experimental import pallas as pl
from jax.experimental.pallas import tpu as pltpu

F32, BF16, I32 = jnp.float32, jnp.bfloat16, jnp.int32
SDS = jax.ShapeDtypeStruct
BS = pl.BlockSpec

HEAD_DIM = 64
N_KV_HEADS = 2
WINDOW = 128
Q_BLOCK = 128
GRID_W = 64
ROPE_THETA = 10000.0
EPS = 1e-6
NEG_INF = -1e30
LOG2E = math.log2(math.e)
HYENA_BANDS = 16
HYENA_ORDER = 2
DECAY_TARGET = 1e-2
FAST_DECAY_PCT = 0.3
SLOW_DECAY_PCT = 1.5
N_GROUPS = 4
EXPERTS_PER_GROUP = 8
N_EXPERTS = N_GROUPS * EXPERTS_PER_GROUP
TOP_K = 2

LANES = 128
SUBLANES = 8
VMEM_LIMIT = 60 * 1024 * 1024

ROW_TILE = 1024
MOE_BLOCK = 512
DISPATCH_TILE = 2048
COMBINE_TILE = 1024
FILTER_TILE = 512

FFT_MINOR = 128
FFT_MAJOR = 64
N_SLABS = FFT_MAJOR // 2 + 1


def _cparams(n_grid):
    return pltpu.CompilerParams(dimension_semantics=("arbitrary",) * n_grid, vmem_limit_bytes=VMEM_LIMIT)


def _mm(a, b):
    return jnp.dot(a, b, preferred_element_type=F32)


def _mm_nt(a, b):
    return lax.dot_general(a, b, (((1,), (1,)), ((), ())), preferred_element_type=F32)


def _split(x):
    hi = x.astype(BF16)
    lo = (x - hi.astype(F32)).astype(BF16)
    return hi, lo


def _mm_split(w_hi, w_lo, x, passes):
    if passes == 1:
        return _mm(w_hi, x.astype(BF16))
    x_hi, x_lo = _split(x)
    out = _mm(w_hi, x_hi) + _mm(w_hi, x_lo)
    if passes == 3:
        out = out + _mm(w_lo, x_hi)
    return out


def _mm3(a, b):
    a_hi, a_lo = _split(a)
    b_hi, b_lo = _split(b)
    return _mm(a_hi, b_hi) + _mm(a_hi, b_lo) + _mm(a_lo, b_hi)


def _silu(x):
    return x * (1.0 / (1.0 + jnp.exp(-x)))


def _rms(x):
    return x * lax.rsqrt(jnp.mean(x * x, axis=-1, keepdims=True) + EPS)


@functools.lru_cache(maxsize=None)
def _rope_tables(seq_len):
    pos = np.arange(seq_len)
    rows = (pos // GRID_W).astype(np.float64)
    cols = (pos % GRID_W).astype(np.float64)
    pairs = HEAD_DIM // 4
    inv_freq = ROPE_THETA ** (-np.arange(pairs, dtype=np.float64) / pairs)
    ang_r = rows[:, None] * inv_freq[None, :]
    ang_c = cols[:, None] * inv_freq[None, :]
    ang = np.concatenate([ang_r, ang_r, ang_c, ang_c], axis=1)
    sign = np.tile(np.concatenate([-np.ones(pairs), np.ones(pairs)]), 2)
    cos = np.cos(ang)
    sin = np.sin(ang) * sign[None, :]
    reps = LANES // HEAD_DIM
    return np.tile(cos, (1, reps)).astype(np.float32), np.tile(sin, (1, reps)).astype(np.float32)


@functools.lru_cache(maxsize=None)
def _block_diag_mean(width):
    h = np.arange(width) // HEAD_DIM
    return ((h[:, None] == h[None, :]).astype(np.float32) / HEAD_DIM).astype(ml_dtypes.bfloat16)


@functools.lru_cache(maxsize=None)
def _filter_features(seq_len, width):
    pos = np.arange(seq_len, dtype=np.float64)
    t = pos / seq_len
    bands = np.linspace(1e-4, HYENA_BANDS - 1, HYENA_BANDS)
    ang = (2.0 * math.pi / seq_len) * pos[:, None] * bands[None, :]
    feat = np.concatenate([t[:, None], np.cos(ang), -np.sin(ang)], axis=-1)
    out = np.zeros((seq_len, width), np.float32)
    out[:, :feat.shape[1]] = feat
    return out


@functools.lru_cache(maxsize=None)
def _decay_rates(width):
    max_decay = math.log(DECAY_TARGET) / FAST_DECAY_PCT
    min_decay = math.log(DECAY_TARGET) / SLOW_DECAY_PCT
    return np.abs(np.linspace(min_decay, max_decay, width)).astype(np.float32)


@functools.lru_cache(maxsize=None)
def _dft_tables():
    n = FFT_MAJOR * FFT_MINOR
    half = FFT_MAJOR // 2
    n2 = np.arange(half)
    a = np.zeros((FFT_MAJOR, half))
    a[:half] = np.cos(2 * np.pi * np.arange(half)[:, None] * n2[None, :] / FFT_MAJOR)
    a[half] = (-1.0) ** n2
    k2 = np.arange(1, half)
    a[half + 1:] = -np.sin(2 * np.pi * k2[:, None] * n2[None, :] / FFT_MAJOR)
    d = np.zeros((half, FFT_MAJOR))
    d[:, 0] = 1.0 / n
    d[:, 1:half] = 2 * np.cos(2 * np.pi * n2[:, None] * k2[None, :] / FFT_MAJOR) / n
    d[:, half] = (-1.0) ** n2 / n
    d[:, half + 1:] = -2 * np.sin(2 * np.pi * n2[:, None] * k2[None, :] / FFT_MAJOR) / n
    eye = np.eye(SUBLANES)
    ak = np.kron(a, eye)
    dk = np.kron(d, eye)
    s = np.arange(N_SLABS)[:, None, None]
    k1 = np.arange(FFT_MINOR)[None, :, None]
    n1 = np.arange(FFT_MINOR)[None, None, :]
    g = np.exp(-2j * np.pi * n1 * (FFT_MAJOR * k1 + s) / n)
    mf = np.block([[g.real, -g.imag], [g.imag, g.real]])
    gi = np.conj(g).transpose(0, 2, 1)
    mi = np.block([[gi.real, -gi.imag], [gi.imag, gi.real]])
    return {name: m.astype(np.float32) for name, m in (("ak", ak), ("dk", dk), ("mf", mf), ("mi", mi))}


def _dft_hi_lo(name):
    return _split(jnp.asarray(_dft_tables()[name]))


def _ada_body(c_ref, w_ref, b_ref, o_ref):
    o_ref[...] = _mm3(_silu(c_ref[...]), w_ref[...]) + b_ref[...]


def _ada(cc, ada_w, ada_b):
    rows, d = cc.shape
    n = ada_w.shape[1]
    tn = 1536
    return pl.pallas_call(
        _ada_body, out_shape=SDS((rows, n), F32), grid=(n // tn,),
        in_specs=[BS((rows, d), lambda j: (0, 0)), BS((d, tn), lambda j: (0, j)), BS((1, tn), lambda j: (0, j))],
        out_specs=BS((rows, tn), lambda j: (0, j)), compiler_params=_cparams(1), name="ada_mod",
    )(cc, ada_w, ada_b.reshape(1, n))


def _head_rms(t, bd):
    hi, lo = _split(t * t)
    return t * lax.rsqrt(_mm(hi, bd) + _mm(lo, bd) + EPS)


def _rope(t, cos, sin):
    n = t.shape[1]
    quarter = HEAD_DIM // 4
    lane = lax.broadcasted_iota(I32, t.shape, 1)
    up = pltpu.roll(t, n - quarter, axis=1)
    dn = pltpu.roll(t, quarter, axis=1)
    partner = jnp.where((lane & (2 * quarter - 1)) < quarter, up, dn)
    return t * cos + partner * sin


def _inproj_body(x_ref, sh_ref, sc_ref, nw_ref, w_ref, qnw_ref, knw_ref, cos_ref, sin_ref, bdq_ref, bdk_ref,
                 q_ref, k_ref, v_ref, u_ref, *, att_w, kv_w):
    h = _rms(x_ref[0]) * nw_ref[...]
    h = h * (1.0 + sc_ref[0]) + sh_ref[0]
    proj = _mm(h.astype(BF16), w_ref[...])
    cos, sin = cos_ref[...], sin_ref[...]
    reps = att_w // LANES
    cos_q = jnp.concatenate([cos] * reps, axis=1)
    sin_q = jnp.concatenate([sin] * reps, axis=1)
    q = _head_rms(proj[:, :att_w], bdq_ref[...]) * qnw_ref[...]
    q_ref[0] = (_rope(q, cos_q, sin_q) * (HEAD_DIM ** -0.5 * LOG2E)).astype(BF16)
    k = _head_rms(proj[:, att_w:att_w + kv_w], bdk_ref[...]) * knw_ref[...]
    k_ref[0] = _rope(k, cos, sin).astype(BF16)
    v_ref[0] = proj[:, att_w + kv_w:att_w + 2 * kv_w].astype(BF16)
    hy_off = att_w + 2 * kv_w
    for j in range(u_ref.shape[1]):
        u_ref[0, j] = proj[:, hy_off + LANES * j:hy_off + LANES * (j + 1)]


def _inproj(x, sh, sc, norm_w, w_bf, qnw, knw, att_w, kv_w):
    bsz, seq, d = x.shape
    tm = ROW_TILE
    hy_tiles = (w_bf.shape[1] - att_w - 2 * kv_w) // LANES
    cos, sin = _rope_tables(seq)
    row = lambda b, i: (b, i, 0)
    fixed2 = lambda b, i: (0, 0)
    per_b = lambda b, i: (b, 0, 0)
    return pl.pallas_call(
        functools.partial(_inproj_body, att_w=att_w, kv_w=kv_w),
        out_shape=(SDS((bsz, seq, att_w), BF16), SDS((bsz, seq, kv_w), BF16), SDS((bsz, seq, kv_w), BF16),
                   SDS((bsz, hy_tiles, seq, LANES), F32)),
        grid=(bsz, seq // tm),
        in_specs=[BS((1, tm, d), row), BS((1, 1, d), per_b), BS((1, 1, d), per_b), BS((1, d), fixed2),
                  BS(w_bf.shape, fixed2), BS((1, att_w), fixed2), BS((1, kv_w), fixed2),
                  BS((tm, LANES), lambda b, i: (i, 0)), BS((tm, LANES), lambda b, i: (i, 0)),
                  BS((att_w, att_w), fixed2), BS((kv_w, kv_w), fixed2)],
        out_specs=(BS((1, tm, att_w), row), BS((1, tm, kv_w), row), BS((1, tm, kv_w), row),
                   BS((1, hy_tiles, tm, LANES), lambda b, i: (b, 0, i, 0))),
        compiler_params=_cparams(2), name="in_proj",
    )(x, sh, sc, norm_w.reshape(1, d), w_bf, qnw, knw, jnp.asarray(cos), jnp.asarray(sin),
      jnp.asarray(_block_diag_mean(att_w)), jnp.asarray(_block_diag_mean(kv_w)))


def _ctxproj_body(x_ref, sh_ref, sc_ref, nw_ref, w_ref, knw_ref, bdk_ref, k_ref, v_ref, *, kv_w):
    h = _rms(x_ref[0]) * nw_ref[...]
    h = h * (1.0 + sc_ref[...]) + sh_ref[...]
    proj = _mm(h.astype(BF16), w_ref[...])
    k_ref[0] = (_head_rms(proj[:, :kv_w], bdk_ref[...]) * knw_ref[...]).astype(BF16)
    v_ref[0] = proj[:, kv_w:].astype(BF16)


def _ctxproj(ctx, sh, sc, norm_w, w_kv_bf, knw, kv_w):
    bsz, n_ctx, d = ctx.shape
    fixed2 = lambda b: (0, 0)
    row = lambda b: (b, 0, 0)
    return pl.pallas_call(
        functools.partial(_ctxproj_body, kv_w=kv_w),
        out_shape=(SDS((bsz, n_ctx, kv_w), BF16), SDS((bsz, n_ctx, kv_w), BF16)), grid=(bsz,),
        in_specs=[BS((1, n_ctx, d), row), BS((1, d), fixed2), BS((1, d), fixed2), BS((1, d), fixed2),
                  BS(w_kv_bf.shape, fixed2), BS((1, kv_w), fixed2), BS((kv_w, kv_w), fixed2)],
        out_specs=(BS((1, n_ctx, kv_w), row), BS((1, n_ctx, kv_w), row)),
        compiler_params=_cparams(1), name="ctx_proj",
    )(ctx, sh, sc, norm_w.reshape(1, d), w_kv_bf, knw, jnp.asarray(_block_diag_mean(kv_w)))


def _attn_body(sink_ref, q_ref, kp_ref, kc_ref, kn_ref, vp_ref, vc_ref, vn_ref, kx_ref, vx_ref, gw_ref, bias_ref,
               o_ref):
    n_ctx = kx_ref.shape[1]
    n_heads = q_ref.shape[2] // HEAD_DIM
    group = n_heads // N_KV_HEADS
    rows = group * Q_BLOCK
    bias = bias_ref[0]
    head_of_row = lax.broadcasted_iota(I32, (rows, 1), 0) // Q_BLOCK
    outs = []
    for g in range(N_KV_HEADS):
        sl = slice(g * HEAD_DIM, (g + 1) * HEAD_DIM)
        kb = jnp.concatenate([kx_ref[0][:, sl], kp_ref[0][:, sl], kc_ref[0][:, sl], kn_ref[0][:, sl]], axis=0)
        vb = jnp.concatenate([vx_ref[0][:, sl], vp_ref[0][:, sl], vc_ref[0][:, sl], vn_ref[0][:, sl]], axis=0)
        heads = range(g * group, (g + 1) * group)
        q4 = jnp.concatenate([q_ref[0][:, h * HEAD_DIM:(h + 1) * HEAD_DIM] for h in heads], axis=0)
        sink = jnp.zeros((rows, 1), F32)
        for hh, h in enumerate(heads):
            sink = jnp.where(head_of_row == hh, sink_ref[h] * LOG2E, sink)
        s = _mm_nt(q4, kb)
        s_ctx = s[:, :n_ctx]
        s_band = s[:, n_ctx:] + bias
        m = jnp.maximum(jnp.maximum(jnp.max(s_ctx, axis=-1, keepdims=True), jnp.max(s_band, axis=-1, keepdims=True)),
                        sink)
        p_ctx = jnp.exp2((s_ctx - m).astype(BF16))
        p_band = jnp.exp2((s_band - m).astype(BF16))
        denom = (jnp.sum(p_ctx.astype(F32), axis=-1, keepdims=True) + jnp.sum(p_band.astype(F32), axis=-1, keepdims=True)
                 + jnp.exp2(sink - m))
        o4 = (_mm(p_ctx, vb[:n_ctx]) + _mm(p_band, vb[n_ctx:])) / denom
        for hh in range(group):
            outs.append(o4[hh * Q_BLOCK:(hh + 1) * Q_BLOCK])
    att = jnp.concatenate(outs, axis=1)
    o_ref[0] = (_rms(att) * gw_ref[...]).astype(BF16)


@functools.lru_cache(maxsize=None)
def _band_bias(rows):
    r = (np.arange(rows) % Q_BLOCK)[:, None]
    j = np.arange(3 * Q_BLOCK)[None, :]
    in_band = (j >= r) & (j <= r + 2 * WINDOW)
    cases = (in_band, in_band & (j >= Q_BLOCK), in_band & (j < 2 * Q_BLOCK))
    return np.stack([np.where(c, 0.0, NEG_INF) for c in cases]).astype(np.float32)


def _attention(sink, q, k, v, kx, vx, gw):
    bsz, seq, att_w = q.shape
    kv_w = k.shape[2]
    n_ctx = kx.shape[1]
    nb = seq // Q_BLOCK
    cur = lambda b, i: (b, i, 0)
    prev = lambda b, i: (b, jnp.maximum(i - 1, 0), 0)
    nxt = lambda b, i: (b, jnp.minimum(i + 1, nb - 1), 0)
    per_b = lambda b, i: (b, 0, 0)
    kvb = (1, Q_BLOCK, kv_w)
    rows = (att_w // HEAD_DIM // N_KV_HEADS) * Q_BLOCK
    assert nb >= 2
    edge_case = lambda b, i: (jnp.where(i == 0, 1, jnp.where(i == nb - 1, 2, 0)), 0, 0)
    return pl.pallas_call(
        _attn_body, out_shape=SDS((bsz, seq, att_w), BF16), grid=(bsz, nb),
        in_specs=[BS(memory_space=pltpu.SMEM), BS((1, Q_BLOCK, att_w), cur),
                  BS(kvb, prev), BS(kvb, cur), BS(kvb, nxt), BS(kvb, prev), BS(kvb, cur), BS(kvb, nxt),
                  BS((1, n_ctx, kv_w), per_b), BS((1, n_ctx, kv_w), per_b), BS((1, att_w), lambda b, i: (0, 0)),
                  BS((1, rows, 3 * Q_BLOCK), edge_case)],
        out_specs=BS((1, Q_BLOCK, att_w), cur), compiler_params=_cparams(2), name="window_attn",
    )(sink, q, k, k, k, v, v, v, kx, vx, gw, jnp.asarray(_band_bias(rows)))


def _filter_body(f_ref, w1h, w1l, b1, w2h, w2l, b2, w3h, w3l, sf_ref, dl_ref, o_ref, *, hy_w):
    def mm_w(a, wh, wl):
        a_hi, a_lo = _split(a)
        return _mm(a_hi, wh[...]) + _mm(a_hi, wl[...]) + _mm(a_lo, wh[...])

    f = f_ref[...]
    h = jnp.sin(sf_ref[0:1, :] * (mm_w(f, w1h, w1l) + b1[...]))
    h = jnp.sin(sf_ref[1:2, :] * (mm_w(h, w2h, w2l) + b2[...]))
    h = mm_w(h, w3h, w3l)
    h = h * jnp.exp(-f[:, 0:1] * dl_ref[...])
    tl = f.shape[0]
    row = lax.broadcasted_iota(I32, h.shape, 0) + pl.program_id(0) * tl
    col = lax.broadcasted_iota(I32, h.shape, 1)
    is_bwd = ((col // hy_w) & 1) == 1
    h = jnp.where((row == 0) & is_bwd, 0.0, h)
    for j in range(o_ref.shape[0]):
        o_ref[j] = h[:, LANES * j:LANES * (j + 1)]


def _filter_mlp(seq, w1, b1, w2, b2, w3, sin_freq, hy_w):
    hid = w1.shape[1]
    fw = 64
    feat = jnp.asarray(_filter_features(seq, fw))
    w1p = jnp.zeros((fw, hid), F32).at[:w1.shape[0]].set(w1)
    n_out = w3.shape[1]
    delta = jnp.asarray(np.tile(_decay_rates(hy_w), n_out // hy_w)).reshape(1, n_out)
    tl = FILTER_TILE
    fixed = lambda i: (0, 0)
    ops = []
    for w in (w1p, w2, w3):
        ops.extend(_split(w))
    return pl.pallas_call(
        functools.partial(_filter_body, hy_w=hy_w),
        out_shape=SDS((n_out // LANES, seq, LANES), F32), grid=(seq // tl,),
        in_specs=[BS((tl, fw), lambda i: (i, 0)), BS((fw, hid), fixed), BS((fw, hid), fixed), BS((1, hid), fixed),
                  BS((hid, hid), fixed), BS((hid, hid), fixed), BS((1, hid), fixed),
                  BS((hid, n_out), fixed), BS((hid, n_out), fixed), BS((2, hid), fixed), BS((1, n_out), fixed)],
        out_specs=BS((n_out // LANES, tl, LANES), lambda i: (0, i, 0)),
        compiler_params=_cparams(1), name="hyena_filter_mlp",
    )(feat, ops[0], ops[1], b1.reshape(1, hid), ops[2], ops[3], b2.reshape(1, hid), ops[4], ops[5], sin_freq, delta)


C_TILES = 2
C_GROUP = C_TILES * LANES


def _load_cat(ref, row0, n_rows):
    return jnp.concatenate([ref[c, pl.ds(row0, n_rows), :] for c in range(C_TILES)], axis=1)


def _store_cat(ref, row0, n_rows, val):
    for c in range(C_TILES):
        ref[c, pl.ds(row0, n_rows), :] = val[:, c * LANES:(c + 1) * LANES]


def _fwd_major(src, p_ref, ak, passes):
    half = FFT_MAJOR // 2

    def body(g, carry):
        r0 = pl.multiple_of(g * SUBLANES, SUBLANES)
        st = jnp.concatenate([_load_cat(src, n2 * FFT_MINOR + r0, SUBLANES) for n2 in range(half)], axis=0)
        out = _mm_split(ak[0][...], ak[1][...] if passes == 3 else None, st, passes)
        for p in range(FFT_MAJOR):
            _store_cat(p_ref, p * FFT_MINOR + r0, SUBLANES, out[p * SUBLANES:(p + 1) * SUBLANES])
        return carry

    lax.fori_loop(0, FFT_MINOR // SUBLANES, body, 0, unroll=4)


def _inv_major(p_ref, dst, dk, passes):
    half = FFT_MAJOR // 2

    def body(g, carry):
        r0 = pl.multiple_of(g * SUBLANES, SUBLANES)
        st = jnp.concatenate([_load_cat(p_ref, p * FFT_MINOR + r0, SUBLANES) for p in range(FFT_MAJOR)], axis=0)
        out = _mm_split(dk[0][...], dk[1][...] if passes == 3 else None, st, passes)
        for n2 in range(half):
            _store_cat(dst, n2 * FFT_MINOR + r0, SUBLANES, out[n2 * SUBLANES:(n2 + 1) * SUBLANES])
        return carry

    lax.fori_loop(0, FFT_MINOR // SUBLANES, body, 0, unroll=4)


def _slab_spectrum(p_ref, mf, s, passes, real_slot=None):
    m = FFT_MINOR
    if real_slot is None:
        st = jnp.concatenate([_load_cat(p_ref, s * m, m), _load_cat(p_ref, (FFT_MAJOR // 2 + s) * m, m)], axis=0)
        x = _mm_split(mf[0][s], mf[1][s] if passes == 3 else None, st, passes)
    else:
        st = _load_cat(p_ref, real_slot * m, m)
        x = _mm_split(mf[0][s, :, 0:m], mf[1][s, :, 0:m] if passes == 3 else None, st, passes)
    return x[:m], x[m:]


def _spectrum_body(f_ref, b_ref, akh, akl, mfh, mfl, o_ref, p_ref):
    half = FFT_MAJOR // 2
    ak, mf = (akh, akl), (mfh, mfl)
    for is_bwd, src in ((False, f_ref), (True, b_ref)):
        _fwd_major(src, p_ref, ak, 3)

        def put(s, re, im, is_bwd=is_bwd):
            if is_bwd:
                o_ref[0, 0, s, 0] = o_ref[0, 0, s, 0] + re
                o_ref[0, 0, s, 1] = o_ref[0, 0, s, 1] - im
            else:
                o_ref[0, 0, s, 0] = re
                o_ref[0, 0, s, 1] = im

        put(0, *_slab_spectrum(p_ref, mf, 0, 3, real_slot=0))
        put(half, *_slab_spectrum(p_ref, mf, half, 3, real_slot=half))

        def body(s, carry):
            put(s, *_slab_spectrum(p_ref, mf, s, 3))
            return carry

        lax.fori_loop(1, half, body, 0, unroll=8)


def _spectrum(hf, seq, hy_w):
    ak, mf = _dft_hi_lo("ak"), _dft_hi_lo("mf")
    n_grp = hy_w // C_GROUP
    fixed2 = lambda o, c: (0, 0)
    fixed3 = lambda o, c: (0, 0, 0)
    one = pl.Buffered(1)
    return pl.pallas_call(
        _spectrum_body,
        out_shape=SDS((HYENA_ORDER, n_grp, N_SLABS, 2, FFT_MINOR, C_GROUP), F32), grid=(HYENA_ORDER, n_grp),
        in_specs=[BS((C_TILES, seq, LANES), lambda o, c: (o * 2 * n_grp + c, 0, 0), pipeline_mode=one),
                  BS((C_TILES, seq, LANES), lambda o, c: (o * 2 * n_grp + n_grp + c, 0, 0), pipeline_mode=one),
                  BS(ak[0].shape, fixed2, pipeline_mode=one), BS(ak[0].shape, fixed2, pipeline_mode=one),
                  BS(mf[0].shape, fixed3, pipeline_mode=one), BS(mf[0].shape, fixed3, pipeline_mode=one)],
        out_specs=BS((1, 1, N_SLABS, 2, FFT_MINOR, C_GROUP), lambda o, c: (o, c, 0, 0, 0, 0)),
        scratch_shapes=[pltpu.VMEM((C_TILES, FFT_MAJOR * FFT_MINOR, LANES), F32)],
        compiler_params=_cparams(2), name="hyena_spectrum",
    )(hf, hf, ak[0], ak[1], mf[0], mf[1])


CONV_CHUNK = 128
CONV_PASSES = 1


def _sconv_chunk(ref, c, r0, n_rows, seq, prm_ref, base):
    cl = slice(c * LANES, (c + 1) * LANES)
    cur = ref[c, pl.ds(r0, n_rows), :]
    row = lax.broadcasted_iota(I32, cur.shape, 0)
    before = ref[c, pl.ds(jnp.maximum(r0 - 1, 0), 1), :] * (r0 > 0).astype(F32)
    after = ref[c, pl.ds(jnp.minimum(r0 + n_rows, seq - 1), 1), :] * (r0 + n_rows < seq).astype(F32)
    prev = jnp.where(row == 0, before, pltpu.roll(cur, 1, axis=0))
    nxt = jnp.where(row == n_rows - 1, after, pltpu.roll(cur, n_rows - 1, axis=0))
    w = lambda k: prm_ref[0, 0, base + k:base + k + 1, cl]
    return w(3) + w(0) * prev + w(1) * cur + w(2) * nxt


def _conv_body(z_ref, x_ref, prm_ref, akh, dkh, mfh, mih, h_ref, o_ref, zs_ref, p_ref, *, conv_z):
    seq = z_ref.shape[2]
    half = FFT_MAJOR // 2
    m = FFT_MINOR
    zv, xv = z_ref.at[0], x_ref.at[0]
    ak, dk, mf, mi = (akh, None), (dkh, None), (mfh, None), (mih, None)
    n_chunks = seq // CONV_CHUNK

    if conv_z:
        def prep(i, carry):
            r0 = pl.multiple_of(i * CONV_CHUNK, CONV_CHUNK)
            for c in range(C_TILES):
                zs_ref[c, pl.ds(r0, CONV_CHUNK), :] = _sconv_chunk(zv, c, r0, CONV_CHUNK, seq, prm_ref, 0)
            return carry

        lax.fori_loop(0, n_chunks, prep, 0)
        src = zs_ref
    else:
        src = zv
    _fwd_major(src, p_ref, ak, CONV_PASSES)

    def slab(s, real_slot=None):
        xr, xi = _slab_spectrum(p_ref, mf, s, CONV_PASSES, real_slot)
        hr, hi = h_ref[0, 0, s, 0], h_ref[0, 0, s, 1]
        y = jnp.concatenate([xr * hr - xi * hi, xr * hi + xi * hr], axis=0)
        if real_slot is None:
            out = _mm_split(mi[0][s], None, y, CONV_PASSES)
            _store_cat(p_ref, s * m, m, out[:m])
            _store_cat(p_ref, (half + s) * m, m, out[m:])
        else:
            out = _mm_split(mi[0][s, 0:m, :], None, y, CONV_PASSES)
            _store_cat(p_ref, real_slot * m, m, out)

    slab(0, real_slot=0)
    slab(half, real_slot=half)

    def slab_loop(s, carry):
        slab(s)
        return carry

    lax.fori_loop(1, half, slab_loop, 0, unroll=8)
    _inv_major(p_ref, zs_ref, dk, CONV_PASSES)

    def fin(i, carry):
        r0 = pl.multiple_of(i * CONV_CHUNK, CONV_CHUNK)
        for c in range(C_TILES):
            cl = slice(c * LANES, (c + 1) * LANES)
            gate = _sconv_chunk(xv, c, r0, CONV_CHUNK, seq, prm_ref, 4)
            if conv_z:
                zval = _sconv_chunk(zv, c, r0, CONV_CHUNK, seq, prm_ref, 0)
            else:
                zval = zv[c, pl.ds(r0, CONV_CHUNK), :]
            skip = prm_ref[0, 0, 8:9, cl]
            o_ref[0, c, pl.ds(r0, CONV_CHUNK), :] = gate * (zs_ref[c, pl.ds(r0, CONV_CHUNK), :] + skip * zval)
        return carry

    lax.fori_loop(0, n_chunks, fin, 0, unroll=4)


def _hyena_order(z, z_grp0, x, x_grp0, prm, order, spec, conv_z, n_grp):
    bsz, _, seq, _ = z.shape
    ak, dk, mf, mi = (_dft_hi_lo(n)[0] for n in ("ak", "dk", "mf", "mi"))
    one = pl.Buffered(1)
    fixed2 = lambda c, b: (0, 0)
    fixed3 = lambda c, b: (0, 0, 0)
    blk = (1, C_TILES, seq, LANES)
    return pl.pallas_call(
        functools.partial(_conv_body, conv_z=conv_z),
        out_shape=SDS((bsz, n_grp * C_TILES, seq, LANES), F32), grid=(n_grp, bsz),
        in_specs=[BS(blk, lambda c, b: (b, z_grp0 + c, 0, 0)),
                  BS(blk, lambda c, b: (b, x_grp0 + c, 0, 0)),
                  BS((1, 1, 16, C_GROUP), lambda c, b: (order, c, 0, 0)),
                  BS(ak.shape, fixed2, pipeline_mode=one), BS(dk.shape, fixed2, pipeline_mode=one),
                  BS(mf.shape, fixed3, pipeline_mode=one), BS(mi.shape, fixed3, pipeline_mode=one),
                  BS((1, 1, N_SLABS, 2, FFT_MINOR, C_GROUP), lambda c, b: (order, c, 0, 0, 0, 0), pipeline_mode=one)],
        out_specs=BS(blk, lambda c, b: (b, c, 0, 0)),
        scratch_shapes=[pltpu.VMEM((C_TILES, seq, LANES), F32),
                        pltpu.VMEM((C_TILES, FFT_MAJOR * FFT_MINOR, LANES), F32)],
        compiler_params=_cparams(2), name=f"hyena_conv{order}",
    )(z, x, prm, ak, dk, mf, mi, spec)


ROUTER_ROWS = 48
EXPERT_ROW0 = 8


def _outproj_body(att_ref, hy_ref, gwh_ref, wo_ref, x_ref, g1_ref, sh2_ref, sc2_ref, n2w_ref, wrh_ref, wrl_ref,
                  br_ref, tri_ref, x1_ref, h2_ref, ids_ref, gates_ref, cnt_ref, carry_ref):
    @pl.when((pl.program_id(0) == 0) & (pl.program_id(1) == 0))
    def _():
        carry_ref[...] = jnp.zeros_like(carry_ref)

    hy = jnp.concatenate([hy_ref[0, j] for j in range(hy_ref.shape[1])], axis=1)
    hyn = _rms(hy) * gwh_ref[...]
    mix = _mm(jnp.concatenate([att_ref[0], hyn.astype(BF16)], axis=1), wo_ref[...])
    x1 = x_ref[0] + g1_ref[0] * mix
    x1_ref[0] = x1
    h2 = _rms(x1) * n2w_ref[...]
    h2 = h2 * (1.0 + sc2_ref[0]) + sh2_ref[0]
    h2_ref[...] = h2

    h_hi, h_lo = _split(h2)
    lg = _mm_nt(wrh_ref[...], h_hi) + _mm_nt(wrh_ref[...], h_lo) + _mm_nt(wrl_ref[...], h_hi) + br_ref[...]
    tm = lg.shape[1]
    gl = lg[0:N_GROUPS]
    el = lg[EXPERT_ROW0:EXPERT_ROW0 + N_EXPERTS]
    gmax = jnp.max(gl, axis=0, keepdims=True)
    rg = lax.broadcasted_iota(I32, gl.shape, 0).astype(F32)
    g_idx = jnp.min(jnp.where(gl == gmax, rg, float(N_GROUPS)), axis=0, keepdims=True)
    g_val = 1.0 / jnp.sum(jnp.exp(gl - gmax), axis=0, keepdims=True)
    re_i = lax.broadcasted_iota(I32, el.shape, 0)
    re = re_i.astype(F32)
    in_group = (re_i // EXPERTS_PER_GROUP).astype(F32) == g_idx
    elm = jnp.where(in_group, el, NEG_INF)
    m1 = jnp.max(elm, axis=0, keepdims=True)
    i1 = jnp.min(jnp.where(elm == m1, re, float(N_EXPERTS)), axis=0, keepdims=True)
    elm2 = jnp.where(re == i1, NEG_INF, elm)
    m2 = jnp.max(elm2, axis=0, keepdims=True)
    i2 = jnp.min(jnp.where(elm2 == m2, re, float(N_EXPERTS)), axis=0, keepdims=True)
    e2 = jnp.exp(m2 - m1)
    inv = g_val / (1.0 + e2)

    oh1 = (re == i1).astype(F32)
    oh2 = (re == i2).astype(F32)
    oh = oh1 + oh2
    base = _mm(oh.astype(BF16), tri_ref[...]) + carry_ref[:, 0:1]
    rank1 = jnp.sum(oh1 * base, axis=0, keepdims=True)
    rank2 = jnp.sum(oh2 * base, axis=0, keepdims=True)
    carry_ref[...] = carry_ref[...] + jnp.sum(oh, axis=1, keepdims=True)
    cnt_ref[...] = carry_ref[...]

    zi = jnp.zeros((1, tm), I32)
    for k, val in enumerate((i1.astype(I32), i2.astype(I32), rank1.astype(I32), rank2.astype(I32), zi, zi, zi, zi)):
        ids_ref[k:k + 1, :] = val
    zf = jnp.zeros((1, tm), F32)
    for k, val in enumerate((inv, inv * e2, zf, zf, zf, zf, zf, zf)):
        gates_ref[k:k + 1, :] = val


def _outproj_route(att_n, hy, gw_hy, wo_bf, x, g1, sh2, sc2, norm2_w, wr_hi, wr_lo, br):
    bsz, seq, d = x.shape
    tm = ROW_TILE
    n_tok = bsz * seq
    att_w = att_n.shape[2]
    hy_tiles = hy.shape[1]
    tri = jnp.asarray(np.triu(np.ones((tm, tm), np.float32), k=1).astype(ml_dtypes.bfloat16))
    row = lambda b, i: (b, i, 0)
    per_b = lambda b, i: (b, 0, 0)
    fixed2 = lambda b, i: (0, 0)
    tok = lambda b, i: (0, b * (seq // tm) + i)
    return pl.pallas_call(
        _outproj_body,
        out_shape=(SDS((bsz, seq, d), F32), SDS((n_tok, d), F32), SDS((SUBLANES, n_tok), I32),
                   SDS((SUBLANES, n_tok), F32), SDS((N_EXPERTS, LANES), F32)),
        grid=(bsz, seq // tm),
        in_specs=[BS((1, tm, att_w), row), BS((1, hy_tiles, tm, LANES), lambda b, i: (b, 0, i, 0)),
                  BS((1, hy_tiles * LANES), fixed2), BS(wo_bf.shape, fixed2), BS((1, tm, d), row),
                  BS((1, 1, d), per_b), BS((1, 1, d), per_b), BS((1, 1, d), per_b), BS((1, d), fixed2),
                  BS((ROUTER_ROWS, d), fixed2), BS((ROUTER_ROWS, d), fixed2), BS((ROUTER_ROWS, 1), fixed2),
                  BS((tm, tm), fixed2)],
        out_specs=(BS((1, tm, d), row), BS((tm, d), lambda b, i: (b * (seq // tm) + i, 0)),
                   BS((SUBLANES, tm), tok), BS((SUBLANES, tm), tok), BS((N_EXPERTS, LANES), fixed2)),
        scratch_shapes=[pltpu.VMEM((N_EXPERTS, LANES), F32)],
        compiler_params=_cparams(2), name="out_proj_router",
    )(att_n, hy, gw_hy, wo_bf, x, g1, sh2, sc2, norm2_w.reshape(1, d), wr_hi, wr_lo, br, tri)


ISSUE_UNROLL = 8


def _row_copy(src, src_row, dst, dst_row, sem):
    return pltpu.make_async_copy(src.at[pl.ds(src_row, 1), :], dst.at[pl.ds(dst_row, 1), :], sem)


def _dispatch_body(pend_ref, d_ref, h_ref, buf_ref, zero_ref, sem):
    n = h_ref.shape[0]

    @pl.when(pl.program_id(0) == 0)
    def _():
        zero_ref[...] = jnp.zeros_like(zero_ref)

        def tail_copy(e):
            row0 = pl.multiple_of(pend_ref[e] - MOE_BLOCK, MOE_BLOCK)
            return pltpu.make_async_copy(zero_ref, buf_ref.at[pl.ds(row0, MOE_BLOCK), :], sem)

        def non_empty(e):
            return pend_ref[e] > (pend_ref[e - 1] if e > 0 else 0)

        def spare_copy(i):
            row0 = pl.multiple_of(i * MOE_BLOCK, MOE_BLOCK)
            return pltpu.make_async_copy(zero_ref, buf_ref.at[pl.ds(row0, MOE_BLOCK), :], sem)

        first_spare = pend_ref[N_EXPERTS - 1] // MOE_BLOCK
        n_blocks = buf_ref.shape[0] // MOE_BLOCK
        for e in range(N_EXPERTS):
            @pl.when(non_empty(e))
            def _(e=e):
                tail_copy(e).start()
        lax.fori_loop(first_spare, n_blocks, lambda i, c: (spare_copy(i).start(), c)[1], 0)
        for e in range(N_EXPERTS):
            @pl.when(non_empty(e))
            def _(e=e):
                tail_copy(e).wait()
        lax.fori_loop(first_spare, n_blocks, lambda i, c: (spare_copy(i).wait(), c)[1], 0)

    def issue(t, carry):
        for j in range(TOP_K):
            _row_copy(h_ref, t, buf_ref, d_ref[j, t], sem).start()
        return carry

    lax.fori_loop(0, n, issue, 0, unroll=ISSUE_UNROLL)
    for j in range(TOP_K):
        pltpu.make_async_copy(h_ref, buf_ref.at[pl.ds(0, n), :], sem).wait()


def _dispatch(pend, dest, h2, n_rows):
    n_tok, d = h2.shape
    tm = DISPATCH_TILE
    grid_spec = pltpu.PrefetchScalarGridSpec(
        num_scalar_prefetch=1, grid=(n_tok // tm,),
        in_specs=[BS((TOP_K, tm), lambda i, pend: (0, i), memory_space=pltpu.SMEM),
                  BS((tm, d), lambda i, pend: (i, 0))],
        out_specs=BS(memory_space=pl.ANY),
        scratch_shapes=[pltpu.VMEM((MOE_BLOCK, d), F32), pltpu.SemaphoreType.DMA(())])
    return pl.pallas_call(_dispatch_body, out_shape=SDS((n_rows, d), F32), grid_spec=grid_spec,
                          compiler_params=_cparams(1), name="moe_dispatch")(pend, dest, h2)


def _expert_body(be_ref, nu_ref, x_ref, w1_ref, w3_ref, w2_ref, y_ref, w13_s, w2_s):
    i = pl.program_id(0)
    hidden = w2_ref.shape[1]
    active = i < nu_ref[0]
    new_expert = jnp.logical_or(i == 0, be_ref[i] != be_ref[jnp.maximum(i - 1, 0)])

    @pl.when(jnp.logical_and(active, new_expert))
    def _():
        w13_s[:, :hidden] = w1_ref[0].astype(BF16)
        w13_s[:, hidden:] = w3_ref[0].astype(BF16)
        w2_s[...] = w2_ref[0].astype(BF16)

    @pl.when(active)
    def _():
        h = _mm(x_ref[...].astype(BF16), w13_s[...])
        act = _silu(h[:, :hidden]) * h[:, hidden:]
        y_ref[...] = _mm(act.astype(BF16), w2_s[...])

    @pl.when(jnp.logical_not(active))
    def _():
        y_ref[...] = jnp.zeros_like(y_ref)


def _experts(block_expert, n_used, buf, w1, w3, w2):
    n_rows, d = buf.shape
    hidden = w2.shape[1]
    nb = n_rows // MOE_BLOCK
    by_expert = lambda i, be, nu: (be[i], 0, 0)
    grid_spec = pltpu.PrefetchScalarGridSpec(
        num_scalar_prefetch=2, grid=(nb,),
        in_specs=[BS((MOE_BLOCK, d), lambda i, be, nu: (jnp.minimum(i, nu[0] - 1), 0)),
                  BS((1, d, hidden), by_expert), BS((1, d, hidden), by_expert), BS((1, hidden, d), by_expert)],
        out_specs=BS((MOE_BLOCK, d), lambda i, be, nu: (i, 0)),
        scratch_shapes=[pltpu.VMEM((d, 2 * hidden), BF16), pltpu.VMEM((hidden, d), BF16)])
    return pl.pallas_call(_expert_body, out_shape=SDS((n_rows, d), F32), grid_spec=grid_spec,
                          compiler_params=_cparams(1), name="moe_experts")(block_expert, n_used, buf, w1, w3, w2)


def _combine_body(d_ref, dn_ref, y_ref, x1_ref, g2_ref, gt_ref, o_ref, rows_ref, sem):
    i = pl.program_id(0)
    n = x1_ref.shape[0]
    slot = i % 2

    def issue_tile(dest_ref, s):
        def issue(t, carry):
            for j in range(TOP_K):
                _row_copy(y_ref, dest_ref[j, t], rows_ref.at[s, j], t, sem.at[s]).start()
            return carry

        lax.fori_loop(0, n, issue, 0, unroll=ISSUE_UNROLL)

    @pl.when(i == 0)
    def _():
        issue_tile(d_ref, 0)

    @pl.when(i < pl.num_programs(0) - 1)
    def _():
        issue_tile(dn_ref, 1 - slot)

    for j in range(TOP_K):
        pltpu.make_async_copy(y_ref.at[pl.ds(0, n), :], rows_ref.at[slot, j], sem.at[slot]).wait()
    gt = gt_ref[...]
    moe = gt[:, 0:1] * rows_ref[slot, 0] + gt[:, 1:2] * rows_ref[slot, 1]
    o_ref[...] = x1_ref[...] + g2_ref[0] * moe


def _combine(dest, y, x1, g2, gates_t, seq):
    n_tok, d = x1.shape
    tm = COMBINE_TILE
    steps = n_tok // tm
    return pl.pallas_call(
        _combine_body, out_shape=SDS((n_tok, d), F32), grid=(steps,),
        in_specs=[BS((TOP_K, tm), lambda i: (0, i), memory_space=pltpu.SMEM),
                  BS((TOP_K, tm), lambda i: (0, jnp.minimum(i + 1, steps - 1)), memory_space=pltpu.SMEM),
                  BS(memory_space=pl.ANY),
                  BS((tm, d), lambda i: (i, 0)), BS((1, 1, d), lambda i: (i // (seq // tm), 0, 0)),
                  BS((tm, TOP_K), lambda i: (i, 0))],
        out_specs=BS((tm, d), lambda i: (i, 0)),
        scratch_shapes=[pltpu.VMEM((2, TOP_K, tm, d), F32), pltpu.SemaphoreType.DMA((2,))],
        compiler_params=_cparams(1), name="moe_combine",
    )(dest, dest, y, x1, g2, gates_t)


def kernel(x, c, ctx, c_ctx, ada_w, ada_b, norm1_w, w_in, q_norm_w, k_norm_w, attn_sink, hy_conv_w, hy_conv_b, hy_w1,
           hy_b1, hy_w2, hy_b2, hy_w3, hy_sin_freq, hy_skip, group_norm_w, w_out, norm2_w, router_g_w, router_g_b,
           router_e_w, router_e_b, exp_w1, exp_w3, exp_w2):
    bsz, seq, d = x.shape
    assert ada_w.shape[0] == 1, "single-layer block"
    att_w = d // 2
    hy_w = d - att_w
    kv_w = N_KV_HEADS * HEAD_DIM
    n_heads = att_w // HEAD_DIM
    n_grp = hy_w // C_GROUP
    assert 2 * seq == FFT_MAJOR * FFT_MINOR and seq % ROW_TILE == 0 and hy_w % C_GROUP == 0
    assert w_in.shape[2] == att_w + 2 * kv_w + (HYENA_ORDER + 1) * hy_w

    pad = (-(bsz + 1)) % SUBLANES
    cc = jnp.concatenate([c, c_ctx[None, :], jnp.zeros((pad, d), F32)], axis=0)
    mod = _ada(cc, ada_w[0], ada_b[0])
    lat = [mod[:bsz, k * d:(k + 1) * d].reshape(bsz, 1, d) for k in range(6)]
    sh1, sc1, g1, sh2, sc2, g2 = lat
    csh1 = mod[bsz:bsz + 1, 0:d]
    csc1 = mod[bsz:bsz + 1, d:2 * d]

    w_in_bf = w_in[0].astype(BF16)
    qnw = jnp.tile(q_norm_w[0], n_heads).reshape(1, att_w)
    knw = jnp.tile(k_norm_w[0], N_KV_HEADS).reshape(1, kv_w)
    q, k, v, u = _inproj(x, sh1, sc1, norm1_w[0], w_in_bf, qnw, knw, att_w, kv_w)
    kx, vx = _ctxproj(ctx, csh1, csc1, norm1_w[0], w_in_bf[:, att_w:att_w + 2 * kv_w], knw, kv_w)

    gw = group_norm_w[0]
    att_n = _attention(attn_sink[0], q, k, v, kx, vx, gw[:att_w].reshape(1, att_w))

    hf = _filter_mlp(seq, hy_w1[0], hy_b1[0], hy_w2[0], hy_b2[0], hy_w3[0], hy_sin_freq[0], hy_w)
    spec = _spectrum(hf, seq, hy_w)
    cw = hy_conv_w[0].reshape(3, HYENA_ORDER + 1, n_grp, C_GROUP)
    cb = hy_conv_b[0].reshape(1, HYENA_ORDER + 1, n_grp, C_GROUP)
    taps = jnp.concatenate([cw, cb], axis=0)
    prm = jnp.zeros((HYENA_ORDER, n_grp, 16, C_GROUP), F32)
    for o in range(HYENA_ORDER):
        prm = prm.at[o, :, 4:8].set(jnp.transpose(taps[:, o + 1], (1, 0, 2)))
        prm = prm.at[o, :, 8].set(hy_skip[0, o].reshape(n_grp, C_GROUP))
    prm = prm.at[0, :, 0:4].set(jnp.transpose(taps[:, 0], (1, 0, 2)))
    z = _hyena_order(u, 0, u, n_grp, prm, 0, spec, True, n_grp)
    hy = _hyena_order(z, 0, u, 2 * n_grp, prm, 1, spec, False, n_grp)

    wr = jnp.zeros((ROUTER_ROWS, d), F32)
    wr = wr.at[0:N_GROUPS].set(router_g_w[0].T).at[EXPERT_ROW0:EXPERT_ROW0 + N_EXPERTS].set(router_e_w[0].T)
    br = jnp.zeros((ROUTER_ROWS, 1), F32)
    br = br.at[0:N_GROUPS, 0].set(router_g_b[0]).at[EXPERT_ROW0:EXPERT_ROW0 + N_EXPERTS, 0].set(router_e_b[0])
    wr_hi, wr_lo = _split(wr)
    x1, h2, ids, gates, counts = _outproj_route(att_n, hy, gw[att_w:].reshape(1, hy_w), w_out[0].astype(BF16), x,
                                                g1, sh2, sc2, norm2_w[0], wr_hi, wr_lo, br)

    n_tok = bsz * seq
    cnt = counts[:, 0].astype(I32)
    padded = (cnt + MOE_BLOCK - 1) // MOE_BLOCK * MOE_BLOCK
    pend = jnp.cumsum(padded)
    pstart = pend - padded
    e_iota = jnp.arange(N_EXPERTS, dtype=I32)
    is_e = ids[0:TOP_K, :, None] == e_iota
    dest = jnp.sum(jnp.where(is_e, pstart, 0), axis=-1) + ids[TOP_K:2 * TOP_K]
    n_blocks = (n_tok * TOP_K) // MOE_BLOCK + N_EXPERTS
    block_row0 = jnp.arange(n_blocks, dtype=I32) * MOE_BLOCK
    block_expert = jnp.minimum(jnp.sum((block_row0[:, None] >= pend[None, :]).astype(I32), axis=-1), N_EXPERTS - 1)
    n_used = (pend[-1:] // MOE_BLOCK).astype(I32)

    buf = _dispatch(pend.astype(I32), dest, h2, n_blocks * MOE_BLOCK)
    y = _experts(block_expert, n_used, buf, exp_w1[0], exp_w3[0], exp_w2[0])
    out = _combine(dest, y, x1.reshape(n_tok, d), g2, gates[0:TOP_K].T, seq)
    return out.reshape(bsz, seq, d)
```

```python
import functools
import math

import ml_dtypes
import numpy as np

import jax
import jax.numpy as jnp
from jax import lax
from jax.experimental import pallas as pl
from jax.experimental.pallas import tpu as pltpu

F32, BF16, I32 = jnp.float32, jnp.bfloat16, jnp.int32
SDS = jax.ShapeDtypeStruct
BS = pl.BlockSpec

HEAD_DIM = 64
N_KV_HEADS = 2
WINDOW = 128
Q_BLOCK = 128
GRID_W = 64
ROPE_THETA = 10000.0
EPS = 1e-6
NEG_INF = -1e30
LOG2E = math.log2(math.e)
HYENA_BANDS = 16
HYENA_ORDER = 2
DECAY_TARGET = 1e-2
FAST_DECAY_PCT = 0.3
SLOW_DECAY_PCT = 1.5
N_GROUPS = 4
EXPERTS_PER_GROUP = 8
N_EXPERTS = N_GROUPS * EXPERTS_PER_GROUP
TOP_K = 2

LANES = 128
SUBLANES = 8
VMEM_LIMIT = 60 * 1024 * 1024

ROW_TILE = 1024
MOE_BLOCK = 512
DISPATCH_TILE = 4096
COMBINE_TILE = 1024
FILTER_TILE = 512

FFT_MINOR = 128
FFT_MAJOR = 64
N_SLABS = FFT_MAJOR // 2 + 1


def _cparams(n_grid):
    return pltpu.CompilerParams(dimension_semantics=("arbitrary",) * n_grid, vmem_limit_bytes=VMEM_LIMIT)


def _mm(a, b):
    return jnp.dot(a, b, preferred_element_type=F32)


def _mm_nt(a, b):
    return lax.dot_general(a, b, (((1,), (1,)), ((), ())), preferred_element_type=F32)


def _split(x):
    hi = x.astype(BF16)
    lo = (x - hi.astype(F32)).astype(BF16)
    return hi, lo


def _mm_split(w_hi, w_lo, x, passes):
    if passes == 1:
        return _mm(w_hi, x.astype(BF16))
    x_hi, x_lo = _split(x)
    out = _mm(w_hi, x_hi) + _mm(w_hi, x_lo)
    if passes == 3:
        out = out + _mm(w_lo, x_hi)
    return out


def _mm3(a, b):
    a_hi, a_lo = _split(a)
    b_hi, b_lo = _split(b)
    return _mm(a_hi, b_hi) + _mm(a_hi, b_lo) + _mm(a_lo, b_hi)


def _silu(x):
    return x * (1.0 / (1.0 + jnp.exp(-x)))


def _rms(x):
    return x * lax.rsqrt(jnp.mean(x * x, axis=-1, keepdims=True) + EPS)


@functools.lru_cache(maxsize=None)
def _rope_tables(seq_len):
    pos = np.arange(seq_len)
    rows = (pos // GRID_W).astype(np.float64)
    cols = (pos % GRID_W).astype(np.float64)
    pairs = HEAD_DIM // 4
    inv_freq = ROPE_THETA ** (-np.arange(pairs, dtype=np.float64) / pairs)
    ang_r = rows[:, None] * inv_freq[None, :]
    ang_c = cols[:, None] * inv_freq[None, :]
    ang = np.concatenate([ang_r, ang_r, ang_c, ang_c], axis=1)
    sign = np.tile(np.concatenate([-np.ones(pairs), np.ones(pairs)]), 2)
    cos = np.cos(ang)
    sin = np.sin(ang) * sign[None, :]
    reps = LANES // HEAD_DIM
    return np.tile(cos, (1, reps)).astype(np.float32), np.tile(sin, (1, reps)).astype(np.float32)


@functools.lru_cache(maxsize=None)
def _block_diag_mean(width):
    h = np.arange(width) // HEAD_DIM
    return ((h[:, None] == h[None, :]).astype(np.float32) / HEAD_DIM).astype(ml_dtypes.bfloat16)


@functools.lru_cache(maxsize=None)
def _filter_features(seq_len, width):
    pos = np.arange(seq_len, dtype=np.float64)
    t = pos / seq_len
    bands = np.linspace(1e-4, HYENA_BANDS - 1, HYENA_BANDS)
    ang = (2.0 * math.pi / seq_len) * pos[:, None] * bands[None, :]
    feat = np.concatenate([t[:, None], np.cos(ang), -np.sin(ang)], axis=-1)
    out = np.zeros((seq_len, width), np.float32)
    out[:, :feat.shape[1]] = feat
    return out


@functools.lru_cache(maxsize=None)
def _decay_rates(width):
    max_decay = math.log(DECAY_TARGET) / FAST_DECAY_PCT
    min_decay = math.log(DECAY_TARGET) / SLOW_DECAY_PCT
    return np.abs(np.linspace(min_decay, max_decay, width)).astype(np.float32)


@functools.lru_cache(maxsize=None)
def _dft_tables():
    n = FFT_MAJOR * FFT_MINOR
    half = FFT_MAJOR // 2
    n2 = np.arange(half)
    a = np.zeros((FFT_MAJOR, half))
    a[:half] = np.cos(2 * np.pi * np.arange(half)[:, None] * n2[None, :] / FFT_MAJOR)
    a[half] = (-1.0) ** n2
    k2 = np.arange(1, half)
    a[half + 1:] = -np.sin(2 * np.pi * k2[:, None] * n2[None, :] / FFT_MAJOR)
    d = np.zeros((half, FFT_MAJOR))
    d[:, 0] = 1.0 / n
    d[:, 1:half] = 2 * np.cos(2 * np.pi * n2[:, None] * k2[None, :] / FFT_MAJOR) / n
    d[:, half] = (-1.0) ** n2 / n
    d[:, half + 1:] = -2 * np.sin(2 * np.pi * n2[:, None] * k2[None, :] / FFT_MAJOR) / n
    eye = np.eye(SUBLANES)
    ak = np.kron(a, eye)
    dk = np.kron(d, eye)
    s = np.arange(N_SLABS)[:, None, None]
    k1 = np.arange(FFT_MINOR)[None, :, None]
    n1 = np.arange(FFT_MINOR)[None, None, :]
    g = np.exp(-2j * np.pi * n1 * (FFT_MAJOR * k1 + s) / n)
    mf = np.block([[g.real, -g.imag], [g.imag, g.real]])
    gi = np.conj(g).transpose(0, 2, 1)
    mi = np.block([[gi.real, -gi.imag], [gi.imag, gi.real]])
    return {name: m.astype(np.float32) for name, m in (("ak", ak), ("dk", dk), ("mf", mf), ("mi", mi))}


def _dft_hi_lo(name):
    return _split(jnp.asarray(_dft_tables()[name]))


def _ada_body(c_ref, w_ref, b_ref, o_ref):
    o_ref[...] = _mm3(_silu(c_ref[...]), w_ref[...]) + b_ref[...]


def _ada(cc, ada_w, ada_b):
    rows, d = cc.shape
    n = ada_w.shape[1]
    tn = 1536
    return pl.pallas_call(
        _ada_body, out_shape=SDS((rows, n), F32), grid=(n // tn,),
        in_specs=[BS((rows, d), lambda j: (0, 0)), BS((d, tn), lambda j: (0, j)), BS((1, tn), lambda j: (0, j))],
        out_specs=BS((rows, tn), lambda j: (0, j)), compiler_params=_cparams(1), name="ada_mod",
    )(cc, ada_w, ada_b.reshape(1, n))


def _head_rms(t, bd):
    hi, lo = _split(t * t)
    return t * lax.rsqrt(_mm(hi, bd) + _mm(lo, bd) + EPS)


def _rope(t, cos, sin):
    n = t.shape[1]
    quarter = HEAD_DIM // 4
    lane = lax.broadcasted_iota(I32, t.shape, 1)
    up = pltpu.roll(t, n - quarter, axis=1)
    dn = pltpu.roll(t, quarter, axis=1)
    partner = jnp.where((lane & (2 * quarter - 1)) < quarter, up, dn)
    return t * cos + partner * sin


def _inproj_body(x_ref, sh_ref, sc_ref, nw_ref, w_ref, qnw_ref, knw_ref, cos_ref, sin_ref, bdq_ref, bdk_ref,
                 q_ref, k_ref, v_ref, u_ref, *, att_w, kv_w):
    h = _rms(x_ref[0]) * nw_ref[...]
    h = h * (1.0 + sc_ref[0]) + sh_ref[0]
    proj = _mm(h.astype(BF16), w_ref[...])
    cos, sin = cos_ref[...], sin_ref[...]
    reps = att_w // LANES
    cos_q = jnp.concatenate([cos] * reps, axis=1)
    sin_q = jnp.concatenate([sin] * reps, axis=1)
    q = _head_rms(proj[:, :att_w], bdq_ref[...]) * qnw_ref[...]
    q_ref[0] = (_rope(q, cos_q, sin_q) * (HEAD_DIM ** -0.5 * LOG2E)).astype(BF16)
    k = _head_rms(proj[:, att_w:att_w + kv_w], bdk_ref[...]) * knw_ref[...]
    k_ref[0] = _rope(k, cos, sin).astype(BF16)
    v_ref[0] = proj[:, att_w + kv_w:att_w + 2 * kv_w].astype(BF16)
    hy_off = att_w + 2 * kv_w
    for j in range(u_ref.shape[1]):
        u_ref[0, j] = proj[:, hy_off + LANES * j:hy_off + LANES * (j + 1)]


def _inproj(x, sh, sc, norm_w, w_bf, qnw, knw, att_w, kv_w):
    bsz, seq, d = x.shape
    tm = ROW_TILE
    hy_tiles = (w_bf.shape[1] - att_w - 2 * kv_w) // LANES
    cos, sin = _rope_tables(seq)
    row = lambda b, i: (b, i, 0)
    fixed2 = lambda b, i: (0, 0)
    per_b = lambda b, i: (b, 0, 0)
    return pl.pallas_call(
        functools.partial(_inproj_body, att_w=att_w, kv_w=kv_w),
        out_shape=(SDS((bsz, seq, att_w), BF16), SDS((bsz, seq, kv_w), BF16), SDS((bsz, seq, kv_w), BF16),
                   SDS((bsz, hy_tiles, seq, LANES), F32)),
        grid=(bsz, seq // tm),
        in_specs=[BS((1, tm, d), row), BS((1, 1, d), per_b), BS((1, 1, d), per_b), BS((1, d), fixed2),
                  BS(w_bf.shape, fixed2), BS((1, att_w), fixed2), BS((1, kv_w), fixed2),
                  BS((tm, LANES), lambda b, i: (i, 0)), BS((tm, LANES), lambda b, i: (i, 0)),
                  BS((att_w, att_w), fixed2), BS((kv_w, kv_w), fixed2)],
        out_specs=(BS((1, tm, att_w), row), BS((1, tm, kv_w), row), BS((1, tm, kv_w), row),
                   BS((1, hy_tiles, tm, LANES), lambda b, i: (b, 0, i, 0))),
        compiler_params=_cparams(2), name="in_proj",
    )(x, sh, sc, norm_w.reshape(1, d), w_bf, qnw, knw, jnp.asarray(cos), jnp.asarray(sin),
      jnp.asarray(_block_diag_mean(att_w)), jnp.asarray(_block_diag_mean(kv_w)))


def _ctxproj_body(x_ref, sh_ref, sc_ref, nw_ref, w_ref, knw_ref, bdk_ref, k_ref, v_ref, *, kv_w):
    h = _rms(x_ref[0]) * nw_ref[...]
    h = h * (1.0 + sc_ref[...]) + sh_ref[...]
    proj = _mm(h.astype(BF16), w_ref[...])
    k_ref[0] = (_head_rms(proj[:, :kv_w], bdk_ref[...]) * knw_ref[...]).astype(BF16)
    v_ref[0] = proj[:, kv_w:].astype(BF16)


def _ctxproj(ctx, sh, sc, norm_w, w_kv_bf, knw, kv_w):
    bsz, n_ctx, d = ctx.shape
    fixed2 = lambda b: (0, 0)
    row = lambda b: (b, 0, 0)
    return pl.pallas_call(
        functools.partial(_ctxproj_body, kv_w=kv_w),
        out_shape=(SDS((bsz, n_ctx, kv_w), BF16), SDS((bsz, n_ctx, kv_w), BF16)), grid=(bsz,),
        in_specs=[BS((1, n_ctx, d), row), BS((1, d), fixed2), BS((1, d), fixed2), BS((1, d), fixed2),
                  BS(w_kv_bf.shape, fixed2), BS((1, kv_w), fixed2), BS((kv_w, kv_w), fixed2)],
        out_specs=(BS((1, n_ctx, kv_w), row), BS((1, n_ctx, kv_w), row)),
        compiler_params=_cparams(1), name="ctx_proj",
    )(ctx, sh, sc, norm_w.reshape(1, d), w_kv_bf, knw, jnp.asarray(_block_diag_mean(kv_w)))


def _attn_body(sink_ref, q_ref, kp_ref, kc_ref, kn_ref, vp_ref, vc_ref, vn_ref, kx_ref, vx_ref, gw_ref, bias_ref,
               o_ref):
    n_ctx = kx_ref.shape[1]
    n_heads = q_ref.shape[2] // HEAD_DIM
    group = n_heads // N_KV_HEADS
    rows = group * Q_BLOCK
    bias = bias_ref[0]
    head_of_row = lax.broadcasted_iota(I32, (rows, 1), 0) // Q_BLOCK
    outs = []
    for g in range(N_KV_HEADS):
        sl = slice(g * HEAD_DIM, (g + 1) * HEAD_DIM)
        kb = jnp.concatenate([kx_ref[0][:, sl], kp_ref[0][:, sl], kc_ref[0][:, sl], kn_ref[0][:, sl]], axis=0)
        vb = jnp.concatenate([vx_ref[0][:, sl], vp_ref[0][:, sl], vc_ref[0][:, sl], vn_ref[0][:, sl]], axis=0)
        heads = range(g * group, (g + 1) * group)
        q4 = jnp.concatenate([q_ref[0][:, h * HEAD_DIM:(h + 1) * HEAD_DIM] for h in heads], axis=0)
        sink = jnp.zeros((rows, 1), F32)
        for hh, h in enumerate(heads):
            sink = jnp.where(head_of_row == hh, sink_ref[h] * LOG2E, sink)
        s = _mm_nt(q4, kb)
        s_ctx = s[:, :n_ctx]
        s_band = s[:, n_ctx:] + bias
        m = jnp.maximum(jnp.maximum(jnp.max(s_ctx, axis=-1, keepdims=True), jnp.max(s_band, axis=-1, keepdims=True)),
                        sink)
        p_ctx = jnp.exp2((s_ctx - m).astype(BF16))
        p_band = jnp.exp2((s_band - m).astype(BF16))
        denom = (jnp.sum(p_ctx.astype(F32), axis=-1, keepdims=True) + jnp.sum(p_band.astype(F32), axis=-1, keepdims=True)
                 + jnp.exp2(sink - m))
        o4 = (_mm(p_ctx, vb[:n_ctx]) + _mm(p_band, vb[n_ctx:])) / denom
        for hh in range(group):
            outs.append(o4[hh * Q_BLOCK:(hh + 1) * Q_BLOCK])
    att = jnp.concatenate(outs, axis=1)
    o_ref[0] = (_rms(att) * gw_ref[...]).astype(BF16)


@functools.lru_cache(maxsize=None)
def _band_bias(rows):
    r = (np.arange(rows) % Q_BLOCK)[:, None]
    j = np.arange(3 * Q_BLOCK)[None, :]
    in_band = (j >= r) & (j <= r + 2 * WINDOW)
    cases = (in_band, in_band & (j >= Q_BLOCK), in_band & (j < 2 * Q_BLOCK))
    return np.stack([np.where(c, 0.0, NEG_INF) for c in cases]).astype(np.float32)


def _attention(sink, q, k, v, kx, vx, gw):
    bsz, seq, att_w = q.shape
    kv_w = k.shape[2]
    n_ctx = kx.shape[1]
    nb = seq // Q_BLOCK
    cur = lambda b, i: (b, i, 0)
    prev = lambda b, i: (b, jnp.maximum(i - 1, 0), 0)
    nxt = lambda b, i: (b, jnp.minimum(i + 1, nb - 1), 0)
    per_b = lambda b, i: (b, 0, 0)
    kvb = (1, Q_BLOCK, kv_w)
    rows = (att_w // HEAD_DIM // N_KV_HEADS) * Q_BLOCK
    assert nb >= 2
    edge_case = lambda b, i: (jnp.where(i == 0, 1, jnp.where(i == nb - 1, 2, 0)), 0, 0)
    return pl.pallas_call(
        _attn_body, out_shape=SDS((bsz, seq, att_w), BF16), grid=(bsz, nb),
        in_specs=[BS(memory_space=pltpu.SMEM), BS((1, Q_BLOCK, att_w), cur),
                  BS(kvb, prev), BS(kvb, cur), BS(kvb, nxt), BS(kvb, prev), BS(kvb, cur), BS(kvb, nxt),
                  BS((1, n_ctx, kv_w), per_b), BS((1, n_ctx, kv_w), per_b), BS((1, att_w), lambda b, i: (0, 0)),
                  BS((1, rows, 3 * Q_BLOCK), edge_case)],
        out_specs=BS((1, Q_BLOCK, att_w), cur), compiler_params=_cparams(2), name="window_attn",
    )(sink, q, k, k, k, v, v, v, kx, vx, gw, jnp.asarray(_band_bias(rows)))


def _filter_body(f_ref, w1h, w1l, b1, w2h, w2l, b2, w3h, w3l, sf_ref, dl_ref, o_ref, *, hy_w):
    def mm_w(a, wh, wl):
        a_hi, a_lo = _split(a)
        return _mm(a_hi, wh[...]) + _mm(a_hi, wl[...]) + _mm(a_lo, wh[...])

    f = f_ref[...]
    h = jnp.sin(sf_ref[0:1, :] * (mm_w(f, w1h, w1l) + b1[...]))
    h = jnp.sin(sf_ref[1:2, :] * (mm_w(h, w2h, w2l) + b2[...]))
    h = mm_w(h, w3h, w3l)
    h = h * jnp.exp(-f[:, 0:1] * dl_ref[...])
    tl = f.shape[0]
    row = lax.broadcasted_iota(I32, h.shape, 0) + pl.program_id(0) * tl
    col = lax.broadcasted_iota(I32, h.shape, 1)
    is_bwd = ((col // hy_w) & 1) == 1
    h = jnp.where((row == 0) & is_bwd, 0.0, h)
    for j in range(o_ref.shape[0]):
        o_ref[j] = h[:, LANES * j:LANES * (j + 1)]


def _filter_mlp(seq, w1, b1, w2, b2, w3, sin_freq, hy_w):
    hid = w1.shape[1]
    fw = 64
    feat = jnp.asarray(_filter_features(seq, fw))
    w1p = jnp.zeros((fw, hid), F32).at[:w1.shape[0]].set(w1)
    n_out = w3.shape[1]
    delta = jnp.asarray(np.tile(_decay_rates(hy_w), n_out // hy_w)).reshape(1, n_out)
    tl = FILTER_TILE
    fixed = lambda i: (0, 0)
    ops = []
    for w in (w1p, w2, w3):
        ops.extend(_split(w))
    return pl.pallas_call(
        functools.partial(_filter_body, hy_w=hy_w),
        out_shape=SDS((n_out // LANES, seq, LANES), F32), grid=(seq // tl,),
        in_specs=[BS((tl, fw), lambda i: (i, 0)), BS((fw, hid), fixed), BS((fw, hid), fixed), BS((1, hid), fixed),
                  BS((hid, hid), fixed), BS((hid, hid), fixed), BS((1, hid), fixed),
                  BS((hid, n_out), fixed), BS((hid, n_out), fixed), BS((2, hid), fixed), BS((1, n_out), fixed)],
        out_specs=BS((n_out // LANES, tl, LANES), lambda i: (0, i, 0)),
        compiler_params=_cparams(1), name="hyena_filter_mlp",
    )(feat, ops[0], ops[1], b1.reshape(1, hid), ops[2], ops[3], b2.reshape(1, hid), ops[4], ops[5], sin_freq, delta)


C_TILES = 2
C_GROUP = C_TILES * LANES


def _load_cat(ref, row0, n_rows):
    return jnp.concatenate([ref[c, pl.ds(row0, n_rows), :] for c in range(C_TILES)], axis=1)


def _store_cat(ref, row0, n_rows, val):
    for c in range(C_TILES):
        ref[c, pl.ds(row0, n_rows), :] = val[:, c * LANES:(c + 1) * LANES]


def _fwd_major(src, p_ref, ak, passes):
    half = FFT_MAJOR // 2

    def body(g, carry):
        r0 = pl.multiple_of(g * SUBLANES, SUBLANES)
        st = jnp.concatenate([_load_cat(src, n2 * FFT_MINOR + r0, SUBLANES) for n2 in range(half)], axis=0)
        out = _mm_split(ak[0][...], ak[1][...] if passes == 3 else None, st, passes)
        for p in range(FFT_MAJOR):
            _store_cat(p_ref, p * FFT_MINOR + r0, SUBLANES, out[p * SUBLANES:(p + 1) * SUBLANES])
        return carry

    lax.fori_loop(0, FFT_MINOR // SUBLANES, body, 0, unroll=4)


def _inv_major(p_ref, dst, dk, passes):
    half = FFT_MAJOR // 2

    def body(g, carry):
        r0 = pl.multiple_of(g * SUBLANES, SUBLANES)
        st = jnp.concatenate([_load_cat(p_ref, p * FFT_MINOR + r0, SUBLANES) for p in range(FFT_MAJOR)], axis=0)
        out = _mm_split(dk[0][...], dk[1][...] if passes == 3 else None, st, passes)
        for n2 in range(half):
            _store_cat(dst, n2 * FFT_MINOR + r0, SUBLANES, out[n2 * SUBLANES:(n2 + 1) * SUBLANES])
        return carry

    lax.fori_loop(0, FFT_MINOR // SUBLANES, body, 0, unroll=4)


def _slab_spectrum(p_ref, mf, s, passes, real_slot=None):
    m = FFT_MINOR
    if real_slot is None:
        st = jnp.concatenate([_load_cat(p_ref, s * m, m), _load_cat(p_ref, (FFT_MAJOR // 2 + s) * m, m)], axis=0)
        x = _mm_split(mf[0][s], mf[1][s] if passes == 3 else None, st, passes)
    else:
        st = _load_cat(p_ref, real_slot * m, m)
        x = _mm_split(mf[0][s, :, 0:m], mf[1][s, :, 0:m] if passes == 3 else None, st, passes)
    return x[:m], x[m:]


def _spectrum_body(f_ref, b_ref, akh, akl, mfh, mfl, o_ref, p_ref):
    half = FFT_MAJOR // 2
    ak, mf = (akh, akl), (mfh, mfl)
    for is_bwd, src in ((False, f_ref), (True, b_ref)):
        _fwd_major(src, p_ref, ak, 3)

        def put(s, re, im, is_bwd=is_bwd):
            if is_bwd:
                o_ref[0, 0, s, 0] = o_ref[0, 0, s, 0] + re
                o_ref[0, 0, s, 1] = o_ref[0, 0, s, 1] - im
            else:
                o_ref[0, 0, s, 0] = re
                o_ref[0, 0, s, 1] = im

        put(0, *_slab_spectrum(p_ref, mf, 0, 3, real_slot=0))
        put(half, *_slab_spectrum(p_ref, mf, half, 3, real_slot=half))

        def body(s, carry):
            put(s, *_slab_spectrum(p_ref, mf, s, 3))
            return carry

        lax.fori_loop(1, half, body, 0, unroll=8)


def _spectrum(hf, seq, hy_w):
    ak, mf = _dft_hi_lo("ak"), _dft_hi_lo("mf")
    n_grp = hy_w // C_GROUP
    fixed2 = lambda o, c: (0, 0)
    fixed3 = lambda o, c: (0, 0, 0)
    one = pl.Buffered(1)
    return pl.pallas_call(
        _spectrum_body,
        out_shape=SDS((HYENA_ORDER, n_grp, N_SLABS, 2, FFT_MINOR, C_GROUP), F32), grid=(HYENA_ORDER, n_grp),
        in_specs=[BS((C_TILES, seq, LANES), lambda o, c: (o * 2 * n_grp + c, 0, 0), pipeline_mode=one),
                  BS((C_TILES, seq, LANES), lambda o, c: (o * 2 * n_grp + n_grp + c, 0, 0), pipeline_mode=one),
                  BS(ak[0].shape, fixed2, pipeline_mode=one), BS(ak[0].shape, fixed2, pipeline_mode=one),
                  BS(mf[0].shape, fixed3, pipeline_mode=one), BS(mf[0].shape, fixed3, pipeline_mode=one)],
        out_specs=BS((1, 1, N_SLABS, 2, FFT_MINOR, C_GROUP), lambda o, c: (o, c, 0, 0, 0, 0)),
        scratch_shapes=[pltpu.VMEM((C_TILES, FFT_MAJOR * FFT_MINOR, LANES), F32)],
        compiler_params=_cparams(2), name="hyena_spectrum",
    )(hf, hf, ak[0], ak[1], mf[0], mf[1])


CONV_CHUNK = 128
CONV_PASSES = 1


def _sconv_chunk(ref, c, r0, n_rows, seq, prm_ref, base):
    cl = slice(c * LANES, (c + 1) * LANES)
    cur = ref[c, pl.ds(r0, n_rows), :]
    row = lax.broadcasted_iota(I32, cur.shape, 0)
    before = ref[c, pl.ds(jnp.maximum(r0 - 1, 0), 1), :] * (r0 > 0).astype(F32)
    after = ref[c, pl.ds(jnp.minimum(r0 + n_rows, seq - 1), 1), :] * (r0 + n_rows < seq).astype(F32)
    prev = jnp.where(row == 0, before, pltpu.roll(cur, 1, axis=0))
    nxt = jnp.where(row == n_rows - 1, after, pltpu.roll(cur, n_rows - 1, axis=0))
    w = lambda k: prm_ref[0, 0, base + k:base + k + 1, cl]
    return w(3) + w(0) * prev + w(1) * cur + w(2) * nxt


def _conv_body(z_ref, x_ref, prm_ref, akh, dkh, mfh, mih, h_ref, o_ref, zs_ref, p_ref, *, conv_z):
    seq = z_ref.shape[2]
    half = FFT_MAJOR // 2
    m = FFT_MINOR
    zv, xv = z_ref.at[0], x_ref.at[0]
    ak, dk, mf, mi = (akh, None), (dkh, None), (mfh, None), (mih, None)
    n_chunks = seq // CONV_CHUNK

    if conv_z:
        def prep(i, carry):
            r0 = pl.multiple_of(i * CONV_CHUNK, CONV_CHUNK)
            for c in range(C_TILES):
                zs_ref[c, pl.ds(r0, CONV_CHUNK), :] = _sconv_chunk(zv, c, r0, CONV_CHUNK, seq, prm_ref, 0)
            return carry

        lax.fori_loop(0, n_chunks, prep, 0)
        src = zs_ref
    else:
        src = zv
    _fwd_major(src, p_ref, ak, CONV_PASSES)

    def slab(s, real_slot=None):
        xr, xi = _slab_spectrum(p_ref, mf, s, CONV_PASSES, real_slot)
        hr, hi = h_ref[0, 0, s, 0], h_ref[0, 0, s, 1]
        y = jnp.concatenate([xr * hr - xi * hi, xr * hi + xi * hr], axis=0)
        if real_slot is None:
            out = _mm_split(mi[0][s], None, y, CONV_PASSES)
            _store_cat(p_ref, s * m, m, out[:m])
            _store_cat(p_ref, (half + s) * m, m, out[m:])
        else:
            out = _mm_split(mi[0][s, 0:m, :], None, y, CONV_PASSES)
            _store_cat(p_ref, real_slot * m, m, out)

    slab(0, real_slot=0)
    slab(half, real_slot=half)

    def slab_loop(s, carry):
        slab(s)
        return carry

    lax.fori_loop(1, half, slab_loop, 0, unroll=8)
    _inv_major(p_ref, zs_ref, dk, CONV_PASSES)

    def fin(i, carry):
        r0 = pl.multiple_of(i * CONV_CHUNK, CONV_CHUNK)
        for c in range(C_TILES):
            cl = slice(c * LANES, (c + 1) * LANES)
            gate = _sconv_chunk(xv, c, r0, CONV_CHUNK, seq, prm_ref, 4)
            if conv_z:
                zval = _sconv_chunk(zv, c, r0, CONV_CHUNK, seq, prm_ref, 0)
            else:
                zval = zv[c, pl.ds(r0, CONV_CHUNK), :]
            skip = prm_ref[0, 0, 8:9, cl]
            o_ref[0, c, pl.ds(r0, CONV_CHUNK), :] = gate * (zs_ref[c, pl.ds(r0, CONV_CHUNK), :] + skip * zval)
        return carry

    lax.fori_loop(0, n_chunks, fin, 0, unroll=4)


def _hyena_order(z, z_grp0, x, x_grp0, prm, order, spec, conv_z, n_grp):
    bsz, _, seq, _ = z.shape
    ak, dk, mf, mi = (_dft_hi_lo(n)[0] for n in ("ak", "dk", "mf", "mi"))
    one = pl.Buffered(1)
    fixed2 = lambda c, b: (0, 0)
    fixed3 = lambda c, b: (0, 0, 0)
    blk = (1, C_TILES, seq, LANES)
    return pl.pallas_call(
        functools.partial(_conv_body, conv_z=conv_z),
        out_shape=SDS((bsz, n_grp * C_TILES, seq, LANES), F32), grid=(n_grp, bsz),
        in_specs=[BS(blk, lambda c, b: (b, z_grp0 + c, 0, 0)),
                  BS(blk, lambda c, b: (b, x_grp0 + c, 0, 0)),
                  BS((1, 1, 16, C_GROUP), lambda c, b: (order, c, 0, 0)),
                  BS(ak.shape, fixed2, pipeline_mode=one), BS(dk.shape, fixed2, pipeline_mode=one),
                  BS(mf.shape, fixed3, pipeline_mode=one), BS(mi.shape, fixed3, pipeline_mode=one),
                  BS((1, 1, N_SLABS, 2, FFT_MINOR, C_GROUP), lambda c, b: (order, c, 0, 0, 0, 0), pipeline_mode=one)],
        out_specs=BS(blk, lambda c, b: (b, c, 0, 0)),
        scratch_shapes=[pltpu.VMEM((C_TILES, seq, LANES), F32),
                        pltpu.VMEM((C_TILES, FFT_MAJOR * FFT_MINOR, LANES), F32)],
        compiler_params=_cparams(2), name=f"hyena_conv{order}",
    )(z, x, prm, ak, dk, mf, mi, spec)


ROUTER_ROWS = 48
EXPERT_ROW0 = 8


def _outproj_body(att_ref, hy_ref, gwh_ref, wo_ref, x_ref, g1_ref, sh2_ref, sc2_ref, n2w_ref, wrh_ref, wrl_ref,
                  br_ref, tri_ref, x1_ref, h2_ref, ids_ref, gates_ref, cnt_ref, carry_ref):
    @pl.when((pl.program_id(0) == 0) & (pl.program_id(1) == 0))
    def _():
        carry_ref[...] = jnp.zeros_like(carry_ref)

    hy = jnp.concatenate([hy_ref[0, j] for j in range(hy_ref.shape[1])], axis=1)
    hyn = _rms(hy) * gwh_ref[...]
    mix = _mm(jnp.concatenate([att_ref[0], hyn.astype(BF16)], axis=1), wo_ref[...])
    x1 = x_ref[0] + g1_ref[0] * mix
    x1_ref[0] = x1
    h2 = _rms(x1) * n2w_ref[...]
    h2 = h2 * (1.0 + sc2_ref[0]) + sh2_ref[0]
    h2_ref[...] = h2

    h_hi, h_lo = _split(h2)
    lg = _mm_nt(wrh_ref[...], h_hi) + _mm_nt(wrh_ref[...], h_lo) + _mm_nt(wrl_ref[...], h_hi) + br_ref[...]
    tm = lg.shape[1]
    gl = lg[0:N_GROUPS]
    el = lg[EXPERT_ROW0:EXPERT_ROW0 + N_EXPERTS]
    gmax = jnp.max(gl, axis=0, keepdims=True)
    rg = lax.broadcasted_iota(I32, gl.shape, 0).astype(F32)
    g_idx = jnp.min(jnp.where(gl == gmax, rg, float(N_GROUPS)), axis=0, keepdims=True)
    g_val = 1.0 / jnp.sum(jnp.exp(gl - gmax), axis=0, keepdims=True)
    re_i = lax.broadcasted_iota(I32, el.shape, 0)
    re = re_i.astype(F32)
    in_group = (re_i // EXPERTS_PER_GROUP).astype(F32) == g_idx
    elm = jnp.where(in_group, el, NEG_INF)
    m1 = jnp.max(elm, axis=0, keepdims=True)
    i1 = jnp.min(jnp.where(elm == m1, re, float(N_EXPERTS)), axis=0, keepdims=True)
    elm2 = jnp.where(re == i1, NEG_INF, elm)
    m2 = jnp.max(elm2, axis=0, keepdims=True)
    i2 = jnp.min(jnp.where(elm2 == m2, re, float(N_EXPERTS)), axis=0, keepdims=True)
    e2 = jnp.exp(m2 - m1)
    inv = g_val / (1.0 + e2)

    oh1 = (re == i1).astype(F32)
    oh2 = (re == i2).astype(F32)
    oh = oh1 + oh2
    base = _mm(oh.astype(BF16), tri_ref[...]) + carry_ref[:, 0:1]
    rank1 = jnp.sum(oh1 * base, axis=0, keepdims=True)
    rank2 = jnp.sum(oh2 * base, axis=0, keepdims=True)
    carry_ref[...] = carry_ref[...] + jnp.sum(oh, axis=1, keepdims=True)
    cnt_ref[...] = carry_ref[...]

    zi = jnp.zeros((1, tm), I32)
    for k, val in enumerate((i1.astype(I32), i2.astype(I32), rank1.astype(I32), rank2.astype(I32), zi, zi, zi, zi)):
        ids_ref[k:k + 1, :] = val
    zf = jnp.zeros((1, tm), F32)
    for k, val in enumerate((inv, inv * e2, zf, zf, zf, zf, zf, zf)):
        gates_ref[k:k + 1, :] = val


def _outproj_route(att_n, hy, gw_hy, wo_bf, x, g1, sh2, sc2, norm2_w, wr_hi, wr_lo, br):
    bsz, seq, d = x.shape
    tm = ROW_TILE
    n_tok = bsz * seq
    att_w = att_n.shape[2]
    hy_tiles = hy.shape[1]
    tri = jnp.asarray(np.triu(np.ones((tm, tm), np.float32), k=1).astype(ml_dtypes.bfloat16))
    row = lambda b, i: (b, i, 0)
    per_b = lambda b, i: (b, 0, 0)
    fixed2 = lambda b, i: (0, 0)
    tok = lambda b, i: (0, b * (seq // tm) + i)
    return pl.pallas_call(
        _outproj_body,
        out_shape=(SDS((bsz, seq, d), F32), SDS((n_tok, d), F32), SDS((SUBLANES, n_tok), I32),
                   SDS((SUBLANES, n_tok), F32), SDS((N_EXPERTS, LANES), F32)),
        grid=(bsz, seq // tm),
        in_specs=[BS((1, tm, att_w), row), BS((1, hy_tiles, tm, LANES), lambda b, i: (b, 0, i, 0)),
                  BS((1, hy_tiles * LANES), fixed2), BS(wo_bf.shape, fixed2), BS((1, tm, d), row),
                  BS((1, 1, d), per_b), BS((1, 1, d), per_b), BS((1, 1, d), per_b), BS((1, d), fixed2),
                  BS((ROUTER_ROWS, d), fixed2), BS((ROUTER_ROWS, d), fixed2), BS((ROUTER_ROWS, 1), fixed2),
                  BS((tm, tm), fixed2)],
        out_specs=(BS((1, tm, d), row), BS((tm, d), lambda b, i: (b * (seq // tm) + i, 0)),
                   BS((SUBLANES, tm), tok), BS((SUBLANES, tm), tok), BS((N_EXPERTS, LANES), fixed2)),
        scratch_shapes=[pltpu.VMEM((N_EXPERTS, LANES), F32)],
        compiler_params=_cparams(2), name="out_proj_router",
    )(att_n, hy, gw_hy, wo_bf, x, g1, sh2, sc2, norm2_w.reshape(1, d), wr_hi, wr_lo, br, tri)


ISSUE_UNROLL = 8


def _row_copy(src, src_row, dst, dst_row, sem):
    return pltpu.make_async_copy(src.at[pl.ds(src_row, 1), :], dst.at[pl.ds(dst_row, 1), :], sem)


def _dispatch_body(pend_ref, d_ref, h_ref, buf_ref, zero_ref, sem):
    n = h_ref.shape[0]

    @pl.when(pl.program_id(0) == 0)
    def _():
        zero_ref[...] = jnp.zeros_like(zero_ref)

        def tail_copy(e):
            row0 = pl.multiple_of(pend_ref[e] - MOE_BLOCK, MOE_BLOCK)
            return pltpu.make_async_copy(zero_ref, buf_ref.at[pl.ds(row0, MOE_BLOCK), :], sem)

        def non_empty(e):
            return pend_ref[e] > (pend_ref[e - 1] if e > 0 else 0)

        def spare_copy(i):
            row0 = pl.multiple_of(i * MOE_BLOCK, MOE_BLOCK)
            return pltpu.make_async_copy(zero_ref, buf_ref.at[pl.ds(row0, MOE_BLOCK), :], sem)

        first_spare = pend_ref[N_EXPERTS - 1] // MOE_BLOCK
        n_blocks = buf_ref.shape[0] // MOE_BLOCK
        for e in range(N_EXPERTS):
            @pl.when(non_empty(e))
            def _(e=e):
                tail_copy(e).start()
        lax.fori_loop(first_spare, n_blocks, lambda i, c: (spare_copy(i).start(), c)[1], 0)
        for e in range(N_EXPERTS):
            @pl.when(non_empty(e))
            def _(e=e):
                tail_copy(e).wait()
        lax.fori_loop(first_spare, n_blocks, lambda i, c: (spare_copy(i).wait(), c)[1], 0)

    def issue(t, carry):
        for j in range(TOP_K):
            _row_copy(h_ref, t, buf_ref, d_ref[j, t], sem).start()
        return carry

    lax.fori_loop(0, n, issue, 0, unroll=ISSUE_UNROLL)
    for j in range(TOP_K):
        pltpu.make_async_copy(h_ref, buf_ref.at[pl.ds(0, n), :], sem).wait()


def _dispatch(pend, dest, h2, n_rows):
    n_tok, d = h2.shape
    tm = DISPATCH_TILE
    grid_spec = pltpu.PrefetchScalarGridSpec(
        num_scalar_prefetch=1, grid=(n_tok // tm,),
        in_specs=[BS((TOP_K, tm), lambda i, pend: (0, i), memory_space=pltpu.SMEM),
                  BS((tm, d), lambda i, pend: (i, 0))],
        out_specs=BS(memory_space=pl.ANY),
        scratch_shapes=[pltpu.VMEM((MOE_BLOCK, d), F32), pltpu.SemaphoreType.DMA(())])
    return pl.pallas_call(_dispatch_body, out_shape=SDS((n_rows, d), F32), grid_spec=grid_spec,
                          compiler_params=_cparams(1), name="moe_dispatch")(pend, dest, h2)


def _expert_body(be_ref, nu_ref, x_ref, w1_ref, w3_ref, w2_ref, y_ref, w13_s, w2_s):
    i = pl.program_id(0)
    hidden = w2_ref.shape[1]
    active = i < nu_ref[0]
    new_expert = jnp.logical_or(i == 0, be_ref[i] != be_ref[jnp.maximum(i - 1, 0)])

    @pl.when(jnp.logical_and(active, new_expert))
    def _():
        w13_s[:, :hidden] = w1_ref[0].astype(BF16)
        w13_s[:, hidden:] = w3_ref[0].astype(BF16)
        w2_s[...] = w2_ref[0].astype(BF16)

    @pl.when(active)
    def _():
        h = _mm(x_ref[...].astype(BF16), w13_s[...])
        act = _silu(h[:, :hidden]) * h[:, hidden:]
        y_ref[...] = _mm(act.astype(BF16), w2_s[...])

    @pl.when(jnp.logical_not(active))
    def _():
        y_ref[...] = jnp.zeros_like(y_ref)


def _experts(block_expert, n_used, buf, w1, w3, w2):
    n_rows, d = buf.shape
    hidden = w2.shape[1]
    nb = n_rows // MOE_BLOCK
    by_expert = lambda i, be, nu: (be[i], 0, 0)
    grid_spec = pltpu.PrefetchScalarGridSpec(
        num_scalar_prefetch=2, grid=(nb,),
        in_specs=[BS((MOE_BLOCK, d), lambda i, be, nu: (jnp.minimum(i, nu[0] - 1), 0)),
                  BS((1, d, hidden), by_expert), BS((1, d, hidden), by_expert), BS((1, hidden, d), by_expert)],
        out_specs=BS((MOE_BLOCK, d), lambda i, be, nu: (i, 0)),
        scratch_shapes=[pltpu.VMEM((d, 2 * hidden), BF16), pltpu.VMEM((hidden, d), BF16)])
    return pl.pallas_call(_expert_body, out_shape=SDS((n_rows, d), F32), grid_spec=grid_spec,
                          compiler_params=_cparams(1), name="moe_experts")(block_expert, n_used, buf, w1, w3, w2)


def _combine_body(d_ref, dn_ref, y_ref, x1_ref, g2_ref, gt_ref, o_ref, rows_ref, sem):
    i = pl.program_id(0)
    n = x1_ref.shape[0]
    slot = i % 2

    def issue_tile(dest_ref, s):
        def issue(t, carry):
            for j in range(TOP_K):
                _row_copy(y_ref, dest_ref[j, t], rows_ref.at[s, j], t, sem.at[s]).start()
            return carry

        lax.fori_loop(0, n, issue, 0, unroll=ISSUE_UNROLL)

    @pl.when(i == 0)
    def _():
        issue_tile(d_ref, 0)

    @pl.when(i < pl.num_programs(0) - 1)
    def _():
        issue_tile(dn_ref, 1 - slot)

    for j in range(TOP_K):
        pltpu.make_async_copy(y_ref.at[pl.ds(0, n), :], rows_ref.at[slot, j], sem.at[slot]).wait()
    gt = gt_ref[...]
    moe = gt[:, 0:1] * rows_ref[slot, 0] + gt[:, 1:2] * rows_ref[slot, 1]
    o_ref[...] = x1_ref[...] + g2_ref[0] * moe


def _combine(dest, y, x1, g2, gates_t, seq):
    n_tok, d = x1.shape
    tm = COMBINE_TILE
    steps = n_tok // tm
    return pl.pallas_call(
        _combine_body, out_shape=SDS((n_tok, d), F32), grid=(steps,),
        in_specs=[BS((TOP_K, tm), lambda i: (0, i), memory_space=pltpu.SMEM),
                  BS((TOP_K, tm), lambda i: (0, jnp.minimum(i + 1, steps - 1)), memory_space=pltpu.SMEM),
                  BS(memory_space=pl.ANY),
                  BS((tm, d), lambda i: (i, 0)), BS((1, 1, d), lambda i: (i // (seq // tm), 0, 0)),
                  BS((tm, TOP_K), lambda i: (i, 0))],
        out_specs=BS((tm, d), lambda i: (i, 0)),
        scratch_shapes=[pltpu.VMEM((2, TOP_K, tm, d), F32), pltpu.SemaphoreType.DMA((2,))],
        compiler_params=_cparams(1), name="moe_combine",
    )(dest, dest, y, x1, g2, gates_t)


def kernel(x, c, ctx, c_ctx, ada_w, ada_b, norm1_w, w_in, q_norm_w, k_norm_w, attn_sink, hy_conv_w, hy_conv_b, hy_w1,
           hy_b1, hy_w2, hy_b2, hy_w3, hy_sin_freq, hy_skip, group_norm_w, w_out, norm2_w, router_g_w, router_g_b,
           router_e_w, router_e_b, exp_w1, exp_w3, exp_w2):
    bsz, seq, d = x.shape
    assert ada_w.shape[0] == 1, "single-layer block"
    att_w = d // 2
    hy_w = d - att_w
    kv_w = N_KV_HEADS * HEAD_DIM
    n_heads = att_w // HEAD_DIM
    n_grp = hy_w // C_GROUP
    assert 2 * seq == FFT_MAJOR * FFT_MINOR and seq % ROW_TILE == 0 and hy_w % C_GROUP == 0
    assert w_in.shape[2] == att_w + 2 * kv_w + (HYENA_ORDER + 1) * hy_w

    pad = (-(bsz + 1)) % SUBLANES
    cc = jnp.concatenate([c, c_ctx[None, :], jnp.zeros((pad, d), F32)], axis=0)
    mod = _ada(cc, ada_w[0], ada_b[0])
    lat = [mod[:bsz, k * d:(k + 1) * d].reshape(bsz, 1, d) for k in range(6)]
    sh1, sc1, g1, sh2, sc2, g2 = lat
    csh1 = mod[bsz:bsz + 1, 0:d]
    csc1 = mod[bsz:bsz + 1, d:2 * d]

    w_in_bf = w_in[0].astype(BF16)
    qnw = jnp.tile(q_norm_w[0], n_heads).reshape(1, att_w)
    knw = jnp.tile(k_norm_w[0], N_KV_HEADS).reshape(1, kv_w)
    q, k, v, u = _inproj(x, sh1, sc1, norm1_w[0], w_in_bf, qnw, knw, att_w, kv_w)
    kx, vx = _ctxproj(ctx, csh1, csc1, norm1_w[0], w_in_bf[:, att_w:att_w + 2 * kv_w], knw, kv_w)

    gw = group_norm_w[0]
    att_n = _attention(attn_sink[0], q, k, v, kx, vx, gw[:att_w].reshape(1, att_w))

    hf = _filter_mlp(seq, hy_w1[0], hy_b1[0], hy_w2[0], hy_b2[0], hy_w3[0], hy_sin_freq[0], hy_w)
    spec = _spectrum(hf, seq, hy_w)
    cw = hy_conv_w[0].reshape(3, HYENA_ORDER + 1, n_grp, C_GROUP)
    cb = hy_conv_b[0].reshape(1, HYENA_ORDER + 1, n_grp, C_GROUP)
    taps = jnp.concatenate([cw, cb], axis=0)
    prm = jnp.zeros((HYENA_ORDER, n_grp, 16, C_GROUP), F32)
    for o in range(HYENA_ORDER):
        prm = prm.at[o, :, 4:8].set(jnp.transpose(taps[:, o + 1], (1, 0, 2)))
        prm = prm.at[o, :, 8].set(hy_skip[0, o].reshape(n_grp, C_GROUP))
    prm = prm.at[0, :, 0:4].set(jnp.transpose(taps[:, 0], (1, 0, 2)))
    z = _hyena_order(u, 0, u, n_grp, prm, 0, spec, True, n_grp)
    hy = _hyena_order(z, 0, u, 2 * n_grp, prm, 1, spec, False, n_grp)

    wr = jnp.zeros((ROUTER_ROWS, d), F32)
    wr = wr.at[0:N_GROUPS].set(router_g_w[0].T).at[EXPERT_ROW0:EXPERT_ROW0 + N_EXPERTS].set(router_e_w[0].T)
    br = jnp.zeros((ROUTER_ROWS, 1), F32)
    br = br.at[0:N_GROUPS, 0].set(router_g_b[0]).at[EXPERT_ROW0:EXPERT_ROW0 + N_EXPERTS, 0].set(router_e_b[0])
    wr_hi, wr_lo = _split(wr)
    x1, h2, ids, gates, counts = _outproj_route(att_n, hy, gw[att_w:].reshape(1, hy_w), w_out[0].astype(BF16), x,
                                                g1, sh2, sc2, norm2_w[0], wr_hi, wr_lo, br)

    n_tok = bsz * seq
    cnt = counts[:, 0].astype(I32)
    padded = (cnt + MOE_BLOCK - 1) // MOE_BLOCK * MOE_BLOCK
    pend = jnp.cumsum(padded)
    pstart = pend - padded
    e_iota = jnp.arange(N_EXPERTS, dtype=I32)
    is_e = ids[0:TOP_K, :, None] == e_iota
    dest = jnp.sum(jnp.where(is_e, pstart, 0), axis=-1) + ids[TOP_K:2 * TOP_K]
    n_blocks = (n_tok * TOP_K) // MOE_BLOCK + N_EXPERTS
    block_row0 = jnp.arange(n_blocks, dtype=I32) * MOE_BLOCK
    block_expert = jnp.minimum(jnp.sum((block_row0[:, None] >= pend[None, :]).astype(I32), axis=-1), N_EXPERTS - 1)
    n_used = (pend[-1:] // MOE_BLOCK).astype(I32)

    buf = _dispatch(pend.astype(I32), dest, h2, n_blocks * MOE_BLOCK)
    y = _experts(block_expert, n_used, buf, exp_w1[0], exp_w3[0], exp_w2[0])
    out = _combine(dest, y, x1.reshape(n_tok, d), g2, gates[0:TOP_K].T, seq)
    return out.reshape(bsz, seq, d)
```
